```python
import jax
import jax.numpy as jnp
from jax import lax
import numpy as np

D_MODEL = 1024
BATCH = 2
SEQ = 8192
DEPTH = 2

CHUNK = 64
N_AB = (DEPTH + 1) // 2
N_C = DEPTH // 2
RMS_EPS = 1e-6
RG_WIDTH = D_MODEL
RG_BLOCKS = 16
RG_BLOCK = RG_WIDTH // RG_BLOCKS
RG_CONV = 4
RG_C = 8.0
HG_HEADS = 8
HG_DK = D_MODEL // HG_HEADS
HG_DV = D_MODEL // HG_HEADS
HG_KW = HG_HEADS * HG_DK
HG_VW = HG_HEADS * HG_DV
AB_SPLITS = (RG_WIDTH, RG_WIDTH, HG_KW, HG_KW, HG_VW, HG_VW)
AB_IN = sum(AB_SPLITS)
AB_OUT = RG_WIDTH + HG_VW
RW_HEAD = 64
RW_HEADS = D_MODEL // RW_HEAD
RW_WIDTH = RW_HEADS * RW_HEAD
RW_DECAY_LORA = 64
RW_AAA_LORA = 64
RW_GN_EPS = 64e-5

kernel_name = "hybrid_rglru_hgrn2_rwkv7_trunk"


def _f32(t):
    return t.astype(jnp.float32)


def rms_norm(x, gain):
    xf = _f32(x)
    return xf * lax.rsqrt(jnp.mean(xf * xf, axis=-1, keepdims=True) + RMS_EPS) * _f32(gain)


def _linear_scan_combine(left, right):
    a_l, b_l = left
    a_r, b_r = right
    return a_l * a_r, a_r * b_l + b_r


def rglru_mixer(xb, conv_w, conv_b, w_a, b_a, w_x, b_x, lam):
    bsz, seq, _ = xb.shape
    xc = lax.conv_general_dilated(xb, conv_w[:, None, :], window_strides=(1,),
                                  padding=[(RG_CONV - 1, 0)],
                                  dimension_numbers=("NWC", "WIO", "NWC"),
                                  feature_group_count=RG_WIDTH) + conv_b
    xblk = xc.reshape(bsz, seq, RG_BLOCKS, RG_BLOCK)
    gate_r = jax.nn.sigmoid(jnp.einsum("bsni,nij->bsnj", xblk, w_a).reshape(bsz, seq, RG_WIDTH) + b_a)
    gate_i = jax.nn.sigmoid(jnp.einsum("bsni,nij->bsnj", xblk, w_x).reshape(bsz, seq, RG_WIDTH) + b_x)
    log_a = -RG_C * gate_r * jax.nn.softplus(-lam)
    a = jnp.exp(log_a)
    b = jnp.sqrt(-jnp.expm1(2.0 * log_a)) * (gate_i * xc)
    _, h = lax.associative_scan(_linear_scan_combine, (a, b), axis=1)
    return h


def hgrn2_mixer(q, f_raw, v, lb):
    bsz, seq, _ = q.shape
    n_chunks = seq // CHUNK
    q = jax.nn.silu(q)
    log_f = jnp.logaddexp(jnp.log(lb), jnp.log1p(-lb) + jax.nn.log_sigmoid(f_raw))
    k = -jnp.expm1(log_f)

    def to_chunks(t, d):
        return t.reshape(bsz, n_chunks, CHUNK, HG_HEADS, d).transpose(1, 0, 3, 2, 4)

    qc = to_chunks(q, HG_DK)
    kc = to_chunks(k, HG_DK)
    vc = to_chunks(v, HG_DV)
    gc = jnp.cumsum(to_chunks(log_f, HG_DK), axis=3)
    causal = jnp.tril(jnp.ones((CHUNK, CHUNK), dtype=bool))[None, None, :, :, None]

    def chunk_step(state, inp):
        q_c, k_c, v_c, g_c = inp
        rel = jnp.where(causal, g_c[:, :, :, None, :] - g_c[:, :, None, :, :], -jnp.inf)
        scores = jnp.einsum("bhtd,bhsd,bhtsd->bhts", q_c, k_c, jnp.exp(rel))
        out = (jnp.einsum("bhts,bhsv->bhtv", scores, v_c)
               + jnp.einsum("bhtd,bhdv->bhtv", q_c * jnp.exp(g_c), state))
        g_end = g_c[:, :, -1, :]
        new_state = (state * jnp.exp(g_end)[..., None]
                     + jnp.einsum("bhsd,bhsv->bhdv", k_c * jnp.exp(g_end[:, :, None, :] - g_c), v_c))
        return new_state, out

    state0 = jnp.zeros((bsz, HG_HEADS, HG_DK, HG_DV), jnp.float32)
    _, oc = lax.scan(chunk_step, state0, (qc, kc, vc, gc))
    return oc.transpose(1, 0, 3, 2, 4).reshape(bsz, seq, HG_HEADS, HG_DV)


def rglru_hgrn2_layer(u, w_in, w_out, conv_w, conv_b, w_a, b_a, w_x, b_x, lam, lb, g_norm):
    bsz, seq, _ = u.shape
    w_in, w_out, conv_w, conv_b, w_a, b_a, w_x, b_x, lam, g_norm = map(
        _f32, (w_in, w_out, conv_w, conv_b, w_a, b_a, w_x, b_x, lam, g_norm))
    proj = jnp.einsum("bsd,de->bse", u, w_in)
    rg_x, rg_g, hg_q, hg_f, hg_i, hg_g = jnp.split(proj, np.cumsum(AB_SPLITS)[:-1].tolist(), axis=-1)
    h = rglru_mixer(rg_x, conv_w, conv_b, w_a, b_a, w_x, b_x, lam)
    o = hgrn2_mixer(hg_q, hg_f, hg_i, lb)
    o = o * lax.rsqrt(jnp.mean(o * o, axis=-1, keepdims=True) + RMS_EPS) * g_norm
    merged = jnp.concatenate([h * jax.nn.silu(rg_g),
                              o.reshape(bsz, seq, HG_VW) * jax.nn.silu(hg_g)], axis=-1)
    return merged @ w_out


def rwkv7_layer(u, mu, w_in, w_out, w0, w1, w2, a0, a1, a2, k_k, k_a, r_k, ln_w, ln_b):
    bsz, seq, _ = u.shape
    mu, w_in, w_out, w0, w1, w2, a0, a1, a2, k_k, k_a, r_k, ln_w, ln_b = map(
        _f32, (mu, w_in, w_out, w0, w1, w2, a0, a1, a2, k_k, k_a, r_k, ln_w, ln_b))
    shifted = jnp.pad(u, ((0, 0), (1, 0), (0, 0)))[:, :-1]
    mixes = u[None] + (shifted - u)[None] * mu[:, None, None, :]
    r, k, v, g = jnp.einsum("pbsd,pdw->pbsw", mixes[:4], w_in)
    w_log = -jax.nn.softplus(-(w0 + jnp.tanh(mixes[4] @ w1) @ w2)) - 0.5
    decay = jnp.exp(-jnp.exp(w_log))
    iclr = jax.nn.sigmoid(a0 + (mixes[5] @ a1) @ a2)

    def heads(t):
        return t.reshape(bsz, seq, RW_HEADS, RW_HEAD)

    kk = heads(k * k_k)
    kk = kk / jnp.maximum(jnp.linalg.norm(kk, axis=-1, keepdims=True), 1e-12)
    k = heads(k * (1.0 + (iclr - 1.0) * k_a))
    r, v, decay, iclr = heads(r), heads(v), heads(decay), heads(iclr)

    def step(state, inp):
        r_t, w_t, k_t, v_t, al_t, be_t = inp
        sa = jnp.einsum("bhvk,bhk->bhv", state, al_t)
        state = (state * w_t[:, :, None, :] + sa[..., None] * be_t[:, :, None, :]
                 + v_t[..., None] * k_t[:, :, None, :])
        return state, jnp.einsum("bhvk,bhk->bhv", state, r_t)

    def time_major(t):
        return t.transpose(1, 0, 2, 3)

    state0 = jnp.zeros((bsz, RW_HEADS, RW_HEAD, RW_HEAD), jnp.float32)
    xs = (time_major(r), time_major(decay), time_major(k), time_major(v),
          time_major(-kk), time_major(kk * iclr))
    _, y = lax.scan(step, state0, xs)
    y = time_major(y)
    mean = jnp.mean(y, axis=-1, keepdims=True)
    var = jnp.mean(jnp.square(y - mean), axis=-1, keepdims=True)
    y = ((y - mean) * lax.rsqrt(var + RW_GN_EPS)).reshape(bsz, seq, RW_WIDTH) * ln_w + ln_b
    bonus = jnp.sum(r * k * r_k, axis=-1, keepdims=True) * v
    y = y + bonus.reshape(bsz, seq, RW_WIDTH)
    return (y * jax.nn.silu(g)) @ w_out


def setup_inputs(seed: int = 0) -> dict:
    key = jax.random.key(seed)
    ks = iter(jax.random.split(key, 40))

    def nrm(shape, scale):
        return scale * jax.random.normal(next(ks), shape, jnp.float32)

    def unif(shape, lo, hi):
        return jax.random.uniform(next(ks), shape, jnp.float32, lo, hi)

    x = nrm((BATCH, SEQ, D_MODEL), 1.0)
    pre_norm = 1.0 + nrm((DEPTH, D_MODEL), 0.02)
    post_norm = 1.0 + nrm((DEPTH, D_MODEL), 0.02)
    ab_w_in = nrm((N_AB, D_MODEL, AB_IN), D_MODEL ** -0.5)
    ab_w_out = nrm((N_AB, AB_OUT, D_MODEL), AB_OUT ** -0.5)
    rg_conv_w = nrm((N_AB, RG_CONV, RG_WIDTH), RG_CONV ** -0.5)
    rg_conv_b = nrm((N_AB, RG_WIDTH), 0.01)
    rg_w_a = nrm((N_AB, RG_BLOCKS, RG_BLOCK, RG_BLOCK), RG_BLOCK ** -0.5)
    rg_b_a = nrm((N_AB, RG_WIDTH), 0.01)
    rg_w_x = nrm((N_AB, RG_BLOCKS, RG_BLOCK, RG_BLOCK), RG_BLOCK ** -0.5)
    rg_b_x = nrm((N_AB, RG_WIDTH), 0.01)
    a_pow_c = unif((N_AB, RG_WIDTH), 0.9, 0.999)
    a_base = a_pow_c ** (1.0 / RG_C)
    rg_lambda = jnp.log(a_base) - jnp.log1p(-a_base)
    hg_lower_bounds = nrm((DEPTH + 1, HG_KW), 0.1)
    hg_out_norm = 1.0 + nrm((N_AB, HG_DV), 0.02)
    rw_mu = unif((N_C, 6, D_MODEL), 0.0, 1.0)
    rw_w_in = nrm((N_C, 4, D_MODEL, RW_WIDTH), D_MODEL ** -0.5)
    rw_w_out = nrm((N_C, RW_WIDTH, D_MODEL), RW_WIDTH ** -0.5)
    rw_w0 = unif((N_C, RW_WIDTH), -6.0, -1.0)
    rw_w1 = nrm((N_C, D_MODEL, RW_DECAY_LORA), D_MODEL ** -0.5)
    rw_w2 = nrm((N_C, RW_DECAY_LORA, RW_WIDTH), 0.1 * RW_DECAY_LORA ** -0.5)
    rw_a0 = nrm((N_C, RW_WIDTH), 0.1)
    rw_a1 = nrm((N_C, D_MODEL, RW_AAA_LORA), D_MODEL ** -0.5)
    rw_a2 = nrm((N_C, RW_AAA_LORA, RW_WIDTH), 0.1 * RW_AAA_LORA ** -0.5)
    rw_k_k = 0.85 + nrm((N_C, RW_WIDTH), 0.02)
    rw_k_a = 1.0 + nrm((N_C, RW_WIDTH), 0.02)
    rw_r_k = nrm((N_C, RW_HEADS, RW_HEAD), 0.1)
    rw_ln_w = 1.0 + nrm((N_C, RW_WIDTH), 0.02)
    rw_ln_b = nrm((N_C, RW_WIDTH), 0.01)
    return {"x": x, "pre_norm": pre_norm, "post_norm": post_norm,
            "ab_w_in": ab_w_in, "ab_w_out": ab_w_out,
            "rg_conv_w": rg_conv_w, "rg_conv_b": rg_conv_b,
            "rg_w_a": rg_w_a, "rg_b_a": rg_b_a, "rg_w_x": rg_w_x, "rg_b_x": rg_b_x,
            "rg_lambda": rg_lambda, "hg_lower_bounds": hg_lower_bounds, "hg_out_norm": hg_out_norm,
            "rw_mu": rw_mu, "rw_w_in": rw_w_in, "rw_w_out": rw_w_out,
            "rw_w0": rw_w0, "rw_w1": rw_w1, "rw_w2": rw_w2,
            "rw_a0": rw_a0, "rw_a1": rw_a1, "rw_a2": rw_a2,
            "rw_k_k": rw_k_k, "rw_k_a": rw_k_a, "rw_r_k": rw_r_k,
            "rw_ln_w": rw_ln_w, "rw_ln_b": rw_ln_b}


def reference(x, pre_norm, post_norm, ab_w_in, ab_w_out, rg_conv_w, rg_conv_b, rg_w_a, rg_b_a,
              rg_w_x, rg_b_x, rg_lambda, hg_lower_bounds, hg_out_norm, rw_mu, rw_w_in, rw_w_out,
              rw_w0, rw_w1, rw_w2, rw_a0, rw_a1, rw_a2, rw_k_k, rw_k_a, rw_r_k, rw_ln_w, rw_ln_b):
    lb_table = jnp.cumsum(jax.nn.softmax(_f32(hg_lower_bounds), axis=0), axis=0)
    for layer in range(DEPTH):
        j = layer // 2
        u = rms_norm(x, pre_norm[layer])
        if layer % 2 == 0:
            mix = rglru_hgrn2_layer(u, ab_w_in[j], ab_w_out[j], rg_conv_w[j], rg_conv_b[j],
                                    rg_w_a[j], rg_b_a[j], rg_w_x[j], rg_b_x[j], rg_lambda[j],
                                    lb_table[layer + 1] - lb_table[0], hg_out_norm[j])
        else:
            mix = rwkv7_layer(u, rw_mu[j], rw_w_in[j], rw_w_out[j], rw_w0[j], rw_w1[j], rw_w2[j],
                              rw_a0[j], rw_a1[j], rw_a2[j], rw_k_k[j], rw_k_a[j], rw_r_k[j],
                              rw_ln_w[j], rw_ln_b[j])
        x = x + rms_norm(mix, post_norm[layer]).astype(x.dtype)
    return x
```

```python
import functools

import jax
import jax.numpy as jnp
from jax import lax
from jax.experimental import pallas as pl
from jax.experimental.pallas import tpu as pltpu

F32 = jnp.float32
BF16 = jnp.bfloat16

D_MODEL = 1024
RMS_EPS = 1e-6
RG_C = 8.0
RG_CONV = 4
HG_HEADS = 8
HG_DK = 128
RW_HEAD = 64
RW_GN_EPS = 64e-5

LANES = 128
SUBLANES = 8
MXU_DIM = 256
VMEM_LIMIT = 56 * 1024 * 1024

CHUNK = 64
SUB = 16


def _params(*sem):
    return pltpu.CompilerParams(dimension_semantics=sem, vmem_limit_bytes=VMEM_LIMIT)


def _dot(a, b):
    return jnp.dot(a.astype(BF16), b.astype(BF16), preferred_element_type=F32)


def _dot_nt(a, b):
    return lax.dot_general(a.astype(BF16), b.astype(BF16), (((1,), (1,)), ((), ())),
                           preferred_element_type=F32)


def _dot_tn(a, b):
    return lax.dot_general(a.astype(BF16), b.astype(BF16), (((0,), (0,)), ((), ())),
                           preferred_element_type=F32)


def _sigmoid(x):
    return 1.0 / (1.0 + jnp.exp(-x))


def _silu(x):
    return x * _sigmoid(x)


def _softplus(z):
    return jnp.maximum(z, 0.0) + jnp.log(1.0 + jnp.exp(-jnp.abs(z)))


def _rms(x, gain):
    return x * lax.rsqrt(jnp.mean(x * x, axis=-1, keepdims=True) + RMS_EPS) * gain


def _cumsum_rows(x):
    n = x.shape[0]
    row = lax.broadcasted_iota(jnp.int32, x.shape, 0)
    d = 1
    while d < n:
        x = x + jnp.where(row >= d, pltpu.roll(x, d, 0), 0.0)
        d *= 2
    return x


def _seg_sum(x, ones_bd):
    hi = x.astype(BF16)
    lo = (x - hi.astype(F32)).astype(BF16)
    outs = []
    for g in range(x.shape[1] // MXU_DIM):
        sl = slice(g * MXU_DIM, (g + 1) * MXU_DIM)
        outs.append(jnp.dot(hi[:, sl], ones_bd, preferred_element_type=F32)
                    + jnp.dot(lo[:, sl], ones_bd, preferred_element_type=F32))
    return jnp.concatenate(outs, axis=-1)


def _norm_proj_kernel(x_ref, g_ref, w_ref, o_ref, u_ref):
    @pl.when(pl.program_id(1) == 0)
    def _():
        u_ref[...] = _rms(x_ref[...], g_ref[...]).astype(BF16)

    o_ref[...] = jnp.dot(u_ref[...], w_ref[...], preferred_element_type=F32)


def _norm_proj(x2d, gain, w_bf16, tm=1024, tn=1024):
    t, d = x2d.shape
    n = w_bf16.shape[1]
    return pl.pallas_call(
        _norm_proj_kernel,
        grid=(t // tm, n // tn),
        in_specs=[pl.BlockSpec((tm, d), lambda i, j: (i, 0)),
                  pl.BlockSpec((1, d), lambda i, j: (0, 0)),
                  pl.BlockSpec((d, tn), lambda i, j: (0, j))],
        out_specs=pl.BlockSpec((None, tm, tn), lambda i, j: (j, i, 0)),
        out_shape=jax.ShapeDtypeStruct((n // tn, t, tn), F32),
        scratch_shapes=[pltpu.VMEM((tm, d), BF16)],
        compiler_params=_params("parallel", "arbitrary"),
        name="l0_norm_proj",
    )(x2d, gain, w_bf16)


def _shift_rows(x, halo, k):
    rolled = pltpu.roll(x, k, 0)
    row = lax.broadcasted_iota(jnp.int32, (SUBLANES, x.shape[1]), 0)
    head = jnp.where(row < k, pltpu.roll(halo, k, 0), rolled[:SUBLANES])
    return jnp.concatenate([head, rolled[SUBLANES:]], axis=0)


def _rglru_kernel(x_ref, g_ref, cw_ref, cb_ref, wa_ref, ba_ref, wx_ref, bx_ref, lam_ref,
                  o_ref, h_ref, halo_ref):
    @pl.when(pl.program_id(1) == 0)
    def _():
        h_ref[...] = jnp.zeros_like(h_ref)
        halo_ref[...] = jnp.zeros_like(halo_ref)

    x = x_ref[...]
    ts, width = x.shape
    halo = halo_ref[...]
    xc = cb_ref[...] + cw_ref[RG_CONV - 1:RG_CONV, :] * x
    for k in range(1, RG_CONV):
        xc = xc + cw_ref[RG_CONV - 1 - k:RG_CONV - k, :] * _shift_rows(x, halo, k)
    halo_ref[...] = x[ts - SUBLANES:, :]

    xcb = xc.astype(BF16)
    pre_r, pre_i = [], []
    for g in range(width // MXU_DIM):
        sl = slice(g * MXU_DIM, (g + 1) * MXU_DIM)
        pre_r.append(jnp.dot(xcb[:, sl], wa_ref[g], preferred_element_type=F32))
        pre_i.append(jnp.dot(xcb[:, sl], wx_ref[g], preferred_element_type=F32))
    gate_r = _sigmoid(jnp.concatenate(pre_r, axis=-1) + ba_ref[...])
    gate_i = _sigmoid(jnp.concatenate(pre_i, axis=-1) + bx_ref[...])

    log_a = (-RG_C) * gate_r * _softplus(-lam_ref[...])
    a = jnp.exp(log_a)
    b = jnp.sqrt(1.0 - jnp.exp(2.0 * log_a)) * (gate_i * xc)

    row = lax.broadcasted_iota(jnp.int32, (ts, 1), 0)
    d = 1
    while d < ts:
        m = row >= d
        b = jnp.where(m, a * pltpu.roll(b, d, 0) + b, b)
        a = jnp.where(m, a * pltpu.roll(a, d, 0), a)
        d *= 2
    h = a * h_ref[...] + b
    h_ref[...] = h[ts - 1:ts, :]
    o_ref[...] = (h * _silu(g_ref[...])).astype(o_ref.dtype)


def _rglru(proj4, conv_w, conv_b, wa_bd, b_a, wx_bd, b_x, lam, ts=256):
    _, bsz, seq, width = proj4.shape
    row = lambda: pl.BlockSpec((1, width), lambda b, s: (0, 0))
    return pl.pallas_call(
        _rglru_kernel,
        grid=(bsz, seq // ts),
        in_specs=[pl.BlockSpec((None, None, ts, width), lambda b, s: (0, b, s, 0)),
                  pl.BlockSpec((None, None, ts, width), lambda b, s: (1, b, s, 0)),
                  pl.BlockSpec((RG_CONV, width), lambda b, s: (0, 0)),
                  row(),
                  pl.BlockSpec(wa_bd.shape, lambda b, s: (0, 0, 0)),
                  row(),
                  pl.BlockSpec(wx_bd.shape, lambda b, s: (0, 0, 0)),
                  row(), row()],
        out_specs=pl.BlockSpec((None, ts, width), lambda b, s: (b, s, 0)),
        out_shape=jax.ShapeDtypeStruct((bsz, seq, width), BF16),
        scratch_shapes=[pltpu.VMEM((1, width), F32), pltpu.VMEM((SUBLANES, width), F32)],
        compiler_params=_params("parallel", "arbitrary"),
        name="l0_rglru",
    )(proj4, proj4, conv_w, conv_b, wa_bd, b_a, wx_bd, b_x, lam)


def _hgrn2_kernel(q_ref, f_ref, v_ref, g_ref, lbp_ref, gn_ref, o_ref, st_ref):
    @pl.when(pl.program_id(2) == 0)
    def _():
        st_ref[...] = jnp.zeros_like(st_ref)

    lbp = lbp_ref[...]
    e = jnp.exp(lbp - jnp.max(lbp, axis=0, keepdims=True))
    p = e / jnp.sum(e, axis=0, keepdims=True)
    lb = (p[0:1] + p[1:2]) - p[0:1]
    gn = gn_ref[...]
    ts = q_ref.shape[0]
    rsub = lax.broadcasted_iota(jnp.int32, (SUB, 1), 0)

    def chunk(c, carry):
        r0 = pl.multiple_of(c * CHUNK, CHUNK)
        rows = pl.ds(r0, CHUNK)
        q = _silu(q_ref[rows, :])
        sig = _sigmoid(f_ref[rows, :])
        f = lb + (1.0 - lb) * sig
        k = (1.0 - lb) * (1.0 - sig)
        v = v_ref[rows, :]
        gcum = _cumsum_rows(jnp.log(f))
        gend = gcum[CHUNK - 1:CHUNK, :]
        st = st_ref[...]

        out = _dot_nt(q * jnp.exp(gcum), st)
        st_ref[...] = st * jnp.exp(gend) + _dot_tn(v, k * jnp.exp(gend - gcum))

        blocks = []
        for i in range(CHUNK // SUB):
            lo = i * SUB
            gi, qi, ki, vi = (t[lo:lo + SUB] for t in (gcum, q, k, v))
            acc = out[lo:lo + SUB]
            if i > 0:
                rho = gcum[lo - 1:lo, :]
                sc = _dot_nt(qi * jnp.exp(gi - rho), k[:lo] * jnp.exp(rho - gcum[:lo]))
                acc = acc + _dot(sc, v[:lo])
            for s in range(SUB):
                w = jnp.exp(jnp.minimum(gi - gi[s:s + 1], 0.0))
                sc = jnp.sum(qi * w * ki[s:s + 1], axis=-1, keepdims=True)
                acc = acc + jnp.where(rsub >= s, sc, 0.0) * vi[s:s + 1]
            blocks.append(acc)
        o = jnp.concatenate(blocks, axis=0)
        o = _rms(o, gn) * _silu(g_ref[rows, :])
        o_ref[rows, :] = o.astype(o_ref.dtype)
        return carry

    lax.fori_loop(0, ts // CHUNK, chunk, 0)


def _hgrn2(proj4, lb_params, g_norm, ts=512):
    _, bsz, seq, width = proj4.shape
    spec = lambda j: pl.BlockSpec((None, None, ts, HG_DK), lambda b, h, s: (j, b, s, h))
    return pl.pallas_call(
        _hgrn2_kernel,
        grid=(bsz, HG_HEADS, seq // ts),
        in_specs=[spec(2), spec(3), spec(4), spec(5),
                  pl.BlockSpec((lb_params.shape[0], HG_DK), lambda b, h, s: (0, h)),
                  pl.BlockSpec((1, HG_DK), lambda b, h, s: (0, 0))],
        out_specs=pl.BlockSpec((None, ts, HG_DK), lambda b, h, s: (b, s, h)),
        out_shape=jax.ShapeDtypeStruct((bsz, seq, width), BF16),
        scratch_shapes=[pltpu.VMEM((HG_DK, HG_DK), F32)],
        compiler_params=_params("parallel", "parallel", "arbitrary"),
        name="l0_hgrn2",
    )(proj4, proj4, proj4, proj4, lb_params, g_norm)


def _out_proj0_kernel(hg_ref, og_ref, w_ref, pn_ref, x_ref, o_ref):
    half = hg_ref.shape[1]
    y = (jnp.dot(hg_ref[...], w_ref[:half, :], preferred_element_type=F32)
         + jnp.dot(og_ref[...], w_ref[half:, :], preferred_element_type=F32))
    o_ref[...] = x_ref[...] + _rms(y, pn_ref[...])


def _out_proj0(hg, og, w_bf16, post_gain, x2d, tm=512):
    t, d = x2d.shape
    tok = lambda: pl.BlockSpec((tm, d), lambda i: (i, 0))
    return pl.pallas_call(
        _out_proj0_kernel,
        grid=(t // tm,),
        in_specs=[tok(), tok(),
                  pl.BlockSpec(w_bf16.shape, lambda i: (0, 0)),
                  pl.BlockSpec((1, d), lambda i: (0, 0)),
                  tok()],
        out_specs=tok(),
        out_shape=jax.ShapeDtypeStruct((t, d), F32),
        compiler_params=_params("parallel"),
        name="l0_out_proj",
    )(hg, og, w_bf16, post_gain, x2d)


def _rwkv_front_kernel(x_ref, pn_ref, mu_ref, win_ref, w0_ref, w1_ref, w2_ref, a0_ref, a1_ref,
                       a2_ref, kk_ref, ka_ref, ones_ref,
                       r_out, lw_out, k_out, v_out, kk_out, b_out, g_out, prev_ref):
    @pl.when(pl.program_id(1) == 0)
    def _():
        prev_ref[...] = jnp.zeros_like(prev_ref)

    u = _rms(x_ref[...], pn_ref[...])
    tm = u.shape[0]
    row = lax.broadcasted_iota(jnp.int32, (tm, 1), 0)
    shifted = jnp.where(row == 0, prev_ref[...], pltpu.roll(u, 1, 0))
    prev_ref[...] = u[tm - 1:tm, :]
    delta = shifted - u
    mix = lambda p: u + delta * mu_ref[p:p + 1, :]

    r = _dot(mix(0), win_ref[0])
    k = _dot(mix(1), win_ref[1])
    v = _dot(mix(2), win_ref[2])
    g = _dot(mix(3), win_ref[3])
    lora_w = _dot(jnp.tanh(_dot(mix(4), w1_ref[...])), w2_ref[...])
    w_log = -_softplus(-(w0_ref[...] + lora_w)) - 0.5
    iclr = _sigmoid(a0_ref[...] + _dot(_dot(mix(5), a1_ref[...]), a2_ref[...]))

    kk = k * kk_ref[...]
    norm = jnp.sqrt(_seg_sum(kk * kk, ones_ref[...]))
    kk = kk / jnp.maximum(norm, 1e-12)

    r_out[...] = r
    lw_out[...] = -jnp.exp(w_log)
    k_out[...] = k * (1.0 + (iclr - 1.0) * ka_ref[...])
    v_out[...] = v
    kk_out[...] = kk
    b_out[...] = kk * iclr
    g_out[...] = g


def _rwkv_front(x1, pre_gain, mu, win, w0, w1, w2, a0, a1, a2, k_k, k_a, ones_bd, tm=256):
    bsz, seq, d = x1.shape
    tok = lambda: pl.BlockSpec((None, tm, d), lambda b, s: (b, s, 0))
    full = lambda a: pl.BlockSpec(a.shape, lambda b, s: (0,) * a.ndim)
    args = (pre_gain, mu, win, w0, w1, w2, a0, a1, a2, k_k, k_a, ones_bd)
    return pl.pallas_call(
        _rwkv_front_kernel,
        grid=(bsz, seq // tm),
        in_specs=[tok()] + [full(a) for a in args],
        out_specs=[tok() for _ in range(7)],
        out_shape=[jax.ShapeDtypeStruct((bsz, seq, d), F32) for _ in range(7)],
        scratch_shapes=[pltpu.VMEM((1, d), F32)],
        compiler_params=_params("parallel", "arbitrary"),
        name="l1_front",
    )(x1, *args)


def _rwkv_scan_kernel(r_ref, lw_ref, k_ref, v_ref, kk_ref, b_ref, y_ref, ht_ref):
    @pl.when(pl.program_id(2) == 0)
    def _():
        ht_ref[...] = jnp.zeros_like(ht_ref)

    ts = r_ref.shape[0]
    lane = lax.broadcasted_iota(jnp.int32, (CHUNK, LANES), 1)
    head0 = lane < RW_HEAD
    row = lax.broadcasted_iota(jnp.int32, (CHUNK, CHUNK), 0)
    col = lax.broadcasted_iota(jnp.int32, (CHUNK, CHUNK), 1)
    strict = row > col
    incl = row >= col
    eye = jnp.where(row == col, 1.0, 0.0)
    sr = lax.broadcasted_iota(jnp.int32, (LANES, LANES), 0) < RW_HEAD
    sc = lax.broadcasted_iota(jnp.int32, (LANES, LANES), 1) < RW_HEAD
    same_head = sr == sc

    def chunk(c, carry):
        rows = pl.ds(pl.multiple_of(c * CHUNK, CHUNK), CHUNK)
        r, lw, k, v, kk, b = (ref[rows, :] for ref in (r_ref, lw_ref, k_ref, v_ref, kk_ref, b_ref))
        lcum = _cumsum_rows(lw)
        lend = lcum[CHUNK - 1:CHUNK, :]
        e_neg = jnp.exp(-lcum)
        e_end = jnp.exp(lend - lcum)
        at = -kk * jnp.exp(lcum - lw)
        rt = r * jnp.exp(lcum)
        bt = b * e_neg
        kt = k * e_neg

        lhs = jnp.concatenate([jnp.where(head0, at, 0.0), jnp.where(head0, 0.0, at),
                               jnp.where(head0, rt, 0.0), jnp.where(head0, 0.0, rt)], axis=0)
        s_b = _dot_nt(lhs, bt)
        s_k = _dot_nt(lhs, kt)

        tmats, a_rb, a_rk, akv = [], [], [], []
        for h in range(2):
            n = jnp.where(strict, s_b[h * CHUNK:(h + 1) * CHUNK], 0.0)
            a_ak = jnp.where(strict, s_k[h * CHUNK:(h + 1) * CHUNK], 0.0)
            a_rb.append(jnp.where(incl, s_b[(2 + h) * CHUNK:(3 + h) * CHUNK], 0.0))
            a_rk.append(jnp.where(incl, s_k[(2 + h) * CHUNK:(3 + h) * CHUNK], 0.0))
            t = eye + n
            p = n
            d = 2
            while d < CHUNK:
                p = _dot(p, p)
                t = t + _dot(t, p)
                d *= 2
            tmats.append(t)
            akv.append(_dot(a_ak, v))
        rhs = jnp.concatenate([at, jnp.where(head0, akv[0], akv[1])], axis=1)
        tw0 = _dot(tmats[0], rhs)
        tw1 = _dot(tmats[1], rhs)
        wt = jnp.where(head0, tw0[:, :LANES], tw1[:, :LANES])
        ut = jnp.where(head0, tw0[:, LANES:], tw1[:, LANES:])

        ht = ht_ref[...]
        u = _dot_nt(wt, ht) + ut
        y = _dot_nt(rt, ht) + jnp.where(head0,
                                        _dot(a_rb[0], u) + _dot(a_rk[0], v),
                                        _dot(a_rb[1], u) + _dot(a_rk[1], v))
        upd = _dot_tn(u, b * e_end) + _dot_tn(v, k * e_end)
        ht_ref[...] = ht * jnp.exp(lend) + jnp.where(same_head, upd, 0.0)
        y_ref[rows, :] = y
        return carry

    lax.fori_loop(0, ts // CHUNK, chunk, 0)


def _rwkv_scan(r, lw, k, v, kk, b, ts=1024):
    bsz, seq, d = r.shape
    spec = lambda: pl.BlockSpec((None, ts, LANES), lambda bb, h, s: (bb, s, h))
    return pl.pallas_call(
        _rwkv_scan_kernel,
        grid=(bsz, d // LANES, seq // ts),
        in_specs=[spec() for _ in range(6)],
        out_specs=spec(),
        out_shape=jax.ShapeDtypeStruct((bsz, seq, d), F32),
        scratch_shapes=[pltpu.VMEM((LANES, LANES), F32)],
        compiler_params=_params("parallel", "parallel", "arbitrary"),
        name="l1_scan",
    )(r, lw, k, v, kk, b)


def _rwkv_back_kernel(y_ref, r_ref, k_ref, v_ref, g_ref, x_ref, rk_ref, lnw_ref, lnb_ref,
                      w_ref, pn_ref, ones_ref, o_ref):
    ones = ones_ref[...]
    inv_n = 1.0 / RW_HEAD
    y = y_ref[...]
    yc = y - _seg_sum(y, ones) * inv_n
    var = _seg_sum(yc * yc, ones) * inv_n
    yn = yc * lax.rsqrt(var + RW_GN_EPS) * lnw_ref[...] + lnb_ref[...]
    v = v_ref[...]
    yn = yn + _seg_sum(r_ref[...] * k_ref[...] * rk_ref[...], ones) * v
    out = _dot(yn * _silu(g_ref[...]), w_ref[...])
    o_ref[...] = x_ref[...] + _rms(out, pn_ref[...])


def _rwkv_back(y, r, k, v, g, x1, r_k, ln_w, ln_b, w_bf16, post_gain, ones_bd, tm=512):
    t, d = y.shape
    tok = lambda: pl.BlockSpec((tm, d), lambda i: (i, 0))
    full = lambda a: pl.BlockSpec(a.shape, lambda i: (0,) * a.ndim)
    params = (r_k, ln_w, ln_b, w_bf16, post_gain, ones_bd)
    return pl.pallas_call(
        _rwkv_back_kernel,
        grid=(t // tm,),
        in_specs=[tok() for _ in range(6)] + [full(a) for a in params],
        out_specs=tok(),
        out_shape=jax.ShapeDtypeStruct((t, d), F32),
        compiler_params=_params("parallel"),
        name="l1_back",
    )(y, r, k, v, g, x1, *params)


def _block_diag(w, group):
    n, bi, bj = w.shape
    w = w.reshape(n // group, group, bi, bj)
    eye = jnp.eye(group, dtype=w.dtype)
    return jnp.einsum("gaij,ab->gaibj", w, eye).reshape(n // group, group * bi, group * bj)


def kernel(x, pre_norm, post_norm, ab_w_in, ab_w_out, rg_conv_w, rg_conv_b, rg_w_a, rg_b_a,
           rg_w_x, rg_b_x, rg_lambda, hg_lower_bounds, hg_out_norm, rw_mu, rw_w_in, rw_w_out,
           rw_w0, rw_w1, rw_w2, rw_a0, rw_a1, rw_a2, rw_k_k, rw_k_a, rw_r_k, rw_ln_w, rw_ln_b):
    bsz, seq, d = x.shape
    t = bsz * seq
    row = lambda a: a.reshape(1, -1).astype(F32)
    x2d = x.reshape(t, d)
    heads_per_tile = MXU_DIM // RW_HEAD
    ones_bd = _block_diag(jnp.ones((heads_per_tile, RW_HEAD, RW_HEAD), F32), heads_per_tile)[0].astype(BF16)

    proj = _norm_proj(x2d, row(pre_norm[0]), ab_w_in[0].astype(BF16))
    proj4 = proj.reshape(proj.shape[0], bsz, seq, d)
    rg_group = MXU_DIM // rg_w_a.shape[-1]
    hg = _rglru(proj4, rg_conv_w[0].astype(F32), row(rg_conv_b[0]),
                _block_diag(rg_w_a[0], rg_group).astype(BF16), row(rg_b_a[0]),
                _block_diag(rg_w_x[0], rg_group).astype(BF16), row(rg_b_x[0]), row(rg_lambda[0]))
    og = _hgrn2(proj4, hg_lower_bounds.astype(F32), row(hg_out_norm[0]))
    x1 = _out_proj0(hg.reshape(t, d), og.reshape(t, d), ab_w_out[0].astype(BF16),
                    row(post_norm[0]), x2d)

    r, lw, k, v, kk, b, g = _rwkv_front(
        x1.reshape(bsz, seq, d), row(pre_norm[1]), rw_mu[0].astype(F32), rw_w_in[0].astype(BF16),
        row(rw_w0[0]), rw_w1[0].astype(BF16), rw_w2[0].astype(BF16), row(rw_a0[0]),
        rw_a1[0].astype(BF16), rw_a2[0].astype(BF16), row(rw_k_k[0]), row(rw_k_a[0]), ones_bd)
    y = _rwkv_scan(r, lw, k, v, kk, b)
    flat = lambda a: a.reshape(t, d)
    x2 = _rwkv_back(flat(y), flat(r), flat(k), flat(v), flat(g), x1, row(rw_r_k[0]),
                    row(rw_ln_w[0]), row(rw_ln_b[0]), rw_w_out[0].astype(BF16),
                    row(post_norm[1]), ones_bd)
    return x2.reshape(bsz, seq, d)
```

```python
import functools

import jax
import jax.numpy as jnp
from jax import lax
from jax.experimental import pallas as pl
from jax.experimental.pallas import tpu as pltpu

F32 = jnp.float32
BF16 = jnp.bfloat16

D_MODEL = 1024
RMS_EPS = 1e-6
RG_C = 8.0
RG_CONV = 4
HG_HEADS = 8
HG_DK = 128
RW_HEAD = 64
RW_GN_EPS = 64e-5

LANES = 128
SUBLANES = 8
MXU_DIM = 256
VMEM_LIMIT = 56 * 1024 * 1024

CHUNK = 64
SUB = 8
SCAN_GROUP = 4
HG_GROUP = 4
LOG2E = 1.4426950408889634


def _params(*sem):
    return pltpu.CompilerParams(dimension_semantics=sem, vmem_limit_bytes=VMEM_LIMIT)


def _dot(a, b):
    return jnp.dot(a.astype(BF16), b.astype(BF16), preferred_element_type=F32)


def _dot_nt(a, b):
    return lax.dot_general(a.astype(BF16), b.astype(BF16), (((1,), (1,)), ((), ())),
                           preferred_element_type=F32)


def _dot_tn(a, b):
    return lax.dot_general(a.astype(BF16), b.astype(BF16), (((0,), (0,)), ((), ())),
                           preferred_element_type=F32)


def _sigmoid(x):
    return 1.0 / (1.0 + jnp.exp(-x))


def _silu(x):
    return x * _sigmoid(x)


def _softplus(z):
    return jnp.maximum(z, 0.0) + jnp.log(1.0 + jnp.exp(-jnp.abs(z)))


def _rms(x, gain):
    return x * lax.rsqrt(jnp.mean(x * x, axis=-1, keepdims=True) + RMS_EPS) * gain


def _cumsum_rows(x):
    n = x.shape[0]
    row = lax.broadcasted_iota(jnp.int32, x.shape, 0)
    d = 1
    while d < n:
        x = x + jnp.where(row >= d, pltpu.roll(x, d, 0), 0.0)
        d *= 2
    return x


def _seg_sum(x, ones_bd):
    hi = x.astype(BF16)
    lo = (x - hi.astype(F32)).astype(BF16)
    outs = []
    for g in range(x.shape[1] // MXU_DIM):
        sl = slice(g * MXU_DIM, (g + 1) * MXU_DIM)
        outs.append(jnp.dot(hi[:, sl], ones_bd, preferred_element_type=F32)
                    + jnp.dot(lo[:, sl], ones_bd, preferred_element_type=F32))
    return jnp.concatenate(outs, axis=-1)


def _norm_proj_kernel(x_ref, g_ref, w_ref, o_ref, u_ref):
    @pl.when(pl.program_id(1) == 0)
    def _():
        u_ref[...] = _rms(x_ref[...], g_ref[...]).astype(BF16)

    o_ref[...] = jnp.dot(u_ref[...], w_ref[...], preferred_element_type=F32)


def _norm_proj(x2d, gain, w_bf16, tm=1024, tn=1024):
    t, d = x2d.shape
    n = w_bf16.shape[1]
    return pl.pallas_call(
        _norm_proj_kernel,
        grid=(t // tm, n // tn),
        in_specs=[pl.BlockSpec((tm, d), lambda i, j: (i, 0)),
                  pl.BlockSpec((1, d), lambda i, j: (0, 0)),
                  pl.BlockSpec((d, tn), lambda i, j: (0, j))],
        out_specs=pl.BlockSpec((None, tm, tn), lambda i, j: (j, i, 0)),
        out_shape=jax.ShapeDtypeStruct((n // tn, t, tn), F32),
        scratch_shapes=[pltpu.VMEM((tm, d), BF16)],
        compiler_params=_params("parallel", "arbitrary"),
        name="l0_norm_proj",
    )(x2d, gain, w_bf16)


def _shift_rows(x, halo, k):
    rolled = pltpu.roll(x, k, 0)
    row = lax.broadcasted_iota(jnp.int32, (SUBLANES, x.shape[1]), 0)
    head = jnp.where(row < k, pltpu.roll(halo, k, 0), rolled[:SUBLANES])
    return jnp.concatenate([head, rolled[SUBLANES:]], axis=0)


def _rglru_kernel(x_ref, g_ref, cw_ref, cb_ref, wa_ref, ba_ref, wx_ref, bx_ref, lam_ref,
                  o_ref, h_ref, halo_ref):
    @pl.when(pl.program_id(1) == 0)
    def _():
        h_ref[...] = jnp.zeros_like(h_ref)
        halo_ref[...] = jnp.zeros_like(halo_ref)

    x = x_ref[...]
    ts, width = x.shape
    halo = halo_ref[...]
    xc = cb_ref[...] + cw_ref[RG_CONV - 1:RG_CONV, :] * x
    for k in range(1, RG_CONV):
        xc = xc + cw_ref[RG_CONV - 1 - k:RG_CONV - k, :] * _shift_rows(x, halo, k)
    halo_ref[...] = x[ts - SUBLANES:, :]

    xcb = xc.astype(BF16)
    pre_r, pre_i = [], []
    for g in range(width // MXU_DIM):
        sl = slice(g * MXU_DIM, (g + 1) * MXU_DIM)
        pre_r.append(jnp.dot(xcb[:, sl], wa_ref[g], preferred_element_type=F32))
        pre_i.append(jnp.dot(xcb[:, sl], wx_ref[g], preferred_element_type=F32))
    gate_r = _sigmoid(jnp.concatenate(pre_r, axis=-1) + ba_ref[...])
    gate_i = _sigmoid(jnp.concatenate(pre_i, axis=-1) + bx_ref[...])

    log_a = (-RG_C) * gate_r * _softplus(-lam_ref[...])
    a = jnp.exp(log_a)
    b = jnp.sqrt(1.0 - jnp.exp(2.0 * log_a)) * (gate_i * xc)

    row = lax.broadcasted_iota(jnp.int32, (ts, 1), 0)
    d = 1
    while d < ts:
        m = row >= d
        b = jnp.where(m, a * pltpu.roll(b, d, 0) + b, b)
        a = jnp.where(m, a * pltpu.roll(a, d, 0), a)
        d *= 2
    h = a * h_ref[...] + b
    h_ref[...] = h[ts - 1:ts, :]
    o_ref[...] = (h * _silu(g_ref[...])).astype(o_ref.dtype)


def _rglru(proj4, conv_w, conv_b, wa_bd, b_a, wx_bd, b_x, lam, ts=256):
    _, bsz, seq, width = proj4.shape
    row = lambda: pl.BlockSpec((1, width), lambda b, s: (0, 0))
    return pl.pallas_call(
        _rglru_kernel,
        grid=(bsz, seq // ts),
        in_specs=[pl.BlockSpec((None, None, ts, width), lambda b, s: (0, b, s, 0)),
                  pl.BlockSpec((None, None, ts, width), lambda b, s: (1, b, s, 0)),
                  pl.BlockSpec((RG_CONV, width), lambda b, s: (0, 0)),
                  row(),
                  pl.BlockSpec(wa_bd.shape, lambda b, s: (0, 0, 0)),
                  row(),
                  pl.BlockSpec(wx_bd.shape, lambda b, s: (0, 0, 0)),
                  row(), row()],
        out_specs=pl.BlockSpec((None, ts, width), lambda b, s: (b, s, 0)),
        out_shape=jax.ShapeDtypeStruct((bsz, seq, width), BF16),
        scratch_shapes=[pltpu.VMEM((1, width), F32), pltpu.VMEM((SUBLANES, width), F32)],
        compiler_params=_params("parallel", "arbitrary"),
        name="l0_rglru",
    )(proj4, proj4, conv_w, conv_b, wa_bd, b_a, wx_bd, b_x, lam)


def _hgrn2_kernel(q_ref, f_ref, v_ref, g_ref, lbp_ref, gn_ref, o_ref, st_ref):
    @pl.when(pl.program_id(2) == 0)
    def _():
        st_ref[...] = jnp.zeros_like(st_ref)

    lbp = lbp_ref[...]
    e = jnp.exp(lbp - jnp.max(lbp, axis=0, keepdims=True))
    p = e / jnp.sum(e, axis=0, keepdims=True)
    lb = (p[0:1] + p[1:2]) - p[0:1]
    gn = gn_ref[...]
    ts = q_ref.shape[0]
    row = lax.broadcasted_iota(jnp.int32, (CHUNK, LANES), 0)
    srow = lax.broadcasted_iota(jnp.int32, (CHUNK, CHUNK), 0)
    scol = lax.broadcasted_iota(jnp.int32, (CHUNK, CHUNK), 1)
    diag_lane0 = srow & ~(SUB - 1)
    causal = scol <= srow
    levels = []
    h = CHUNK // 2
    while h >= SUB:
        upper = (row & (2 * h - 1)) >= h
        same_block = (srow & ~(2 * h - 1)) == (scol & ~(2 * h - 1))
        levels.append((h, upper, same_block))
        h //= 2

    def block_rows(x, size, r):
        return jnp.concatenate([jnp.broadcast_to(x[lo + r:lo + r + 1], (size, LANES))
                                for lo in range(0, CHUNK, size)], axis=0)

    def group(i, carry):
        idx = range(HG_GROUP)
        rows = [pl.ds(pl.multiple_of((i * HG_GROUP + j) * CHUNK, CHUNK), CHUNK) for j in idx]
        q = [_silu(q_ref[r, :]) for r in rows]
        sig = [_sigmoid(f_ref[r, :]) for r in rows]
        k = [(1.0 - lb) * (1.0 - x) for x in sig]
        v = [v_ref[r, :].astype(BF16) for r in rows]
        gcum = [_cumsum_rows(jnp.log(lb + (1.0 - lb) * x)) for x in sig]
        gend = [x[CHUNK - 1:CHUNK, :] for x in gcum]
        qg = [(a * jnp.exp(b)).astype(BF16) for a, b in zip(q, gcum)]
        vtk = [_dot_tn(a, b * jnp.exp(e - c)) for a, b, c, e in zip(v, k, gcum, gend)]

        scores = [None for _ in idx]
        for h, upper, same_block in levels:
            for j in idx:
                e = jnp.exp(-jnp.abs(gcum[j] - block_rows(gcum[j], 2 * h, h - 1)))
                s = _dot_nt(jnp.where(upper, q[j] * e, 0.0), jnp.where(upper, 0.0, k[j] * e))
                s = jnp.where(same_block, s, 0.0)
                scores[j] = s if scores[j] is None else scores[j] + s

        g2 = [x * LOG2E for x in gcum]
        gk = [a - jnp.log2(b) for a, b in zip(g2, k)]
        sc = [jnp.zeros((CHUNK, CHUNK), F32) for _ in idx]
        for s in range(SUB):
            for j in idx:
                w = jnp.exp2(g2[j] - block_rows(gk[j], SUB, s))
                col = jnp.sum(q[j] * w, axis=-1, keepdims=True)
                sc[j] = jnp.where(scol == diag_lane0 + s, col, sc[j])
        intra = [_dot(a + jnp.where(causal, b, 0.0), c) for a, b, c in zip(scores, sc, v)]

        st = st_ref[...]
        for j in idx:
            o = _dot_nt(qg[j], st) + intra[j]
            st = st * jnp.exp(gend[j]) + vtk[j]
            o = _rms(o, gn) * _silu(g_ref[rows[j], :])
            o_ref[rows[j], :] = o.astype(o_ref.dtype)
        st_ref[...] = st
        return carry

    lax.fori_loop(0, ts // (CHUNK * HG_GROUP), group, 0)


def _hgrn2(proj4, lb_params, g_norm, ts=512):
    _, bsz, seq, width = proj4.shape
    spec = lambda j: pl.BlockSpec((None, None, ts, HG_DK), lambda b, h, s: (j, b, s, h))
    return pl.pallas_call(
        _hgrn2_kernel,
        grid=(bsz, HG_HEADS, seq // ts),
        in_specs=[spec(2), spec(3), spec(4), spec(5),
                  pl.BlockSpec((lb_params.shape[0], HG_DK), lambda b, h, s: (0, h)),
                  pl.BlockSpec((1, HG_DK), lambda b, h, s: (0, 0))],
        out_specs=pl.BlockSpec((None, ts, HG_DK), lambda b, h, s: (b, s, h)),
        out_shape=jax.ShapeDtypeStruct((bsz, seq, width), BF16),
        scratch_shapes=[pltpu.VMEM((HG_DK, HG_DK), F32)],
        compiler_params=_params("parallel", "parallel", "arbitrary"),
        name="l0_hgrn2",
    )(proj4, proj4, proj4, proj4, lb_params, g_norm)


def _out_proj0_kernel(hg_ref, og_ref, w_ref, pn_ref, x_ref, o_ref):
    half = hg_ref.shape[1]
    y = (jnp.dot(hg_ref[...], w_ref[:half, :], preferred_element_type=F32)
         + jnp.dot(og_ref[...], w_ref[half:, :], preferred_element_type=F32))
    o_ref[...] = x_ref[...] + _rms(y, pn_ref[...])


def _out_proj0(hg, og, w_bf16, post_gain, x2d, tm=512):
    t, d = x2d.shape
    tok = lambda: pl.BlockSpec((tm, d), lambda i: (i, 0))
    return pl.pallas_call(
        _out_proj0_kernel,
        grid=(t // tm,),
        in_specs=[tok(), tok(),
                  pl.BlockSpec(w_bf16.shape, lambda i: (0, 0)),
                  pl.BlockSpec((1, d), lambda i: (0, 0)),
                  tok()],
        out_specs=tok(),
        out_shape=jax.ShapeDtypeStruct((t, d), F32),
        compiler_params=_params("parallel"),
        name="l0_out_proj",
    )(hg, og, w_bf16, post_gain, x2d)


def _rwkv_front_kernel(x_ref, pn_ref, mu_ref, win_ref, w0_ref, w1_ref, w2_ref, a0_ref, a1_ref,
                       a2_ref, kk_ref, ka_ref, ones_ref,
                       r_out, lw_out, k_out, v_out, kk_out, b_out, g_out, prev_ref):
    @pl.when(pl.program_id(1) == 0)
    def _():
        prev_ref[...] = jnp.zeros_like(prev_ref)

    u = _rms(x_ref[...], pn_ref[...])
    tm = u.shape[0]
    row = lax.broadcasted_iota(jnp.int32, (tm, 1), 0)
    shifted = jnp.where(row == 0, prev_ref[...], pltpu.roll(u, 1, 0))
    prev_ref[...] = u[tm - 1:tm, :]
    delta = shifted - u
    mix = lambda p: u + delta * mu_ref[p:p + 1, :]

    r = _dot(mix(0), win_ref[0])
    k = _dot(mix(1), win_ref[1])
    v = _dot(mix(2), win_ref[2])
    g = _dot(mix(3), win_ref[3])
    lora_w = _dot(jnp.tanh(_dot(mix(4), w1_ref[...])), w2_ref[...])
    w_log = -_softplus(-(w0_ref[...] + lora_w)) - 0.5
    iclr = _sigmoid(a0_ref[...] + _dot(_dot(mix(5), a1_ref[...]), a2_ref[...]))

    kk = k * kk_ref[...]
    norm = jnp.sqrt(_seg_sum(kk * kk, ones_ref[...]))
    kk = kk / jnp.maximum(norm, 1e-12)

    r_out[...] = r
    lw_out[...] = -jnp.exp(w_log)
    k_out[...] = k * (1.0 + (iclr - 1.0) * ka_ref[...])
    v_out[...] = v
    kk_out[...] = kk
    b_out[...] = kk * iclr
    g_out[...] = g


def _rwkv_front(x1, pre_gain, mu, win, w0, w1, w2, a0, a1, a2, k_k, k_a, ones_bd, tm=256):
    bsz, seq, d = x1.shape
    tok = lambda: pl.BlockSpec((None, tm, d), lambda b, s: (b, s, 0))
    full = lambda a: pl.BlockSpec(a.shape, lambda b, s: (0,) * a.ndim)
    args = (pre_gain, mu, win, w0, w1, w2, a0, a1, a2, k_k, k_a, ones_bd)
    return pl.pallas_call(
        _rwkv_front_kernel,
        grid=(bsz, seq // tm),
        in_specs=[tok()] + [full(a) for a in args],
        out_specs=[tok() for _ in range(7)],
        out_shape=[jax.ShapeDtypeStruct((bsz, seq, d), F32) for _ in range(7)],
        scratch_shapes=[pltpu.VMEM((1, d), F32)],
        compiler_params=_params("parallel", "arbitrary"),
        name="l1_front",
    )(x1, *args)


def _rwkv_scan_kernel(r_ref, lw_ref, k_ref, v_ref, kk_ref, b_ref, y_ref,
                      ht_ref, rt_s, y0_s, ml_s, ha_s, gm_s):
    @pl.when(pl.program_id(2) == 0)
    def _():
        ht_ref[...] = jnp.zeros_like(ht_ref)

    ts = r_ref.shape[0]
    nchunk = ts // CHUNK
    two = 2 * CHUNK
    head0 = lax.broadcasted_iota(jnp.int32, (CHUNK, LANES), 1) < RW_HEAD
    row = lax.broadcasted_iota(jnp.int32, (two, two), 0)
    col = lax.broadcasted_iota(jnp.int32, (two, two), 1)
    strict = (row & (CHUNK - 1)) > (col & (CHUNK - 1))
    incl = (row & (CHUNK - 1)) >= (col & (CHUNK - 1))
    eye = jnp.where(row == col, 1.0, 0.0)

    def stack(x):
        return jnp.concatenate([jnp.where(head0, x, 0.0), jnp.where(head0, 0.0, x)], axis=0)

    def prep(c):
        rows = pl.ds(pl.multiple_of(c * CHUNK, CHUNK), CHUNK)
        r, lw, k, v, kk, b = (ref[rows, :] for ref in (r_ref, lw_ref, k_ref, v_ref, kk_ref, b_ref))
        lcum = _cumsum_rows(lw)
        lend = lcum[CHUNK - 1:CHUNK, :]
        e_neg = jnp.exp(-lcum)
        e_end = jnp.exp(lend - lcum)
        gm_s[c] = jnp.exp(lend)
        return dict(rows=rows, c=c,
                    at2=stack(-kk * jnp.exp(lcum - lw)).astype(BF16),
                    rt2=stack(r * jnp.exp(lcum)),
                    bk2=jnp.concatenate([stack(b * e_neg), stack(k * e_neg)], axis=0).astype(BF16),
                    v2=stack(v).astype(BF16),
                    bkh2=jnp.concatenate([stack(b * e_end), stack(k * e_end)], axis=0).astype(BF16))

    def local_group(i, carry):
        g = [prep(i * SCAN_GROUP + j) for j in range(SCAN_GROUP)]
        each = lambda f: [f(q) for q in g]
        s = each(lambda q: _dot_nt(jnp.concatenate([q["at2"], q["rt2"].astype(BF16)], axis=0), q["bk2"]))
        n = [jnp.where(strict, x[:two, :two], 0.0) for x in s]
        a_ak = [jnp.where(strict, x[:two, two:], 0.0).astype(BF16) for x in s]
        a_rb = [jnp.where(incl, x[two:, :two], 0.0).astype(BF16) for x in s]
        a_rk = [jnp.where(incl, x[two:, two:], 0.0).astype(BF16) for x in s]
        t = [eye + x for x in n]
        p = n
        d = 2
        while d < CHUNK:
            p = [_dot(x, x) for x in p]
            t = [x + _dot(x, y) for x, y in zip(t, p)]
            d *= 2
        akv = [_dot(x, q["v2"]) for x, q in zip(a_ak, g)]
        tw = [_dot(x, jnp.concatenate([q["at2"], y.astype(BF16)], axis=1)).astype(BF16)
              for x, y, q in zip(t, akv, g)]
        ry = [_dot(x, y) for x, y in zip(a_rb, tw)]
        rkv = [_dot(x, q["v2"]) for x, q in zip(a_rk, g)]
        ml = [_dot_tn(q["bkh2"][:two], y[:, :LANES]) for q, y in zip(g, tw)]
        ha = [_dot_tn(jnp.concatenate([y[:, LANES:], q["v2"]], axis=0), q["bkh2"]) for q, y in zip(g, tw)]
        for j, q in enumerate(g):
            rtp = q["rt2"] + ry[j][:, :LANES]
            y0 = ry[j][:, LANES:] + rkv[j]
            rt_s[q["rows"], :] = (rtp[:CHUNK] + rtp[CHUNK:]).astype(BF16)
            y0_s[q["rows"], :] = y0[:CHUNK] + y0[CHUNK:]
            ml_s[q["c"]] = ml[j].astype(BF16)
            ha_s[q["c"]] = ha[j]
        return carry

    lax.fori_loop(0, nchunk // SCAN_GROUP, local_group, 0)

    def step(c, ht):
        rows = pl.ds(pl.multiple_of(c * CHUNK, CHUNK), CHUNK)
        htb = ht.astype(BF16)
        y_ref[rows, :] = _dot_nt(rt_s[rows, :], htb) + y0_s[rows, :]
        return ht * gm_s[c] + _dot_nt(htb, ml_s[c]) + ha_s[c]

    ht_ref[...] = lax.fori_loop(0, nchunk, step, ht_ref[...])


def _rwkv_scan(r, lw, k, v, kk, b, ts=1024):
    bsz, seq, d = r.shape
    nchunk = ts // CHUNK
    spec = lambda: pl.BlockSpec((None, ts, LANES), lambda bb, h, s: (bb, s, h))
    return pl.pallas_call(
        _rwkv_scan_kernel,
        grid=(bsz, d // LANES, seq // ts),
        in_specs=[spec() for _ in range(6)],
        out_specs=spec(),
        out_shape=jax.ShapeDtypeStruct((bsz, seq, d), F32),
        scratch_shapes=[pltpu.VMEM((LANES, LANES), F32),
                        pltpu.VMEM((ts, LANES), BF16),
                        pltpu.VMEM((ts, LANES), F32),
                        pltpu.VMEM((nchunk, LANES, LANES), BF16),
                        pltpu.VMEM((nchunk, LANES, LANES), F32),
                        pltpu.VMEM((nchunk, 1, LANES), F32)],
        compiler_params=_params("parallel", "parallel", "arbitrary"),
        name="l1_scan",
    )(r, lw, k, v, kk, b)


def _rwkv_back_kernel(y_ref, r_ref, k_ref, v_ref, g_ref, x_ref, rk_ref, lnw_ref, lnb_ref,
                      w_ref, pn_ref, ones_ref, o_ref):
    ones = ones_ref[...]
    inv_n = 1.0 / RW_HEAD
    y = y_ref[...]
    yc = y - _seg_sum(y, ones) * inv_n
    var = _seg_sum(yc * yc, ones) * inv_n
    yn = yc * lax.rsqrt(var + RW_GN_EPS) * lnw_ref[...] + lnb_ref[...]
    v = v_ref[...]
    yn = yn + _seg_sum(r_ref[...] * k_ref[...] * rk_ref[...], ones) * v
    out = _dot(yn * _silu(g_ref[...]), w_ref[...])
    o_ref[...] = x_ref[...] + _rms(out, pn_ref[...])


def _rwkv_back(y, r, k, v, g, x1, r_k, ln_w, ln_b, w_bf16, post_gain, ones_bd, tm=512):
    t, d = y.shape
    tok = lambda: pl.BlockSpec((tm, d), lambda i: (i, 0))
    full = lambda a: pl.BlockSpec(a.shape, lambda i: (0,) * a.ndim)
    params = (r_k, ln_w, ln_b, w_bf16, post_gain, ones_bd)
    return pl.pallas_call(
        _rwkv_back_kernel,
        grid=(t // tm,),
        in_specs=[tok() for _ in range(6)] + [full(a) for a in params],
        out_specs=tok(),
        out_shape=jax.ShapeDtypeStruct((t, d), F32),
        compiler_params=_params("parallel"),
        name="l1_back",
    )(y, r, k, v, g, x1, *params)


def _block_diag(w, group):
    n, bi, bj = w.shape
    w = w.reshape(n // group, group, bi, bj)
    eye = jnp.eye(group, dtype=w.dtype)
    return jnp.einsum("gaij,ab->gaibj", w, eye).reshape(n // group, group * bi, group * bj)


def kernel(x, pre_norm, post_norm, ab_w_in, ab_w_out, rg_conv_w, rg_conv_b, rg_w_a, rg_b_a,
           rg_w_x, rg_b_x, rg_lambda, hg_lower_bounds, hg_out_norm, rw_mu, rw_w_in, rw_w_out,
           rw_w0, rw_w1, rw_w2, rw_a0, rw_a1, rw_a2, rw_k_k, rw_k_a, rw_r_k, rw_ln_w, rw_ln_b):
    bsz, seq, d = x.shape
    t = bsz * seq
    row = lambda a: a.reshape(1, -1).astype(F32)
    x2d = x.reshape(t, d)
    heads_per_tile = MXU_DIM // RW_HEAD
    ones_bd = _block_diag(jnp.ones((heads_per_tile, RW_HEAD, RW_HEAD), F32), heads_per_tile)[0].astype(BF16)

    proj = _norm_proj(x2d, row(pre_norm[0]), ab_w_in[0].astype(BF16))
    proj4 = proj.reshape(proj.shape[0], bsz, seq, d)
    rg_group = MXU_DIM // rg_w_a.shape[-1]
    hg = _rglru(proj4, rg_conv_w[0].astype(F32), row(rg_conv_b[0]),
                _block_diag(rg_w_a[0], rg_group).astype(BF16), row(rg_b_a[0]),
                _block_diag(rg_w_x[0], rg_group).astype(BF16), row(rg_b_x[0]), row(rg_lambda[0]))
    og = _hgrn2(proj4, hg_lower_bounds.astype(F32), row(hg_out_norm[0]))
    x1 = _out_proj0(hg.reshape(t, d), og.reshape(t, d), ab_w_out[0].astype(BF16),
                    row(post_norm[0]), x2d)

    r, lw, k, v, kk, b, g = _rwkv_front(
        x1.reshape(bsz, seq, d), row(pre_norm[1]), rw_mu[0].astype(F32), rw_w_in[0].astype(BF16),
        row(rw_w0[0]), rw_w1[0].astype(BF16), rw_w2[0].astype(BF16), row(rw_a0[0]),
        rw_a1[0].astype(BF16), rw_a2[0].astype(BF16), row(rw_k_k[0]), row(rw_k_a[0]), ones_bd)
    y = _rwkv_scan(r, lw, k, v, kk, b)
    flat = lambda a: a.reshape(t, d)
    x2 = _rwkv_back(flat(y), flat(r), flat(k), flat(v), flat(g), x1, row(rw_r_k[0]),
                    row(rw_ln_w[0]), row(rw_ln_b[0]), rw_w_out[0].astype(BF16),
                    row(post_norm[1]), ones_bd)
    return x2.reshape(bsz, seq, d)
```

```python
import functools

import jax
import jax.numpy as jnp
from jax import lax
from jax.experimental import pallas as pl
from jax.experimental.pallas import tpu as pltpu

F32 = jnp.float32
BF16 = jnp.bfloat16

D_MODEL = 1024
RMS_EPS = 1e-6
RG_C = 8.0
RG_CONV = 4
HG_HEADS = 8
HG_DK = 128
RW_HEAD = 64
RW_GN_EPS = 64e-5

LANES = 128
SUBLANES = 8
MXU_DIM = 256
VMEM_LIMIT = 56 * 1024 * 1024

CHUNK = 64
SUB = 8
SCAN_GROUP = 2
HG_GROUP = 4
LOG2E = 1.4426950408889634


def _params(*sem):
    return pltpu.CompilerParams(dimension_semantics=sem, vmem_limit_bytes=VMEM_LIMIT)


def _dot(a, b):
    return jnp.dot(a.astype(BF16), b.astype(BF16), preferred_element_type=F32)


def _dot_nt(a, b):
    return lax.dot_general(a.astype(BF16), b.astype(BF16), (((1,), (1,)), ((), ())),
                           preferred_element_type=F32)


def _dot_tn(a, b):
    return lax.dot_general(a.astype(BF16), b.astype(BF16), (((0,), (0,)), ((), ())),
                           preferred_element_type=F32)


def _sigmoid(x):
    return 1.0 / (1.0 + jnp.exp(-x))


def _silu(x):
    return x * _sigmoid(x)


def _softplus(z):
    return jnp.maximum(z, 0.0) + jnp.log(1.0 + jnp.exp(-jnp.abs(z)))


def _rms(x, gain):
    return x * lax.rsqrt(jnp.mean(x * x, axis=-1, keepdims=True) + RMS_EPS) * gain


def _cumsum_rows(x):
    n = x.shape[0]
    row = lax.broadcasted_iota(jnp.int32, x.shape, 0)
    d = 1
    while d < n:
        x = x + jnp.where(row >= d, pltpu.roll(x, d, 0), 0.0)
        d *= 2
    return x


def _seg_sum(x, ones_bd, split=True):
    hi = x.astype(BF16)
    lo = (x - hi.astype(F32)).astype(BF16) if split else None
    outs = []
    for g in range(x.shape[1] // MXU_DIM):
        sl = slice(g * MXU_DIM, (g + 1) * MXU_DIM)
        acc = jnp.dot(hi[:, sl], ones_bd, preferred_element_type=F32)
        if split:
            acc = acc + jnp.dot(lo[:, sl], ones_bd, preferred_element_type=F32)
        outs.append(acc)
    return jnp.concatenate(outs, axis=-1)


def _norm_proj_kernel(x_ref, g_ref, w_ref, o_ref, u_ref):
    @pl.when(pl.program_id(1) == 0)
    def _():
        u_ref[...] = _rms(x_ref[...], g_ref[...]).astype(BF16)

    o_ref[...] = jnp.dot(u_ref[...], w_ref[...], preferred_element_type=F32).astype(o_ref.dtype)


def _norm_proj(x2d, gain, w_bf16, tm=1024, tn=1024):
    t, d = x2d.shape
    n = w_bf16.shape[1]
    return pl.pallas_call(
        _norm_proj_kernel,
        grid=(t // tm, n // tn),
        in_specs=[pl.BlockSpec((tm, d), lambda i, j: (i, 0)),
                  pl.BlockSpec((1, d), lambda i, j: (0, 0)),
                  pl.BlockSpec((d, tn), lambda i, j: (0, j))],
        out_specs=pl.BlockSpec((None, tm, tn), lambda i, j: (j, i, 0)),
        out_shape=jax.ShapeDtypeStruct((n // tn, t, tn), BF16),
        scratch_shapes=[pltpu.VMEM((tm, d), BF16)],
        compiler_params=_params("parallel", "arbitrary"),
        name="l0_norm_proj",
    )(x2d, gain, w_bf16)


def _rglru_kernel(x_ref, g_ref, cw_ref, cb_ref, wa_ref, ba_ref, wx_ref, bx_ref, lam_ref,
                  o_ref, h_ref, ext_ref, a_ref, b_ref):
    @pl.when(pl.program_id(1) == 0)
    def _():
        h_ref[...] = jnp.zeros_like(h_ref)
        ext_ref[:SUBLANES, :] = jnp.zeros((SUBLANES, ext_ref.shape[1]), F32)

    x = x_ref[...].astype(F32)
    ts, width = x.shape
    ext_ref[SUBLANES:, :] = x
    xc = cb_ref[...] + cw_ref[RG_CONV - 1:RG_CONV, :] * x
    for k in range(1, RG_CONV):
        xc = xc + cw_ref[RG_CONV - 1 - k:RG_CONV - k, :] * ext_ref[SUBLANES - k:SUBLANES - k + ts, :]
    ext_ref[:SUBLANES, :] = x[ts - SUBLANES:, :]

    xcb = xc.astype(BF16)
    pre_r, pre_i = [], []
    for g in range(width // MXU_DIM):
        sl = slice(g * MXU_DIM, (g + 1) * MXU_DIM)
        pre_r.append(jnp.dot(xcb[:, sl], wa_ref[g], preferred_element_type=F32))
        pre_i.append(jnp.dot(xcb[:, sl], wx_ref[g], preferred_element_type=F32))
    gate_r = _sigmoid(jnp.concatenate(pre_r, axis=-1) + ba_ref[...])
    gate_i = _sigmoid(jnp.concatenate(pre_i, axis=-1) + bx_ref[...])

    log_a = (-RG_C) * gate_r * _softplus(-lam_ref[...])
    a = jnp.exp(log_a)
    b = jnp.sqrt(1.0 - a * a) * (gate_i * xc)

    a = a.reshape(ts // SUBLANES, SUBLANES, width)
    b = b.reshape(ts // SUBLANES, SUBLANES, width)
    in_group = lax.broadcasted_iota(jnp.int32, (1, SUBLANES, 1), 1)
    d = 1
    while d < SUBLANES:
        m = in_group >= d
        b = jnp.where(m, a, 0.0) * pltpu.roll(b, d, 1) + b
        a = a * jnp.where(m, pltpu.roll(a, d, 1), 1.0)
        d *= 2
    a_ref[...] = a.reshape(ts, width)
    b_ref[...] = b.reshape(ts, width)

    def carry_group(i, h):
        rows = pl.ds(pl.multiple_of(i * SUBLANES, SUBLANES), SUBLANES)
        hg = a_ref[rows, :] * h + b_ref[rows, :]
        b_ref[rows, :] = hg
        return hg[SUBLANES - 1:, :]

    h_ref[...] = lax.fori_loop(0, ts // SUBLANES, carry_group, h_ref[...], unroll=8)
    o_ref[...] = (b_ref[...] * _silu(g_ref[...].astype(F32))).astype(o_ref.dtype)


def _rglru(proj4, conv_w, conv_b, wa_bd, b_a, wx_bd, b_x, lam, ts=256):
    _, bsz, seq, width = proj4.shape
    row = lambda: pl.BlockSpec((1, width), lambda b, s: (0, 0))
    return pl.pallas_call(
        _rglru_kernel,
        grid=(bsz, seq // ts),
        in_specs=[pl.BlockSpec((None, None, ts, width), lambda b, s: (0, b, s, 0)),
                  pl.BlockSpec((None, None, ts, width), lambda b, s: (1, b, s, 0)),
                  pl.BlockSpec((RG_CONV, width), lambda b, s: (0, 0)),
                  row(),
                  pl.BlockSpec(wa_bd.shape, lambda b, s: (0, 0, 0)),
                  row(),
                  pl.BlockSpec(wx_bd.shape, lambda b, s: (0, 0, 0)),
                  row(), row()],
        out_specs=pl.BlockSpec((None, ts, width), lambda b, s: (b, s, 0)),
        out_shape=jax.ShapeDtypeStruct((bsz, seq, width), BF16),
        scratch_shapes=[pltpu.VMEM((1, width), F32), pltpu.VMEM((SUBLANES + ts, width), F32),
                        pltpu.VMEM((ts, width), F32), pltpu.VMEM((ts, width), F32)],
        compiler_params=_params("parallel", "arbitrary"),
        name="l0_rglru",
    )(proj4, proj4, conv_w, conv_b, wa_bd, b_a, wx_bd, b_x, lam)


def _hgrn2_kernel(q_ref, f_ref, v_ref, g_ref, lbp_ref, gn_ref, o_ref, st_ref):
    @pl.when(pl.program_id(2) == 0)
    def _():
        st_ref[...] = jnp.zeros_like(st_ref)

    lbp = lbp_ref[...]
    e = jnp.exp(lbp - jnp.max(lbp, axis=0, keepdims=True))
    p = e / jnp.sum(e, axis=0, keepdims=True)
    lb = (p[0:1] + p[1:2]) - p[0:1]
    gn = gn_ref[...]
    ts = q_ref.shape[0]
    row = lax.broadcasted_iota(jnp.int32, (CHUNK, LANES), 0)
    srow = lax.broadcasted_iota(jnp.int32, (CHUNK, CHUNK), 0)
    scol = lax.broadcasted_iota(jnp.int32, (CHUNK, CHUNK), 1)
    diag_lane0 = srow & ~(SUB - 1)
    causal = scol <= srow
    levels = []
    h = CHUNK // 2
    while h >= SUB:
        upper = (row & (2 * h - 1)) >= h
        same_block = (srow & ~(2 * h - 1)) == (scol & ~(2 * h - 1))
        levels.append((h, upper, same_block))
        h //= 2

    def block_rows(x, size, r):
        return jnp.concatenate([jnp.broadcast_to(x[lo + r:lo + r + 1], (size, LANES))
                                for lo in range(0, CHUNK, size)], axis=0)

    def group(i, carry):
        idx = range(HG_GROUP)
        rows = [pl.ds(pl.multiple_of((i * HG_GROUP + j) * CHUNK, CHUNK), CHUNK) for j in idx]
        q = [_silu(q_ref[r, :].astype(F32)) for r in rows]
        sig = [_sigmoid(f_ref[r, :].astype(F32)) for r in rows]
        k = [(1.0 - lb) * (1.0 - x) for x in sig]
        v = [v_ref[r, :].astype(BF16) for r in rows]
        gcum = [_cumsum_rows(jnp.log(lb + (1.0 - lb) * x)) for x in sig]
        gend = [x[CHUNK - 1:CHUNK, :] for x in gcum]
        qg = [(a * jnp.exp(b)).astype(BF16) for a, b in zip(q, gcum)]
        vtk = [_dot_tn(a, b * jnp.exp(e - c)) for a, b, c, e in zip(v, k, gcum, gend)]

        scores = [None for _ in idx]
        for h, upper, same_block in levels:
            for j in idx:
                e = jnp.exp(-jnp.abs(gcum[j] - block_rows(gcum[j], 2 * h, h - 1)))
                s = _dot_nt(jnp.where(upper, q[j] * e, 0.0), jnp.where(upper, 0.0, k[j] * e))
                s = jnp.where(same_block, s, 0.0)
                scores[j] = s if scores[j] is None else scores[j] + s

        g2 = [x * LOG2E for x in gcum]
        gk = [a - jnp.log2(b) for a, b in zip(g2, k)]
        sc = [jnp.zeros((CHUNK, CHUNK), F32) for _ in idx]
        for s in range(SUB):
            for j in idx:
                w = jnp.exp2(g2[j] - block_rows(gk[j], SUB, s))
                col = jnp.sum(q[j] * w, axis=-1, keepdims=True)
                sc[j] = jnp.where(scol == diag_lane0 + s, col, sc[j])
        intra = [_dot(a + jnp.where(causal, b, 0.0), c) for a, b, c in zip(scores, sc, v)]

        st = st_ref[...]
        for j in idx:
            o = _dot_nt(qg[j], st) + intra[j]
            st = st * jnp.exp(gend[j]) + vtk[j]
            o = _rms(o, gn) * _silu(g_ref[rows[j], :].astype(F32))
            o_ref[rows[j], :] = o.astype(o_ref.dtype)
        st_ref[...] = st
        return carry

    lax.fori_loop(0, ts // (CHUNK * HG_GROUP), group, 0)


def _hgrn2(proj4, lb_params, g_norm, ts=512):
    _, bsz, seq, width = proj4.shape
    spec = lambda j: pl.BlockSpec((None, None, ts, HG_DK), lambda b, h, s: (j, b, s, h))
    return pl.pallas_call(
        _hgrn2_kernel,
        grid=(bsz, HG_HEADS, seq // ts),
        in_specs=[spec(2), spec(3), spec(4), spec(5),
                  pl.BlockSpec((lb_params.shape[0], HG_DK), lambda b, h, s: (0, h)),
                  pl.BlockSpec((1, HG_DK), lambda b, h, s: (0, 0))],
        out_specs=pl.BlockSpec((None, ts, HG_DK), lambda b, h, s: (b, s, h)),
        out_shape=jax.ShapeDtypeStruct((bsz, seq, width), BF16),
        scratch_shapes=[pltpu.VMEM((HG_DK, HG_DK), F32)],
        compiler_params=_params("parallel", "parallel", "arbitrary"),
        name="l0_hgrn2",
    )(proj4, proj4, proj4, proj4, lb_params, g_norm)


def _out_proj0_kernel(hg_ref, og_ref, w_ref, pn_ref, x_ref, o_ref):
    half = hg_ref.shape[1]
    y = (jnp.dot(hg_ref[...], w_ref[:half, :], preferred_element_type=F32)
         + jnp.dot(og_ref[...], w_ref[half:, :], preferred_element_type=F32))
    o_ref[...] = x_ref[...] + _rms(y, pn_ref[...])


def _out_proj0(hg, og, w_bf16, post_gain, x2d, tm=512):
    t, d = x2d.shape
    tok = lambda: pl.BlockSpec((tm, d), lambda i: (i, 0))
    return pl.pallas_call(
        _out_proj0_kernel,
        grid=(t // tm,),
        in_specs=[tok(), tok(),
                  pl.BlockSpec(w_bf16.shape, lambda i: (0, 0)),
                  pl.BlockSpec((1, d), lambda i: (0, 0)),
                  tok()],
        out_specs=tok(),
        out_shape=jax.ShapeDtypeStruct((t, d), F32),
        compiler_params=_params("parallel"),
        name="l0_out_proj",
    )(hg, og, w_bf16, post_gain, x2d)


def _rwkv_front_kernel(x_ref, pn_ref, mu_ref, win_ref, w0_ref, w1_ref, w2_ref, a0_ref, a1_ref,
                       a2_ref, kk_ref, ka_ref, ones_ref,
                       r_out, lw_out, k_out, v_out, kk_out, b_out, g_out, prev_ref):
    @pl.when(pl.program_id(1) == 0)
    def _():
        prev_ref[...] = jnp.zeros_like(prev_ref)

    u = _rms(x_ref[...], pn_ref[...])
    tm = u.shape[0]
    row = lax.broadcasted_iota(jnp.int32, (tm, 1), 0)
    shifted = jnp.where(row == 0, prev_ref[...], pltpu.roll(u, 1, 0))
    prev_ref[...] = u[tm - 1:tm, :]
    delta = shifted - u
    mix = lambda p: u + delta * mu_ref[p:p + 1, :]

    r = _dot(mix(0), win_ref[0])
    k = _dot(mix(1), win_ref[1])
    v = _dot(mix(2), win_ref[2])
    g = _dot(mix(3), win_ref[3])
    lora_w = _dot(jnp.tanh(_dot(mix(4), w1_ref[...])), w2_ref[...])
    w_log = -_softplus(-(w0_ref[...] + lora_w)) - 0.5
    iclr = _sigmoid(a0_ref[...] + _dot(_dot(mix(5), a1_ref[...]), a2_ref[...]))

    kk = k * kk_ref[...]
    norm = jnp.sqrt(_seg_sum(kk * kk, ones_ref[...]))
    kk = kk / jnp.maximum(norm, 1e-12)

    r_out[...] = r.astype(r_out.dtype)
    lw_out[...] = -jnp.exp(w_log)
    k_out[...] = (k * (1.0 + (iclr - 1.0) * ka_ref[...])).astype(k_out.dtype)
    v_out[...] = v.astype(v_out.dtype)
    kk_out[...] = kk.astype(kk_out.dtype)
    b_out[...] = (kk * iclr).astype(b_out.dtype)
    g_out[...] = g.astype(g_out.dtype)


def _rwkv_front(x1, pre_gain, mu, win, w0, w1, w2, a0, a1, a2, k_k, k_a, ones_bd, tm=256):
    bsz, seq, d = x1.shape
    tok = lambda: pl.BlockSpec((None, tm, d), lambda b, s: (b, s, 0))
    full = lambda a: pl.BlockSpec(a.shape, lambda b, s: (0,) * a.ndim)
    args = (pre_gain, mu, win, w0, w1, w2, a0, a1, a2, k_k, k_a, ones_bd)
    return pl.pallas_call(
        _rwkv_front_kernel,
        grid=(bsz, seq // tm),
        in_specs=[tok()] + [full(a) for a in args],
        out_specs=[tok() for _ in range(7)],
        out_shape=[jax.ShapeDtypeStruct((bsz, seq, d), F32 if i == 1 else BF16) for i in range(7)],
        scratch_shapes=[pltpu.VMEM((1, d), F32)],
        compiler_params=_params("parallel", "arbitrary"),
        name="l1_front",
    )(x1, *args)


def _rwkv_scan_kernel(r_ref, lw_ref, k_ref, v_ref, kk_ref, b_ref, y_ref,
                      ht_ref, rt_s, y0_s, ml_s, ha_s, gm_s):
    @pl.when(pl.program_id(2) == 0)
    def _():
        ht_ref[...] = jnp.zeros_like(ht_ref)

    ts, width = r_ref.shape
    nchunk = ts // CHUNK
    npair = width // LANES
    lanes_of = [slice(p * LANES, (p + 1) * LANES) for p in range(npair)]
    two = 2 * CHUNK
    head0 = lax.broadcasted_iota(jnp.int32, (CHUNK, LANES), 1) < RW_HEAD
    row = lax.broadcasted_iota(jnp.int32, (two, two), 0)
    col = lax.broadcasted_iota(jnp.int32, (two, two), 1)
    strict = (row & (CHUNK - 1)) > (col & (CHUNK - 1))
    incl = (row & (CHUNK - 1)) >= (col & (CHUNK - 1))
    eye = jnp.where(row == col, 1.0, 0.0)

    def stack(x):
        return jnp.concatenate([jnp.where(head0, x, 0.0), jnp.where(head0, 0.0, x)], axis=0)

    def prep(c):
        rows = pl.ds(pl.multiple_of(c * CHUNK, CHUNK), CHUNK)
        r, lw, k, v, kk, b = (ref[rows, :].astype(F32)
                              for ref in (r_ref, lw_ref, k_ref, v_ref, kk_ref, b_ref))
        lcum = _cumsum_rows(lw)
        lend = lcum[CHUNK - 1:CHUNK, :]
        e_neg = jnp.exp(-lcum)
        e_end = jnp.exp(lend - lcum)
        gm_s[c] = jnp.exp(lend)
        at, rt = -kk * jnp.exp(lcum - lw), r * jnp.exp(lcum)
        bt, kt, bh, kh = b * e_neg, k * e_neg, b * e_end, k * e_end
        return [dict(rows=rows, c=c, p=p, sl=sl,
                     at2=stack(at[:, sl]).astype(BF16),
                     rt2=stack(rt[:, sl]),
                     bk2=jnp.concatenate([stack(bt[:, sl]), stack(kt[:, sl])], axis=0).astype(BF16),
                     v2=stack(v[:, sl]).astype(BF16),
                     bkh2=jnp.concatenate([stack(bh[:, sl]), stack(kh[:, sl])], axis=0).astype(BF16))
                for p, sl in enumerate(lanes_of)]

    def local_group(i, carry):
        g = [q for j in range(SCAN_GROUP) for q in prep(i * SCAN_GROUP + j)]
        s = [_dot_nt(jnp.concatenate([q["at2"], q["rt2"].astype(BF16)], axis=0), q["bk2"]) for q in g]
        n = [jnp.where(strict, x[:two, :two], 0.0) for x in s]
        a_ak = [jnp.where(strict, x[:two, two:], 0.0).astype(BF16) for x in s]
        a_rb = [jnp.where(incl, x[two:, :two], 0.0).astype(BF16) for x in s]
        a_rk = [jnp.where(incl, x[two:, two:], 0.0).astype(BF16) for x in s]
        t = [eye + x for x in n]
        pw = [_dot(x, x).astype(BF16) for x in n]
        d = 2
        while d < CHUNK // 2:
            m = [_dot(jnp.concatenate([a.astype(BF16), b], axis=0), b) for a, b in zip(t, pw)]
            t = [a + x[:two] for a, x in zip(t, m)]
            pw = [x[two:].astype(BF16) for x in m]
            d *= 2
        t = [a + _dot(a, b) for a, b in zip(t, pw)]
        akv = [_dot(x, q["v2"]) for x, q in zip(a_ak, g)]
        tw = [_dot(x, jnp.concatenate([q["at2"], y.astype(BF16)], axis=1)).astype(BF16)
              for x, y, q in zip(t, akv, g)]
        ry = [_dot(x, y) for x, y in zip(a_rb, tw)]
        rkv = [_dot(x, q["v2"]) for x, q in zip(a_rk, g)]
        ml = [_dot_tn(q["bkh2"][:two], y[:, :LANES]) for q, y in zip(g, tw)]
        ha = [_dot_tn(jnp.concatenate([y[:, LANES:], q["v2"]], axis=0), q["bkh2"]) for q, y in zip(g, tw)]
        for j, q in enumerate(g):
            rtp = q["rt2"] + ry[j][:, :LANES]
            y0 = ry[j][:, LANES:] + rkv[j]
            rt_s[q["rows"], q["sl"]] = (rtp[:CHUNK] + rtp[CHUNK:]).astype(BF16)
            y0_s[q["rows"], q["sl"]] = y0[:CHUNK] + y0[CHUNK:]
            ml_s[q["c"], q["p"]] = ml[j].astype(BF16)
            ha_s[q["c"], q["p"]] = ha[j]
        return carry

    lax.fori_loop(0, nchunk // SCAN_GROUP, local_group, 0)

    def step(c, hts):
        rows = pl.ds(pl.multiple_of(c * CHUNK, CHUNK), CHUNK)
        rt, gm = rt_s[rows, :], gm_s[c]
        htb = [h.astype(BF16) for h in hts]
        ys = [_dot_nt(rt[:, sl], hb) for sl, hb in zip(lanes_of, htb)]
        new = [h * gm[:, sl] + _dot_nt(hb, ml_s[c, p]) + ha_s[c, p]
               for p, (sl, h, hb) in enumerate(zip(lanes_of, hts, htb))]
        y_ref[rows, :] = (jnp.concatenate(ys, axis=1) + y0_s[rows, :]).astype(y_ref.dtype)
        return tuple(new)

    hts = lax.fori_loop(0, nchunk, step, tuple(ht_ref[p] for p in range(npair)))
    for p in range(npair):
        ht_ref[p] = hts[p]


def _rwkv_scan(r, lw, k, v, kk, b, ts=512, width=512):
    bsz, seq, d = r.shape
    nchunk = ts // CHUNK
    npair = width // LANES
    spec = lambda: pl.BlockSpec((None, ts, width), lambda bb, h, s: (bb, s, h))
    return pl.pallas_call(
        _rwkv_scan_kernel,
        grid=(bsz, d // width, seq // ts),
        in_specs=[spec() for _ in range(6)],
        out_specs=spec(),
        out_shape=jax.ShapeDtypeStruct((bsz, seq, d), BF16),
        scratch_shapes=[pltpu.VMEM((npair, LANES, LANES), F32),
                        pltpu.VMEM((ts, width), BF16),
                        pltpu.VMEM((ts, width), F32),
                        pltpu.VMEM((nchunk, npair, LANES, LANES), BF16),
                        pltpu.VMEM((nchunk, npair, LANES, LANES), F32),
                        pltpu.VMEM((nchunk, 1, width), F32)],
        compiler_params=_params("parallel", "parallel", "arbitrary"),
        name="l1_scan",
    )(r, lw, k, v, kk, b)


def _rwkv_back_kernel(y_ref, r_ref, k_ref, v_ref, g_ref, x_ref, rk_ref, lnw_ref, lnb_ref,
                      w_ref, pn_ref, ones_ref, o_ref):
    ones = ones_ref[...]
    inv_n = 1.0 / RW_HEAD
    y = y_ref[...].astype(F32)
    yc = y - _seg_sum(y, ones, split=False) * inv_n
    var = _seg_sum(yc * yc, ones, split=False) * inv_n
    yn = yc * lax.rsqrt(var + RW_GN_EPS) * lnw_ref[...] + lnb_ref[...]
    rk = r_ref[...].astype(F32) * k_ref[...].astype(F32) * rk_ref[...]
    yn = yn + _seg_sum(rk, ones, split=False) * v_ref[...].astype(F32)
    out = _dot(yn * _silu(g_ref[...].astype(F32)), w_ref[...])
    o_ref[...] = x_ref[...] + _rms(out, pn_ref[...])


def _rwkv_back(y, r, k, v, g, x1, r_k, ln_w, ln_b, w_bf16, post_gain, ones_bd, tm=512):
    t, d = y.shape
    tok = lambda: pl.BlockSpec((tm, d), lambda i: (i, 0))
    full = lambda a: pl.BlockSpec(a.shape, lambda i: (0,) * a.ndim)
    params = (r_k, ln_w, ln_b, w_bf16, post_gain, ones_bd)
    return pl.pallas_call(
        _rwkv_back_kernel,
        grid=(t // tm,),
        in_specs=[tok() for _ in range(6)] + [full(a) for a in params],
        out_specs=tok(),
        out_shape=jax.ShapeDtypeStruct((t, d), F32),
        compiler_params=_params("parallel"),
        name="l1_back",
    )(y, r, k, v, g, x1, *params)


def _block_diag(w, group):
    n, bi, bj = w.shape
    w = w.reshape(n // group, group, bi, bj)
    eye = jnp.eye(group, dtype=w.dtype)
    return jnp.einsum("gaij,ab->gaibj", w, eye).reshape(n // group, group * bi, group * bj)


def kernel(x, pre_norm, post_norm, ab_w_in, ab_w_out, rg_conv_w, rg_conv_b, rg_w_a, rg_b_a,
           rg_w_x, rg_b_x, rg_lambda, hg_lower_bounds, hg_out_norm, rw_mu, rw_w_in, rw_w_out,
           rw_w0, rw_w1, rw_w2, rw_a0, rw_a1, rw_a2, rw_k_k, rw_k_a, rw_r_k, rw_ln_w, rw_ln_b):
    bsz, seq, d = x.shape
    t = bsz * seq
    row = lambda a: a.reshape(1, -1).astype(F32)
    x2d = x.reshape(t, d)
    heads_per_tile = MXU_DIM // RW_HEAD
    ones_bd = _block_diag(jnp.ones((heads_per_tile, RW_HEAD, RW_HEAD), F32), heads_per_tile)[0].astype(BF16)

    proj = _norm_proj(x2d, row(pre_norm[0]), ab_w_in[0].astype(BF16))
    proj4 = proj.reshape(proj.shape[0], bsz, seq, d)
    rg_group = MXU_DIM // rg_w_a.shape[-1]
    hg = _rglru(proj4, rg_conv_w[0].astype(F32), row(rg_conv_b[0]),
                _block_diag(rg_w_a[0], rg_group).astype(BF16), row(rg_b_a[0]),
                _block_diag(rg_w_x[0], rg_group).astype(BF16), row(rg_b_x[0]), row(rg_lambda[0]))
    og = _hgrn2(proj4, hg_lower_bounds.astype(F32), row(hg_out_norm[0]))
    x1 = _out_proj0(hg.reshape(t, d), og.reshape(t, d), ab_w_out[0].astype(BF16),
                    row(post_norm[0]), x2d)

    r, lw, k, v, kk, b, g = _rwkv_front(
        x1.reshape(bsz, seq, d), row(pre_norm[1]), rw_mu[0].astype(F32), rw_w_in[0].astype(BF16),
        row(rw_w0[0]), rw_w1[0].astype(BF16), rw_w2[0].astype(BF16), row(rw_a0[0]),
        rw_a1[0].astype(BF16), rw_a2[0].astype(BF16), row(rw_k_k[0]), row(rw_k_a[0]), ones_bd)
    y = _rwkv_scan(r, lw, k, v, kk, b)
    flat = lambda a: a.reshape(t, d)
    x2 = _rwkv_back(flat(y), flat(r), flat(k), flat(v), flat(g), x1, row(rw_r_k[0]),
                    row(rw_ln_w[0]), row(rw_ln_b[0]), rw_w_out[0].astype(BF16),
                    row(post_norm[1]), ones_bd)
    return x2.reshape(bsz, seq, d)
```

```python
import functools

import jax
import jax.numpy as jnp
from jax import lax
from jax.experimental import pallas as pl
from jax.experimental.pallas import tpu as pltpu

F32 = jnp.float32
BF16 = jnp.bfloat16

D_MODEL = 1024
RMS_EPS = 1e-6
RG_C = 8.0
RG_CONV = 4
HG_HEADS = 8
HG_DK = 128
RW_HEAD = 64
RW_GN_EPS = 64e-5

LANES = 128
SUBLANES = 8
MXU_DIM = 256
VMEM_LIMIT = 56 * 1024 * 1024

CHUNK = 64
SUB = 8
SCAN_GROUP = 2
HG_GROUP = 8
EXP_NEG_HALF = 0.6065306597126334


def _params(*sem):
    return pltpu.CompilerParams(dimension_semantics=sem, vmem_limit_bytes=VMEM_LIMIT)


def _dot(a, b):
    return jnp.dot(a.astype(BF16), b.astype(BF16), preferred_element_type=F32)


def _dot_nt(a, b):
    return lax.dot_general(a.astype(BF16), b.astype(BF16), (((1,), (1,)), ((), ())),
                           preferred_element_type=F32)


def _dot_tn(a, b):
    return lax.dot_general(a.astype(BF16), b.astype(BF16), (((0,), (0,)), ((), ())),
                           preferred_element_type=F32)


def _sigmoid(x):
    return 1.0 / (1.0 + jnp.exp(-x))


def _silu(x):
    return x * _sigmoid(x)


def _softplus(z):
    return jnp.maximum(z, 0.0) + jnp.log(1.0 + jnp.exp(-jnp.abs(z)))


def _rms(x, gain):
    return x * lax.rsqrt(jnp.mean(x * x, axis=-1, keepdims=True) + RMS_EPS) * gain


def _cumsum_rows(x):
    n = x.shape[0]
    row = lax.broadcasted_iota(jnp.int32, x.shape, 0)
    d = 1
    while d < n:
        x = x + jnp.where(row >= d, pltpu.roll(x, d, 0), 0.0)
        d *= 2
    return x


def _seg_sum(x, ones_bd, split=True):
    hi = x.astype(BF16)
    lo = (x - hi.astype(F32)).astype(BF16) if split else None
    outs = []
    for g in range(x.shape[1] // MXU_DIM):
        sl = slice(g * MXU_DIM, (g + 1) * MXU_DIM)
        acc = jnp.dot(hi[:, sl], ones_bd, preferred_element_type=F32)
        if split:
            acc = acc + jnp.dot(lo[:, sl], ones_bd, preferred_element_type=F32)
        outs.append(acc)
    return jnp.concatenate(outs, axis=-1)


def _norm_proj_kernel(x_ref, g_ref, w_ref, o_ref, u_ref):
    @pl.when(pl.program_id(1) == 0)
    def _():
        u_ref[...] = _rms(x_ref[...], g_ref[...]).astype(BF16)

    o_ref[...] = jnp.dot(u_ref[...], w_ref[...], preferred_element_type=F32).astype(o_ref.dtype)


def _norm_proj(x2d, gain, w_bf16, tm=1024, tn=1024):
    t, d = x2d.shape
    n = w_bf16.shape[1]
    return pl.pallas_call(
        _norm_proj_kernel,
        grid=(t // tm, n // tn),
        in_specs=[pl.BlockSpec((tm, d), lambda i, j: (i, 0)),
                  pl.BlockSpec((1, d), lambda i, j: (0, 0)),
                  pl.BlockSpec((d, tn), lambda i, j: (0, j))],
        out_specs=pl.BlockSpec((None, tm, tn), lambda i, j: (j, i, 0)),
        out_shape=jax.ShapeDtypeStruct((n // tn, t, tn), BF16),
        scratch_shapes=[pltpu.VMEM((tm, d), BF16)],
        compiler_params=_params("parallel", "arbitrary"),
        name="l0_norm_proj",
    )(x2d, gain, w_bf16)


def _rglru_kernel(x_ref, g_ref, cw_ref, cb_ref, wa_ref, ba_ref, wx_ref, bx_ref, lam_ref,
                  o_ref, h_ref, ext_ref, a_ref, b_ref):
    @pl.when(pl.program_id(1) == 0)
    def _():
        h_ref[...] = jnp.zeros_like(h_ref)
        ext_ref[:SUBLANES, :] = jnp.zeros((SUBLANES, ext_ref.shape[1]), F32)

    x = x_ref[...].astype(F32)
    ts, width = x.shape
    ext_ref[SUBLANES:, :] = x
    xc = cb_ref[...] + cw_ref[RG_CONV - 1:RG_CONV, :] * x
    for k in range(1, RG_CONV):
        xc = xc + cw_ref[RG_CONV - 1 - k:RG_CONV - k, :] * ext_ref[SUBLANES - k:SUBLANES - k + ts, :]
    ext_ref[:SUBLANES, :] = x[ts - SUBLANES:, :]

    xcb = xc.astype(BF16)
    pre_r, pre_i = [], []
    for g in range(width // MXU_DIM):
        sl = slice(g * MXU_DIM, (g + 1) * MXU_DIM)
        pre_r.append(jnp.dot(xcb[:, sl], wa_ref[g], preferred_element_type=F32))
        pre_i.append(jnp.dot(xcb[:, sl], wx_ref[g], preferred_element_type=F32))
    gate_r = _sigmoid(jnp.concatenate(pre_r, axis=-1) + ba_ref[...])
    gate_i = _sigmoid(jnp.concatenate(pre_i, axis=-1) + bx_ref[...])

    log_a = (-RG_C) * gate_r * _softplus(-lam_ref[...])
    a = jnp.exp(log_a)
    b = jnp.sqrt(1.0 - a * a) * (gate_i * xc)

    a = a.reshape(ts // SUBLANES, SUBLANES, width)
    b = b.reshape(ts // SUBLANES, SUBLANES, width)
    in_group = lax.broadcasted_iota(jnp.int32, (1, SUBLANES, 1), 1)
    d = 1
    while d < SUBLANES:
        m = in_group >= d
        b = jnp.where(m, a, 0.0) * pltpu.roll(b, d, 1) + b
        a = a * jnp.where(m, pltpu.roll(a, d, 1), 1.0)
        d *= 2
    a_ref[...] = a.reshape(ts, width)
    b_ref[...] = b.reshape(ts, width)

    def carry_group(i, h):
        rows = pl.ds(pl.multiple_of(i * SUBLANES, SUBLANES), SUBLANES)
        hg = a_ref[rows, :] * h + b_ref[rows, :]
        b_ref[rows, :] = hg
        return hg[SUBLANES - 1:, :]

    h_ref[...] = lax.fori_loop(0, ts // SUBLANES, carry_group, h_ref[...], unroll=8)
    o_ref[...] = (b_ref[...] * _silu(g_ref[...].astype(F32))).astype(o_ref.dtype)


def _rglru(proj4, conv_w, conv_b, wa_bd, b_a, wx_bd, b_x, lam, ts=256):
    _, bsz, seq, width = proj4.shape
    row = lambda: pl.BlockSpec((1, width), lambda b, s: (0, 0))
    return pl.pallas_call(
        _rglru_kernel,
        grid=(bsz, seq // ts),
        in_specs=[pl.BlockSpec((None, None, ts, width), lambda b, s: (0, b, s, 0)),
                  pl.BlockSpec((None, None, ts, width), lambda b, s: (1, b, s, 0)),
                  pl.BlockSpec((RG_CONV, width), lambda b, s: (0, 0)),
                  row(),
                  pl.BlockSpec(wa_bd.shape, lambda b, s: (0, 0, 0)),
                  row(),
                  pl.BlockSpec(wx_bd.shape, lambda b, s: (0, 0, 0)),
                  row(), row()],
        out_specs=pl.BlockSpec((None, ts, width), lambda b, s: (b, s, 0)),
        out_shape=jax.ShapeDtypeStruct((bsz, seq, width), BF16),
        scratch_shapes=[pltpu.VMEM((1, width), F32), pltpu.VMEM((SUBLANES + ts, width), F32),
                        pltpu.VMEM((ts, width), F32), pltpu.VMEM((ts, width), F32)],
        compiler_params=_params("parallel", "arbitrary"),
        name="l0_rglru",
    )(proj4, proj4, conv_w, conv_b, wa_bd, b_a, wx_bd, b_x, lam)


def _hgrn2_kernel(q_ref, f_ref, v_ref, g_ref, lbp_ref, gn_ref, o_ref, st_ref):
    @pl.when(pl.program_id(2) == 0)
    def _():
        st_ref[...] = jnp.zeros_like(st_ref)

    lbp = lbp_ref[...]
    e = jnp.exp(lbp - jnp.max(lbp, axis=0, keepdims=True))
    p = e / jnp.sum(e, axis=0, keepdims=True)
    lb = (p[0:1] + p[1:2]) - p[0:1]
    gn = gn_ref[...]
    ts = q_ref.shape[0]
    srow = lax.broadcasted_iota(jnp.int32, (CHUNK, CHUNK), 0)
    scol = lax.broadcasted_iota(jnp.int32, (CHUNK, CHUNK), 1)
    diag_lane0 = srow & ~(SUB - 1)
    causal = scol <= srow
    levels = []
    h = CHUNK // 2
    while h >= SUB:
        levels.append((h, (srow & ~(2 * h - 1)) == (scol & ~(2 * h - 1))))
        h //= 2
    tril3 = jnp.where(srow >= scol, 1.0, 0.0).astype(BF16)
    tril3 = jnp.concatenate([tril3, tril3, tril3], axis=1)

    def cumsum_rows(x):
        hi = x.astype(BF16)
        r1 = x - hi.astype(F32)
        mid = r1.astype(BF16)
        lo = (r1 - mid.astype(F32)).astype(BF16)
        return jnp.dot(tril3, jnp.concatenate([hi, mid, lo], axis=0), preferred_element_type=F32)

    def block_rows(x, size, r):
        return jnp.concatenate([jnp.broadcast_to(x[lo + r:lo + r + 1], (size, LANES))
                                for lo in range(0, CHUNK, size)], axis=0)

    def level_factors(q, k, g2, h):
        zero = jnp.zeros((h, LANES), F32)
        qt, kt = [], []
        for lo in range(0, CHUNK, 2 * h):
            mid = lo + h
            rho = g2[mid - 1:mid, :]
            kt += [k[lo:mid] * jnp.exp2(rho - g2[lo:mid]), zero]
            qt += [zero, q[mid:mid + h] * jnp.exp2(g2[mid:mid + h] - rho)]
        return jnp.concatenate(qt, axis=0), jnp.concatenate(kt, axis=0)

    def group(i, carry):
        idx = range(HG_GROUP)
        rows = [pl.ds(pl.multiple_of((i * HG_GROUP + j) * CHUNK, CHUNK), CHUNK) for j in idx]
        q = [_silu(q_ref[r, :].astype(F32)) for r in rows]
        sig = [_sigmoid(f_ref[r, :].astype(F32)) for r in rows]
        k = [(1.0 - lb) * (1.0 - x) for x in sig]
        v = [v_ref[r, :].astype(BF16) for r in rows]
        g2 = [cumsum_rows(jnp.log2(lb + (1.0 - lb) * x)) for x in sig]
        gend = [x[CHUNK - 1:CHUNK, :] for x in g2]
        qg = [(a * jnp.exp2(b)).astype(BF16) for a, b in zip(q, g2)]
        vtk = [_dot_tn(a, b * jnp.exp2(e - c)) for a, b, c, e in zip(v, k, g2, gend)]

        scores = [None for _ in idx]
        for h, same_block in levels:
            for j in idx:
                qt, kt = level_factors(q[j], k[j], g2[j], h)
                s = jnp.where(same_block, _dot_nt(qt, kt), 0.0)
                scores[j] = s if scores[j] is None else scores[j] + s

        gk = [a - jnp.log2(b) for a, b in zip(g2, k)]
        sc = [jnp.zeros((CHUNK, CHUNK), F32) for _ in idx]
        for s in range(SUB):
            for j in idx:
                w = jnp.exp2(g2[j] - block_rows(gk[j], SUB, s))
                col = jnp.sum(q[j] * w, axis=-1, keepdims=True)
                sc[j] = jnp.where(scol == diag_lane0 + s, col, sc[j])
        intra = [_dot(a + jnp.where(causal, b, 0.0), c) for a, b, c in zip(scores, sc, v)]

        st = st_ref[...]
        for j in idx:
            o = _dot_nt(qg[j], st) + intra[j]
            st = st * jnp.exp2(gend[j]) + vtk[j]
            o = _rms(o, gn) * _silu(g_ref[rows[j], :].astype(F32))
            o_ref[rows[j], :] = o.astype(o_ref.dtype)
        st_ref[...] = st
        return carry

    lax.fori_loop(0, ts // (CHUNK * HG_GROUP), group, 0)


def _hgrn2(proj4, lb_params, g_norm, ts=1024):
    _, bsz, seq, width = proj4.shape
    spec = lambda j: pl.BlockSpec((None, None, ts, HG_DK), lambda b, h, s: (j, b, s, h))
    return pl.pallas_call(
        _hgrn2_kernel,
        grid=(bsz, HG_HEADS, seq // ts),
        in_specs=[spec(2), spec(3), spec(4), spec(5),
                  pl.BlockSpec((lb_params.shape[0], HG_DK), lambda b, h, s: (0, h)),
                  pl.BlockSpec((1, HG_DK), lambda b, h, s: (0, 0))],
        out_specs=pl.BlockSpec((None, ts, HG_DK), lambda b, h, s: (b, s, h)),
        out_shape=jax.ShapeDtypeStruct((bsz, seq, width), BF16),
        scratch_shapes=[pltpu.VMEM((HG_DK, HG_DK), F32)],
        compiler_params=_params("parallel", "parallel", "arbitrary"),
        name="l0_hgrn2",
    )(proj4, proj4, proj4, proj4, lb_params, g_norm)


def _out_proj0_kernel(hg_ref, og_ref, w_ref, pn_ref, x_ref, o_ref):
    half = hg_ref.shape[1]
    y = (jnp.dot(hg_ref[...], w_ref[:half, :], preferred_element_type=F32)
         + jnp.dot(og_ref[...], w_ref[half:, :], preferred_element_type=F32))
    o_ref[...] = x_ref[...] + _rms(y, pn_ref[...])


def _out_proj0(hg, og, w_bf16, post_gain, x2d, tm=512):
    t, d = x2d.shape
    tok = lambda: pl.BlockSpec((tm, d), lambda i: (i, 0))
    return pl.pallas_call(
        _out_proj0_kernel,
        grid=(t // tm,),
        in_specs=[tok(), tok(),
                  pl.BlockSpec(w_bf16.shape, lambda i: (0, 0)),
                  pl.BlockSpec((1, d), lambda i: (0, 0)),
                  tok()],
        out_specs=tok(),
        out_shape=jax.ShapeDtypeStruct((t, d), F32),
        compiler_params=_params("parallel"),
        name="l0_out_proj",
    )(hg, og, w_bf16, post_gain, x2d)


def _rwkv_front_kernel(x_ref, pn_ref, mu_ref, win_ref, w0_ref, w1_ref, w2_ref, a0_ref, a1_ref,
                       a2_ref, kk_ref, ka_ref, ones_ref,
                       r_out, lw_out, k_out, v_out, kk_out, b_out, g_out, prev_ref):
    @pl.when(pl.program_id(1) == 0)
    def _():
        prev_ref[...] = jnp.zeros_like(prev_ref)

    u = _rms(x_ref[...], pn_ref[...])
    tm = u.shape[0]
    row = lax.broadcasted_iota(jnp.int32, (tm, 1), 0)
    delta = jnp.where(row == 0, prev_ref[...], pltpu.roll(u, 1, 0)) - u
    prev_ref[...] = u[tm - 1:tm, :]
    mix = lambda p: u + delta * mu_ref[p:p + 1, :]

    r_out[...] = _dot(mix(0), win_ref[0]).astype(r_out.dtype)
    k = _dot(mix(1), win_ref[1])
    v_out[...] = _dot(mix(2), win_ref[2]).astype(v_out.dtype)
    g_out[...] = _dot(mix(3), win_ref[3]).astype(g_out.dtype)
    lora_w = _dot(jnp.tanh(_dot(mix(4), w1_ref[...])), w2_ref[...])
    lora_a = _dot(_dot(mix(5), a1_ref[...]), a2_ref[...])
    lw_out[...] = (-EXP_NEG_HALF) * _sigmoid(w0_ref[...] + lora_w)
    iclr = _sigmoid(a0_ref[...] + lora_a)
    k_out[...] = (k * (1.0 + (iclr - 1.0) * ka_ref[...])).astype(k_out.dtype)
    kk = k * kk_ref[...]
    kk = kk * lax.rsqrt(jnp.maximum(_seg_sum(kk * kk, ones_ref[...]), 1e-24))
    kk_out[...] = kk.astype(kk_out.dtype)
    b_out[...] = (kk * iclr).astype(b_out.dtype)


def _rwkv_front(x1, pre_gain, mu, win, w0, w1, w2, a0, a1, a2, k_k, k_a, ones_bd, tm=512):
    bsz, seq, d = x1.shape
    tok = lambda: pl.BlockSpec((None, tm, d), lambda b, s: (b, s, 0))
    full = lambda a: pl.BlockSpec(a.shape, lambda b, s: (0,) * a.ndim)
    args = (pre_gain, mu, win, w0, w1, w2, a0, a1, a2, k_k, k_a, ones_bd)
    return pl.pallas_call(
        _rwkv_front_kernel,
        grid=(bsz, seq // tm),
        in_specs=[tok()] + [full(a) for a in args],
        out_specs=[tok() for _ in range(7)],
        out_shape=[jax.ShapeDtypeStruct((bsz, seq, d), F32 if i == 1 else BF16) for i in range(7)],
        scratch_shapes=[pltpu.VMEM((1, d), F32)],
        compiler_params=_params("parallel", "arbitrary"),
        name="l1_front",
    )(x1, *args)


def _rwkv_scan_kernel(r_ref, lw_ref, k_ref, v_ref, kk_ref, b_ref, y_ref,
                      ht_ref, rt_s, y0_s, ml_s, ha_s, gm_s):
    @pl.when(pl.program_id(2) == 0)
    def _():
        ht_ref[...] = jnp.zeros_like(ht_ref)

    ts, width = r_ref.shape
    nchunk = ts // CHUNK
    npair = width // LANES
    lanes_of = [slice(p * LANES, (p + 1) * LANES) for p in range(npair)]
    two = 2 * CHUNK
    head0 = lax.broadcasted_iota(jnp.int32, (CHUNK, LANES), 1) < RW_HEAD
    row = lax.broadcasted_iota(jnp.int32, (two, two), 0)
    col = lax.broadcasted_iota(jnp.int32, (two, two), 1)
    strict = (row & (CHUNK - 1)) > (col & (CHUNK - 1))
    incl = (row & (CHUNK - 1)) >= (col & (CHUNK - 1))
    eye = jnp.where(row == col, 1.0, 0.0)

    def stack(x):
        return jnp.concatenate([jnp.where(head0, x, 0.0), jnp.where(head0, 0.0, x)], axis=0)

    def prep(c):
        rows = pl.ds(pl.multiple_of(c * CHUNK, CHUNK), CHUNK)
        r, lw, k, v, kk, b = (ref[rows, :].astype(F32)
                              for ref in (r_ref, lw_ref, k_ref, v_ref, kk_ref, b_ref))
        lcum = _cumsum_rows(lw)
        lend = lcum[CHUNK - 1:CHUNK, :]
        e_neg = jnp.exp(-lcum)
        e_end = jnp.exp(lend - lcum)
        gm_s[c] = jnp.exp(lend)
        at, rt = -kk * jnp.exp(lcum - lw), r * jnp.exp(lcum)
        bt, kt, bh, kh = b * e_neg, k * e_neg, b * e_end, k * e_end
        return [dict(rows=rows, c=c, p=p, sl=sl,
                     at2=stack(at[:, sl]).astype(BF16),
                     rt2=stack(rt[:, sl]),
                     bk2=jnp.concatenate([stack(bt[:, sl]), stack(kt[:, sl])], axis=0).astype(BF16),
                     v2=stack(v[:, sl]).astype(BF16),
                     bkh2=jnp.concatenate([stack(bh[:, sl]), stack(kh[:, sl])], axis=0).astype(BF16))
                for p, sl in enumerate(lanes_of)]

    def local_group(i, carry):
        g = [q for j in range(SCAN_GROUP) for q in prep(i * SCAN_GROUP + j)]
        s = [_dot_nt(jnp.concatenate([q["at2"], q["rt2"].astype(BF16)], axis=0), q["bk2"]) for q in g]
        n = [jnp.where(strict, x[:two, :two], 0.0) for x in s]
        a_ak = [jnp.where(strict, x[:two, two:], 0.0).astype(BF16) for x in s]
        a_rb = [jnp.where(incl, x[two:, :two], 0.0).astype(BF16) for x in s]
        a_rk = [jnp.where(incl, x[two:, two:], 0.0).astype(BF16) for x in s]
        t = [eye + x for x in n]
        pw = [_dot(x, x).astype(BF16) for x in n]
        d = 2
        while d < CHUNK // 2:
            m = [_dot(jnp.concatenate([a.astype(BF16), b], axis=0), b) for a, b in zip(t, pw)]
            t = [a + x[:two] for a, x in zip(t, m)]
            pw = [x[two:].astype(BF16) for x in m]
            d *= 2
        t = [a + _dot(a, b) for a, b in zip(t, pw)]
        akv = [_dot(x, q["v2"]) for x, q in zip(a_ak, g)]
        tw = [_dot(x, jnp.concatenate([q["at2"], y.astype(BF16)], axis=1)).astype(BF16)
              for x, y, q in zip(t, akv, g)]
        ry = [_dot(x, y) for x, y in zip(a_rb, tw)]
        rkv = [_dot(x, q["v2"]) for x, q in zip(a_rk, g)]
        ml = [_dot_tn(q["bkh2"][:two], y[:, :LANES]) for q, y in zip(g, tw)]
        ha = [_dot_tn(jnp.concatenate([y[:, LANES:], q["v2"]], axis=0), q["bkh2"]) for q, y in zip(g, tw)]
        for j, q in enumerate(g):
            rtp = q["rt2"] + ry[j][:, :LANES]
            y0 = ry[j][:, LANES:] + rkv[j]
            rt_s[q["rows"], q["sl"]] = (rtp[:CHUNK] + rtp[CHUNK:]).astype(BF16)
            y0_s[q["rows"], q["sl"]] = y0[:CHUNK] + y0[CHUNK:]
            ml_s[q["c"], q["p"]] = ml[j].astype(BF16)
            ha_s[q["c"], q["p"]] = ha[j]
        return carry

    lax.fori_loop(0, nchunk // SCAN_GROUP, local_group, 0)

    def step(c, hts):
        rows = pl.ds(pl.multiple_of(c * CHUNK, CHUNK), CHUNK)
        rt, gm = rt_s[rows, :], gm_s[c]
        htb = [h.astype(BF16) for h in hts]
        ys = [_dot_nt(rt[:, sl], hb) for sl, hb in zip(lanes_of, htb)]
        new = [h * gm[:, sl] + _dot_nt(hb, ml_s[c, p]) + ha_s[c, p]
               for p, (sl, h, hb) in enumerate(zip(lanes_of, hts, htb))]
        y_ref[rows, :] = (jnp.concatenate(ys, axis=1) + y0_s[rows, :]).astype(y_ref.dtype)
        return tuple(new)

    hts = lax.fori_loop(0, nchunk, step, tuple(ht_ref[p] for p in range(npair)))
    for p in range(npair):
        ht_ref[p] = hts[p]


def _rwkv_scan(r, lw, k, v, kk, b, ts=512, width=512):
    bsz, seq, d = r.shape
    nchunk = ts // CHUNK
    npair = width // LANES
    spec = lambda: pl.BlockSpec((None, ts, width), lambda bb, h, s: (bb, s, h))
    return pl.pallas_call(
        _rwkv_scan_kernel,
        grid=(bsz, d // width, seq // ts),
        in_specs=[spec() for _ in range(6)],
        out_specs=spec(),
        out_shape=jax.ShapeDtypeStruct((bsz, seq, d), BF16),
        scratch_shapes=[pltpu.VMEM((npair, LANES, LANES), F32),
                        pltpu.VMEM((ts, width), BF16),
                        pltpu.VMEM((ts, width), F32),
                        pltpu.VMEM((nchunk, npair, LANES, LANES), BF16),
                        pltpu.VMEM((nchunk, npair, LANES, LANES), F32),
                        pltpu.VMEM((nchunk, 1, width), F32)],
        compiler_params=_params("parallel", "parallel", "arbitrary"),
        name="l1_scan",
    )(r, lw, k, v, kk, b)


def _rwkv_back_kernel(y_ref, r_ref, k_ref, v_ref, g_ref, x_ref, rk_ref, lnw_ref, lnb_ref,
                      w_ref, pn_ref, ones_ref, o_ref):
    ones = ones_ref[...]
    inv_n = 1.0 / RW_HEAD
    y = y_ref[...].astype(F32)
    yc = y - _seg_sum(y, ones, split=False) * inv_n
    var = _seg_sum(yc * yc, ones, split=False) * inv_n
    yn = yc * lax.rsqrt(var + RW_GN_EPS) * lnw_ref[...] + lnb_ref[...]
    rk = r_ref[...].astype(F32) * k_ref[...].astype(F32) * rk_ref[...]
    yn = yn + _seg_sum(rk, ones, split=False) * v_ref[...].astype(F32)
    out = _dot(yn * _silu(g_ref[...].astype(F32)), w_ref[...])
    o_ref[...] = x_ref[...] + _rms(out, pn_ref[...])


def _rwkv_back(y, r, k, v, g, x1, r_k, ln_w, ln_b, w_bf16, post_gain, ones_bd, tm=512):
    t, d = y.shape
    tok = lambda: pl.BlockSpec((tm, d), lambda i: (i, 0))
    full = lambda a: pl.BlockSpec(a.shape, lambda i: (0,) * a.ndim)
    params = (r_k, ln_w, ln_b, w_bf16, post_gain, ones_bd)
    return pl.pallas_call(
        _rwkv_back_kernel,
        grid=(t // tm,),
        in_specs=[tok() for _ in range(6)] + [full(a) for a in params],
        out_specs=tok(),
        out_shape=jax.ShapeDtypeStruct((t, d), F32),
        compiler_params=_params("parallel"),
        name="l1_back",
    )(y, r, k, v, g, x1, *params)


def _block_diag(w, group):
    n, bi, bj = w.shape
    w = w.reshape(n // group, group, bi, bj)
    eye = jnp.eye(group, dtype=w.dtype)
    return jnp.einsum("gaij,ab->gaibj", w, eye).reshape(n // group, group * bi, group * bj)


def kernel(x, pre_norm, post_norm, ab_w_in, ab_w_out, rg_conv_w, rg_conv_b, rg_w_a, rg_b_a,
           rg_w_x, rg_b_x, rg_lambda, hg_lower_bounds, hg_out_norm, rw_mu, rw_w_in, rw_w_out,
           rw_w0, rw_w1, rw_w2, rw_a0, rw_a1, rw_a2, rw_k_k, rw_k_a, rw_r_k, rw_ln_w, rw_ln_b):
    bsz, seq, d = x.shape
    t = bsz * seq
    row = lambda a: a.reshape(1, -1).astype(F32)
    x2d = x.reshape(t, d)
    heads_per_tile = MXU_DIM // RW_HEAD
    ones_bd = _block_diag(jnp.ones((heads_per_tile, RW_HEAD, RW_HEAD), F32), heads_per_tile)[0].astype(BF16)

    proj = _norm_proj(x2d, row(pre_norm[0]), ab_w_in[0].astype(BF16))
    proj4 = proj.reshape(proj.shape[0], bsz, seq, d)
    rg_group = MXU_DIM // rg_w_a.shape[-1]
    hg = _rglru(proj4, rg_conv_w[0].astype(F32), row(rg_conv_b[0]),
                _block_diag(rg_w_a[0], rg_group).astype(BF16), row(rg_b_a[0]),
                _block_diag(rg_w_x[0], rg_group).astype(BF16), row(rg_b_x[0]), row(rg_lambda[0]))
    og = _hgrn2(proj4, hg_lower_bounds.astype(F32), row(hg_out_norm[0]))
    x1 = _out_proj0(hg.reshape(t, d), og.reshape(t, d), ab_w_out[0].astype(BF16),
                    row(post_norm[0]), x2d)

    r, lw, k, v, kk, b, g = _rwkv_front(
        x1.reshape(bsz, seq, d), row(pre_norm[1]), rw_mu[0].astype(F32), rw_w_in[0].astype(BF16),
        row(rw_w0[0]), rw_w1[0].astype(BF16), rw_w2[0].astype(BF16), row(rw_a0[0]),
        rw_a1[0].astype(BF16), rw_a2[0].astype(BF16), row(rw_k_k[0]), row(rw_k_a[0]), ones_bd)
    y = _rwkv_scan(r, lw, k, v, kk, b)
    flat = lambda a: a.reshape(t, d)
    x2 = _rwkv_back(flat(y), flat(r), flat(k), flat(v), flat(g), x1, row(rw_r_k[0]),
                    row(rw_ln_w[0]), row(rw_ln_b[0]), rw_w_out[0].astype(BF16),
                    row(post_norm[1]), ones_bd)
    return x2.reshape(bsz, seq, d)
```

```python
import functools

import jax
import jax.numpy as jnp
from jax import lax
from jax.experimental import pallas as pl
from jax.experimental.pallas import tpu as pltpu

F32 = jnp.float32
BF16 = jnp.bfloat16

D_MODEL = 1024
RMS_EPS = 1e-6
RG_C = 8.0
RG_CONV = 4
HG_HEADS = 8
HG_DK = 128
RW_HEAD = 64
RW_GN_EPS = 64e-5

LANES = 128
SUBLANES = 8
MXU_DIM = 256
VMEM_LIMIT = 56 * 1024 * 1024

CHUNK = 64
SUB = 8
SCAN_GROUP = 4
HG_GROUP = 8
SCAN_STAGES = 8
EXP_NEG_HALF = 0.6065306597126334


def _params(*sem):
    return pltpu.CompilerParams(dimension_semantics=sem, vmem_limit_bytes=VMEM_LIMIT)


def _dot(a, b):
    return jnp.dot(a.astype(BF16), b.astype(BF16), preferred_element_type=F32)


def _dot_nt(a, b):
    return lax.dot_general(a.astype(BF16), b.astype(BF16), (((1,), (1,)), ((), ())),
                           preferred_element_type=F32)


def _dot_tn(a, b):
    return lax.dot_general(a.astype(BF16), b.astype(BF16), (((0,), (0,)), ((), ())),
                           preferred_element_type=F32)


def _sigmoid(x):
    return 1.0 / (1.0 + jnp.exp(-x))


def _silu(x):
    return x * _sigmoid(x)


def _softplus(z):
    return jnp.maximum(z, 0.0) + jnp.log(1.0 + jnp.exp(-jnp.abs(z)))


def _rms(x, gain):
    return x * lax.rsqrt(jnp.mean(x * x, axis=-1, keepdims=True) + RMS_EPS) * gain


def _cumsum_rows(x):
    n = x.shape[0]
    row = lax.broadcasted_iota(jnp.int32, x.shape, 0)
    d = 1
    while d < n:
        x = x + jnp.where(row >= d, pltpu.roll(x, d, 0), 0.0)
        d *= 2
    return x


def _seg_sum(x, ones_bd, split=True):
    hi = x.astype(BF16)
    lo = (x - hi.astype(F32)).astype(BF16) if split else None
    outs = []
    for g in range(x.shape[1] // MXU_DIM):
        sl = slice(g * MXU_DIM, (g + 1) * MXU_DIM)
        acc = jnp.dot(hi[:, sl], ones_bd, preferred_element_type=F32)
        if split:
            acc = acc + jnp.dot(lo[:, sl], ones_bd, preferred_element_type=F32)
        outs.append(acc)
    return jnp.concatenate(outs, axis=-1)


def _norm_proj_kernel(x_ref, g_ref, w_ref, o_ref, u_ref):
    @pl.when(pl.program_id(1) == 0)
    def _():
        u_ref[...] = _rms(x_ref[...], g_ref[...]).astype(BF16)

    o_ref[...] = jnp.dot(u_ref[...], w_ref[...], preferred_element_type=F32).astype(o_ref.dtype)


def _norm_proj(x2d, gain, w_bf16, tm=1024, tn=2048):
    t, d = x2d.shape
    n = w_bf16.shape[1]
    return pl.pallas_call(
        _norm_proj_kernel,
        grid=(t // tm, n // tn),
        in_specs=[pl.BlockSpec((tm, d), lambda i, j: (i, 0)),
                  pl.BlockSpec((1, d), lambda i, j: (0, 0)),
                  pl.BlockSpec((d, tn), lambda i, j: (0, j))],
        out_specs=pl.BlockSpec((None, tm, tn), lambda i, j: (j, i, 0)),
        out_shape=jax.ShapeDtypeStruct((n // tn, t, tn), BF16),
        scratch_shapes=[pltpu.VMEM((tm, d), BF16)],
        compiler_params=_params("parallel", "arbitrary"),
        name="l0_norm_proj",
    )(x2d, gain, w_bf16)


def _rglru_kernel(x_ref, g_ref, cw_ref, cb_ref, wa_ref, ba_ref, wx_ref, bx_ref, lam_ref,
                  o_ref, h_ref, ext_ref, a_ref, b_ref):
    @pl.when(pl.program_id(1) == 0)
    def _():
        h_ref[...] = jnp.zeros_like(h_ref)
        ext_ref[:SUBLANES, :] = jnp.zeros((SUBLANES, ext_ref.shape[1]), F32)

    x = x_ref[...].astype(F32)
    ts, width = x.shape
    ext_ref[SUBLANES:, :] = x
    xc = cb_ref[...] + cw_ref[RG_CONV - 1:RG_CONV, :] * x
    for k in range(1, RG_CONV):
        xc = xc + cw_ref[RG_CONV - 1 - k:RG_CONV - k, :] * ext_ref[SUBLANES - k:SUBLANES - k + ts, :]
    ext_ref[:SUBLANES, :] = x[ts - SUBLANES:, :]

    xcb = xc.astype(BF16)
    pre_r, pre_i = [], []
    for g in range(width // MXU_DIM):
        sl = slice(g * MXU_DIM, (g + 1) * MXU_DIM)
        pre_r.append(jnp.dot(xcb[:, sl], wa_ref[g], preferred_element_type=F32))
        pre_i.append(jnp.dot(xcb[:, sl], wx_ref[g], preferred_element_type=F32))
    gate_r = _sigmoid(jnp.concatenate(pre_r, axis=-1) + ba_ref[...])
    gate_i = _sigmoid(jnp.concatenate(pre_i, axis=-1) + bx_ref[...])

    log_a = (-RG_C) * gate_r * _softplus(-lam_ref[...])
    a = jnp.exp(log_a)
    b = jnp.sqrt(1.0 - a * a) * (gate_i * xc)

    a = a.reshape(ts // SUBLANES, SUBLANES, width)
    b = b.reshape(ts // SUBLANES, SUBLANES, width)
    in_group = lax.broadcasted_iota(jnp.int32, (1, SUBLANES, 1), 1)
    d = 1
    while d < SUBLANES:
        m = in_group >= d
        b = jnp.where(m, a, 0.0) * pltpu.roll(b, d, 1) + b
        a = a * jnp.where(m, pltpu.roll(a, d, 1), 1.0)
        d *= 2
    a_ref[...] = a.reshape(ts, width)
    b_ref[...] = b.reshape(ts, width)

    def carry_group(i, h):
        rows = pl.ds(pl.multiple_of(i * SUBLANES, SUBLANES), SUBLANES)
        hg = a_ref[rows, :] * h + b_ref[rows, :]
        b_ref[rows, :] = hg
        return hg[SUBLANES - 1:, :]

    h_ref[...] = lax.fori_loop(0, ts // SUBLANES, carry_group, h_ref[...], unroll=8)
    o_ref[...] = (b_ref[...] * _silu(g_ref[...].astype(F32))).astype(o_ref.dtype)


def _proj_block(proj4, split, block, offset=0):
    per_slab = proj4.shape[-1] // block
    col = split * (D_MODEL // block) + offset
    return col // per_slab, col % per_slab


def _rglru(proj4, conv_w, conv_b, wa_bd, b_a, wx_bd, b_x, lam, ts=256):
    _, bsz, seq, _ = proj4.shape
    width = D_MODEL
    row = lambda: pl.BlockSpec((1, width), lambda b, s: (0, 0))

    def split_spec(split):
        slab, blk = _proj_block(proj4, split, width)
        return pl.BlockSpec((None, None, ts, width), lambda b, s: (slab, b, s, blk))

    return pl.pallas_call(
        _rglru_kernel,
        grid=(bsz, seq // ts),
        in_specs=[split_spec(0), split_spec(1),
                  pl.BlockSpec((RG_CONV, width), lambda b, s: (0, 0)),
                  row(),
                  pl.BlockSpec(wa_bd.shape, lambda b, s: (0, 0, 0)),
                  row(),
                  pl.BlockSpec(wx_bd.shape, lambda b, s: (0, 0, 0)),
                  row(), row()],
        out_specs=pl.BlockSpec((None, ts, width), lambda b, s: (b, s, 0)),
        out_shape=jax.ShapeDtypeStruct((bsz, seq, width), BF16),
        scratch_shapes=[pltpu.VMEM((1, width), F32), pltpu.VMEM((SUBLANES + ts, width), F32),
                        pltpu.VMEM((ts, width), F32), pltpu.VMEM((ts, width), F32)],
        compiler_params=_params("parallel", "arbitrary"),
        name="l0_rglru",
    )(proj4, proj4, conv_w, conv_b, wa_bd, b_a, wx_bd, b_x, lam)


def _hgrn2_kernel(q_ref, f_ref, v_ref, g_ref, lbp_ref, gn_ref, o_ref, st_ref):
    @pl.when(pl.program_id(2) == 0)
    def _():
        st_ref[...] = jnp.zeros_like(st_ref)

    lbp = lbp_ref[...]
    e = jnp.exp(lbp - jnp.max(lbp, axis=0, keepdims=True))
    p = e / jnp.sum(e, axis=0, keepdims=True)
    lb = (p[0:1] + p[1:2]) - p[0:1]
    gn = gn_ref[...]
    ts = q_ref.shape[0]
    srow = lax.broadcasted_iota(jnp.int32, (CHUNK, CHUNK), 0)
    scol = lax.broadcasted_iota(jnp.int32, (CHUNK, CHUNK), 1)
    diag_lane0 = srow & ~(SUB - 1)
    causal = scol <= srow
    levels = []
    h = CHUNK // 2
    while h >= SUB:
        levels.append((h, (srow & ~(2 * h - 1)) == (scol & ~(2 * h - 1))))
        h //= 2
    tril3 = jnp.where(srow >= scol, 1.0, 0.0).astype(BF16)
    tril3 = jnp.concatenate([tril3, tril3, tril3], axis=1)

    def cumsum_rows(x):
        hi = x.astype(BF16)
        r1 = x - hi.astype(F32)
        mid = r1.astype(BF16)
        lo = (r1 - mid.astype(F32)).astype(BF16)
        return jnp.dot(tril3, jnp.concatenate([hi, mid, lo], axis=0), preferred_element_type=F32)

    def block_rows(x, size, r):
        return jnp.concatenate([jnp.broadcast_to(x[lo + r:lo + r + 1], (size, LANES))
                                for lo in range(0, CHUNK, size)], axis=0)

    def level_factors(q, k, g2, h):
        zero = jnp.zeros((h, LANES), F32)
        qt, kt = [], []
        for lo in range(0, CHUNK, 2 * h):
            mid = lo + h
            rho = g2[mid - 1:mid, :]
            kt += [k[lo:mid] * jnp.exp2(rho - g2[lo:mid]), zero]
            qt += [zero, q[mid:mid + h] * jnp.exp2(g2[mid:mid + h] - rho)]
        return jnp.concatenate(qt, axis=0), jnp.concatenate(kt, axis=0)

    def group(i, carry):
        idx = range(HG_GROUP)
        rows = [pl.ds(pl.multiple_of((i * HG_GROUP + j) * CHUNK, CHUNK), CHUNK) for j in idx]
        q = [_silu(q_ref[r, :].astype(F32)) for r in rows]
        sig = [_sigmoid(f_ref[r, :].astype(F32)) for r in rows]
        k = [(1.0 - lb) * (1.0 - x) for x in sig]
        v = [v_ref[r, :].astype(BF16) for r in rows]
        g2 = [cumsum_rows(jnp.log2(lb + (1.0 - lb) * x)) for x in sig]
        gend = [x[CHUNK - 1:CHUNK, :] for x in g2]
        qg = [(a * jnp.exp2(b)).astype(BF16) for a, b in zip(q, g2)]
        vtk = [_dot_tn(a, b * jnp.exp2(e - c)) for a, b, c, e in zip(v, k, g2, gend)]

        scores = [None for _ in idx]
        for h, same_block in levels:
            for j in idx:
                qt, kt = level_factors(q[j], k[j], g2[j], h)
                s = jnp.where(same_block, _dot_nt(qt, kt), 0.0)
                scores[j] = s if scores[j] is None else scores[j] + s

        gk = [a - jnp.log2(b) for a, b in zip(g2, k)]
        sc = [jnp.zeros((CHUNK, CHUNK), F32) for _ in idx]
        for s in range(SUB):
            for j in idx:
                w = jnp.exp2(g2[j] - block_rows(gk[j], SUB, s))
                col = jnp.sum(q[j] * w, axis=-1, keepdims=True)
                sc[j] = jnp.where(scol == diag_lane0 + s, col, sc[j])
        intra = [_dot(a + jnp.where(causal, b, 0.0), c) for a, b, c in zip(scores, sc, v)]

        st = st_ref[...]
        for j in idx:
            o = _dot_nt(qg[j], st) + intra[j]
            st = st * jnp.exp2(gend[j]) + vtk[j]
            o = _rms(o, gn) * _silu(g_ref[rows[j], :].astype(F32))
            o_ref[rows[j], :] = o.astype(o_ref.dtype)
        st_ref[...] = st
        return carry

    lax.fori_loop(0, ts // (CHUNK * HG_GROUP), group, 0)


def _hgrn2(proj4, lb_params, g_norm, ts=1024):
    _, bsz, seq, _ = proj4.shape
    width = D_MODEL

    def spec(split):
        slab, blk = _proj_block(proj4, split, HG_DK)
        return pl.BlockSpec((None, None, ts, HG_DK), lambda b, h, s: (slab, b, s, blk + h))

    return pl.pallas_call(
        _hgrn2_kernel,
        grid=(bsz, HG_HEADS, seq // ts),
        in_specs=[spec(2), spec(3), spec(4), spec(5),
                  pl.BlockSpec((lb_params.shape[0], HG_DK), lambda b, h, s: (0, h)),
                  pl.BlockSpec((1, HG_DK), lambda b, h, s: (0, 0))],
        out_specs=pl.BlockSpec((None, ts, HG_DK), lambda b, h, s: (b, s, h)),
        out_shape=jax.ShapeDtypeStruct((bsz, seq, width), BF16),
        scratch_shapes=[pltpu.VMEM((HG_DK, HG_DK), F32)],
        compiler_params=_params("parallel", "parallel", "arbitrary"),
        name="l0_hgrn2",
    )(proj4, proj4, proj4, proj4, lb_params, g_norm)


def _out_proj0_kernel(hg_ref, og_ref, w_ref, pn_ref, x_ref, o_ref):
    half = hg_ref.shape[1]
    y = (jnp.dot(hg_ref[...], w_ref[:half, :], preferred_element_type=F32)
         + jnp.dot(og_ref[...], w_ref[half:, :], preferred_element_type=F32))
    o_ref[...] = x_ref[...] + _rms(y, pn_ref[...])


def _out_proj0(hg, og, w_bf16, post_gain, x2d, tm=512):
    t, d = x2d.shape
    tok = lambda: pl.BlockSpec((tm, d), lambda i: (i, 0))
    return pl.pallas_call(
        _out_proj0_kernel,
        grid=(t // tm,),
        in_specs=[tok(), tok(),
                  pl.BlockSpec(w_bf16.shape, lambda i: (0, 0)),
                  pl.BlockSpec((1, d), lambda i: (0, 0)),
                  tok()],
        out_specs=tok(),
        out_shape=jax.ShapeDtypeStruct((t, d), F32),
        compiler_params=_params("parallel"),
        name="l0_out_proj",
    )(hg, og, w_bf16, post_gain, x2d)


def _rwkv_front_kernel(x_ref, pn_ref, mu_ref, win_ref, w0_ref, w1_ref, w2_ref, a0_ref, a1_ref,
                       a2_ref, kk_ref, ka_ref, ones_ref,
                       r_out, lw_out, k_out, v_out, kk_out, b_out, g_out, prev_ref):
    @pl.when(pl.program_id(1) == 0)
    def _():
        prev_ref[...] = jnp.zeros_like(prev_ref)

    u = _rms(x_ref[...], pn_ref[...])
    tm = u.shape[0]
    row = lax.broadcasted_iota(jnp.int32, (tm, 1), 0)
    delta = jnp.where(row == 0, prev_ref[...], pltpu.roll(u, 1, 0)) - u
    prev_ref[...] = u[tm - 1:tm, :]
    mix = lambda p: u + delta * mu_ref[p:p + 1, :]

    r_out[...] = _dot(mix(0), win_ref[0]).astype(r_out.dtype)
    k = _dot(mix(1), win_ref[1])
    v_out[...] = _dot(mix(2), win_ref[2]).astype(v_out.dtype)
    g_out[...] = _dot(mix(3), win_ref[3]).astype(g_out.dtype)
    lora_w = _dot(jnp.tanh(_dot(mix(4), w1_ref[...])), w2_ref[...])
    lora_a = _dot(_dot(mix(5), a1_ref[...]), a2_ref[...])
    lw_out[...] = (-EXP_NEG_HALF) * _sigmoid(w0_ref[...] + lora_w)
    iclr = _sigmoid(a0_ref[...] + lora_a)
    k_out[...] = (k * (1.0 + (iclr - 1.0) * ka_ref[...])).astype(k_out.dtype)
    kk = k * kk_ref[...]
    kk = kk * lax.rsqrt(jnp.maximum(_seg_sum(kk * kk, ones_ref[...]), 1e-24))
    kk_out[...] = kk.astype(kk_out.dtype)
    b_out[...] = (kk * iclr).astype(b_out.dtype)


def _rwkv_front(x1, pre_gain, mu, win, w0, w1, w2, a0, a1, a2, k_k, k_a, ones_bd, tm=512):
    bsz, seq, d = x1.shape
    tok = lambda: pl.BlockSpec((None, tm, d), lambda b, s: (b, s, 0))
    full = lambda a: pl.BlockSpec(a.shape, lambda b, s: (0,) * a.ndim)
    args = (pre_gain, mu, win, w0, w1, w2, a0, a1, a2, k_k, k_a, ones_bd)
    return pl.pallas_call(
        _rwkv_front_kernel,
        grid=(bsz, seq // tm),
        in_specs=[tok()] + [full(a) for a in args],
        out_specs=[tok() for _ in range(7)],
        out_shape=[jax.ShapeDtypeStruct((bsz, seq, d), F32 if i == 1 else BF16) for i in range(7)],
        scratch_shapes=[pltpu.VMEM((1, d), F32)],
        compiler_params=_params("parallel", "arbitrary"),
        name="l1_front",
    )(x1, *args)


def _rwkv_scan_kernel(r_ref, lw_ref, k_ref, v_ref, kk_ref, b_ref, y_ref,
                      ht_ref, rt_s, y0_s, ml_s, ha_s, gm_s):
    @pl.when(pl.program_id(2) == 0)
    def _():
        ht_ref[...] = jnp.zeros_like(ht_ref)

    ts, width = r_ref.shape
    nchunk = ts // CHUNK
    npair = width // LANES
    lanes_of = [slice(p * LANES, (p + 1) * LANES) for p in range(npair)]
    two = 2 * CHUNK
    head0 = lax.broadcasted_iota(jnp.int32, (CHUNK, LANES), 1) < RW_HEAD
    row = lax.broadcasted_iota(jnp.int32, (two, two), 0)
    col = lax.broadcasted_iota(jnp.int32, (two, two), 1)
    strict = (row & (CHUNK - 1)) > (col & (CHUNK - 1))
    incl = (row & (CHUNK - 1)) >= (col & (CHUNK - 1))
    eye = jnp.where(row == col, 1.0, 0.0)

    def stack(x):
        return jnp.concatenate([jnp.where(head0, x, 0.0), jnp.where(head0, 0.0, x)], axis=0)

    def prep(c):
        rows = pl.ds(pl.multiple_of(c * CHUNK, CHUNK), CHUNK)
        r, lw, k, v, kk, b = (ref[rows, :].astype(F32)
                              for ref in (r_ref, lw_ref, k_ref, v_ref, kk_ref, b_ref))
        lcum = _cumsum_rows(lw)
        lend = lcum[CHUNK - 1:CHUNK, :]
        e_neg = jnp.exp(-lcum)
        e_end = jnp.exp(lend - lcum)
        gm_s[c] = jnp.exp(lend)
        at, rt = -kk * jnp.exp(lcum - lw), r * jnp.exp(lcum)
        bt, kt, bh, kh = b * e_neg, k * e_neg, b * e_end, k * e_end
        return [dict(rows=rows, c=c, p=p, sl=sl,
                     at2=stack(at[:, sl]).astype(BF16),
                     rt2=stack(rt[:, sl]),
                     bk2=jnp.concatenate([stack(bt[:, sl]), stack(kt[:, sl])], axis=0).astype(BF16),
                     v2=stack(v[:, sl]).astype(BF16),
                     bkh2=jnp.concatenate([stack(bh[:, sl]), stack(kh[:, sl])], axis=0).astype(BF16))
                for p, sl in enumerate(lanes_of)]

    def local_group(i, between=()):
        pending = list(between)
        passed = [0]

        def boundary():
            fired = len(between) - len(pending)
            if pending and passed[0] * len(between) >= fired * SCAN_STAGES:
                pending.pop(0)()
            passed[0] += 1

        g = [q for j in range(SCAN_GROUP) for q in prep(i * SCAN_GROUP + j)]
        s = [_dot_nt(jnp.concatenate([q["at2"], q["rt2"].astype(BF16)], axis=0), q["bk2"]) for q in g]
        boundary()
        n = [jnp.where(strict, x[:two, :two], 0.0) for x in s]
        a_ak = [jnp.where(strict, x[:two, two:], 0.0).astype(BF16) for x in s]
        a_rbk = [jnp.concatenate([jnp.where(incl, x[two:, :two], 0.0),
                                  jnp.where(incl, x[two:, two:], 0.0)], axis=1).astype(BF16) for x in s]
        t = [eye + x for x in n]
        pw = [_dot(x, x).astype(BF16) for x in n]
        boundary()
        d = 2
        while d < CHUNK // 2:
            m = [_dot(jnp.concatenate([a.astype(BF16), b], axis=0), b) for a, b in zip(t, pw)]
            t = [a + x[:two] for a, x in zip(t, m)]
            pw = [x[two:].astype(BF16) for x in m]
            boundary()
            d *= 2
        t = [a + _dot(a, b) for a, b in zip(t, pw)]
        boundary()
        akv = [_dot(x, q["v2"]) for x, q in zip(a_ak, g)]
        tw = [_dot(x, jnp.concatenate([q["at2"], y.astype(BF16)], axis=1)).astype(BF16)
              for x, y, q in zip(t, akv, g)]
        boundary()
        rw = [_dot(x[:, :two], y[:, :LANES]) for x, y in zip(a_rbk, tw)]
        y0 = [_dot(x, jnp.concatenate([y[:, LANES:], q["v2"]], axis=0)) for x, y, q in zip(a_rbk, tw, g)]
        ml = [_dot_tn(q["bkh2"][:two], y[:, :LANES]) for q, y in zip(g, tw)]
        ha = [_dot_tn(jnp.concatenate([y[:, LANES:], q["v2"]], axis=0), q["bkh2"]) for q, y in zip(g, tw)]
        while pending:
            pending.pop(0)()
        for j, q in enumerate(g):
            rtp = q["rt2"] + rw[j]
            rt_s[q["rows"], q["sl"]] = (rtp[:CHUNK] + rtp[CHUNK:]).astype(BF16)
            y0_s[q["rows"], q["sl"]] = y0[j][:CHUNK] + y0[j][CHUNK:]
            ml_s[q["c"], q["p"]] = ml[j].astype(BF16)
            ha_s[q["c"], q["p"]] = ha[j]

    def state_step(c, hts):
        rows = pl.ds(pl.multiple_of(c * CHUNK, CHUNK), CHUNK)
        rt, gm = rt_s[rows, :], gm_s[c]
        htb = [h.astype(BF16) for h in hts]
        ys = [_dot_nt(rt[:, sl], hb) for sl, hb in zip(lanes_of, htb)]
        new = [h * gm[:, sl] + _dot_nt(hb, ml_s[c, p]) + ha_s[c, p]
               for p, (sl, h, hb) in enumerate(zip(lanes_of, hts, htb))]
        y_ref[rows, :] = (jnp.concatenate(ys, axis=1) + y0_s[rows, :]).astype(y_ref.dtype)
        return tuple(new)

    def pipelined(i, hts):
        box = [hts]

        def advance(j):
            def run():
                box[0] = state_step((i - 1) * SCAN_GROUP + j, box[0])
            return run

        local_group(i, between=[advance(j) for j in range(SCAN_GROUP)])
        return box[0]

    ngroup = nchunk // SCAN_GROUP
    local_group(0)
    hts = lax.fori_loop(1, ngroup, pipelined, tuple(ht_ref[p] for p in range(npair)))
    for j in range(SCAN_GROUP):
        hts = state_step((ngroup - 1) * SCAN_GROUP + j, hts)
    for p in range(npair):
        ht_ref[p] = hts[p]


def _rwkv_scan(r, lw, k, v, kk, b, ts=1024, width=512):
    bsz, seq, d = r.shape
    nchunk = ts // CHUNK
    npair = width // LANES
    spec = lambda: pl.BlockSpec((None, ts, width), lambda bb, h, s: (bb, s, h))
    return pl.pallas_call(
        _rwkv_scan_kernel,
        grid=(bsz, d // width, seq // ts),
        in_specs=[spec() for _ in range(6)],
        out_specs=spec(),
        out_shape=jax.ShapeDtypeStruct((bsz, seq, d), BF16),
        scratch_shapes=[pltpu.VMEM((npair, LANES, LANES), F32),
                        pltpu.VMEM((ts, width), BF16),
                        pltpu.VMEM((ts, width), F32),
                        pltpu.VMEM((nchunk, npair, LANES, LANES), BF16),
                        pltpu.VMEM((nchunk, npair, LANES, LANES), F32),
                        pltpu.VMEM((nchunk, 1, width), F32)],
        compiler_params=_params("parallel", "parallel", "arbitrary"),
        name="l1_scan",
    )(r, lw, k, v, kk, b)


def _rwkv_back_kernel(y_ref, r_ref, k_ref, v_ref, g_ref, x_ref, rk_ref, lnw_ref, lnb_ref,
                      w_ref, pn_ref, ones_ref, o_ref):
    ones = ones_ref[...]
    inv_n = 1.0 / RW_HEAD
    y = y_ref[...].astype(F32)
    yc = y - _seg_sum(y, ones, split=False) * inv_n
    var = _seg_sum(yc * yc, ones, split=False) * inv_n
    yn = yc * lax.rsqrt(var + RW_GN_EPS) * lnw_ref[...] + lnb_ref[...]
    rk = r_ref[...].astype(F32) * k_ref[...].astype(F32) * rk_ref[...]
    yn = yn + _seg_sum(rk, ones, split=False) * v_ref[...].astype(F32)
    out = _dot(yn * _silu(g_ref[...].astype(F32)), w_ref[...])
    o_ref[...] = x_ref[...] + _rms(out, pn_ref[...])


def _rwkv_back(y, r, k, v, g, x1, r_k, ln_w, ln_b, w_bf16, post_gain, ones_bd, tm=512):
    t, d = y.shape
    tok = lambda: pl.BlockSpec((tm, d), lambda i: (i, 0))
    full = lambda a: pl.BlockSpec(a.shape, lambda i: (0,) * a.ndim)
    params = (r_k, ln_w, ln_b, w_bf16, post_gain, ones_bd)
    return pl.pallas_call(
        _rwkv_back_kernel,
        grid=(t // tm,),
        in_specs=[tok() for _ in range(6)] + [full(a) for a in params],
        out_specs=tok(),
        out_shape=jax.ShapeDtypeStruct((t, d), F32),
        compiler_params=_params("parallel"),
        name="l1_back",
    )(y, r, k, v, g, x1, *params)


def _block_diag(w, group):
    n, bi, bj = w.shape
    w = w.reshape(n // group, group, bi, bj)
    eye = jnp.eye(group, dtype=w.dtype)
    return jnp.einsum("gaij,ab->gaibj", w, eye).reshape(n // group, group * bi, group * bj)


def kernel(x, pre_norm, post_norm, ab_w_in, ab_w_out, rg_conv_w, rg_conv_b, rg_w_a, rg_b_a,
           rg_w_x, rg_b_x, rg_lambda, hg_lower_bounds, hg_out_norm, rw_mu, rw_w_in, rw_w_out,
           rw_w0, rw_w1, rw_w2, rw_a0, rw_a1, rw_a2, rw_k_k, rw_k_a, rw_r_k, rw_ln_w, rw_ln_b):
    bsz, seq, d = x.shape
    t = bsz * seq
    row = lambda a: a.reshape(1, -1).astype(F32)
    x2d = x.reshape(t, d)
    heads_per_tile = MXU_DIM // RW_HEAD
    ones_bd = _block_diag(jnp.ones((heads_per_tile, RW_HEAD, RW_HEAD), F32), heads_per_tile)[0].astype(BF16)

    proj = _norm_proj(x2d, row(pre_norm[0]), ab_w_in[0].astype(BF16))
    proj4 = proj.reshape(proj.shape[0], bsz, seq, proj.shape[-1])
    rg_group = MXU_DIM // rg_w_a.shape[-1]
    hg = _rglru(proj4, rg_conv_w[0].astype(F32), row(rg_conv_b[0]),
                _block_diag(rg_w_a[0], rg_group).astype(BF16), row(rg_b_a[0]),
                _block_diag(rg_w_x[0], rg_group).astype(BF16), row(rg_b_x[0]), row(rg_lambda[0]))
    og = _hgrn2(proj4, hg_lower_bounds.astype(F32), row(hg_out_norm[0]))
    x1 = _out_proj0(hg.reshape(t, d), og.reshape(t, d), ab_w_out[0].astype(BF16),
                    row(post_norm[0]), x2d)

    r, lw, k, v, kk, b, g = _rwkv_front(
        x1.reshape(bsz, seq, d), row(pre_norm[1]), rw_mu[0].astype(F32), rw_w_in[0].astype(BF16),
        row(rw_w0[0]), rw_w1[0].astype(BF16), rw_w2[0].astype(BF16), row(rw_a0[0]),
        rw_a1[0].astype(BF16), rw_a2[0].astype(BF16), row(rw_k_k[0]), row(rw_k_a[0]), ones_bd)
    y = _rwkv_scan(r, lw, k, v, kk, b)
    flat = lambda a: a.reshape(t, d)
    x2 = _rwkv_back(flat(y), flat(r), flat(k), flat(v), flat(g), x1, row(rw_r_k[0]),
                    row(rw_ln_w[0]), row(rw_ln_b[0]), rw_w_out[0].astype(BF16),
                    row(post_norm[1]), ones_bd)
    return x2.reshape(bsz, seq, d)
```

```python
import functools

import jax
import jax.numpy as jnp
from jax import lax
from jax.experimental import pallas as pl
from jax.experimental.pallas import tpu as pltpu

F32 = jnp.float32
BF16 = jnp.bfloat16

D_MODEL = 1024
RMS_EPS = 1e-6
RG_C = 8.0
RG_CONV = 4
HG_HEADS = 8
HG_DK = 128
RW_HEAD = 64
RW_GN_EPS = 64e-5

LANES = 128
SUBLANES = 8
MXU_DIM = 256
VMEM_LIMIT = 56 * 1024 * 1024

CHUNK = 64
SCAN_GROUP = 4
HG_GROUP = 8
SCAN_STAGES = 8
EXP_NEG_HALF = 0.6065306597126334


def _params(*sem):
    return pltpu.CompilerParams(dimension_semantics=sem, vmem_limit_bytes=VMEM_LIMIT)


def _dot(a, b):
    return jnp.dot(a.astype(BF16), b.astype(BF16), preferred_element_type=F32)


def _dot_nt(a, b):
    return lax.dot_general(a.astype(BF16), b.astype(BF16), (((1,), (1,)), ((), ())),
                           preferred_element_type=F32)


def _dot_tn(a, b):
    return lax.dot_general(a.astype(BF16), b.astype(BF16), (((0,), (0,)), ((), ())),
                           preferred_element_type=F32)


def _sigmoid(x):
    return 0.5 * jnp.tanh(0.5 * x) + 0.5


def _silu(x):
    return x * _sigmoid(x)


def _softplus(z):
    return jnp.maximum(z, 0.0) + jnp.log(1.0 + jnp.exp(-jnp.abs(z)))


def _rms(x, gain):
    return x * lax.rsqrt(jnp.mean(x * x, axis=-1, keepdims=True) + RMS_EPS) * gain


def _cumsum_rows(x):
    n = x.shape[0]
    row = lax.broadcasted_iota(jnp.int32, x.shape, 0)
    d = 1
    while d < n:
        x = x + jnp.where(row >= d, pltpu.roll(x, d, 0), 0.0)
        d *= 2
    return x


def _seg_sum(x, ones_bd, split=True):
    hi = x.astype(BF16)
    lo = (x - hi.astype(F32)).astype(BF16) if split else None
    outs = []
    for g in range(x.shape[1] // MXU_DIM):
        sl = slice(g * MXU_DIM, (g + 1) * MXU_DIM)
        acc = jnp.dot(hi[:, sl], ones_bd, preferred_element_type=F32)
        if split:
            acc = acc + jnp.dot(lo[:, sl], ones_bd, preferred_element_type=F32)
        outs.append(acc)
    return jnp.concatenate(outs, axis=-1)


def _norm_proj_kernel(x_ref, g_ref, w_ref, o_ref, u_ref):
    @pl.when(pl.program_id(1) == 0)
    def _():
        u_ref[...] = _rms(x_ref[...], g_ref[...]).astype(BF16)

    o_ref[...] = jnp.dot(u_ref[...], w_ref[...], preferred_element_type=F32).astype(o_ref.dtype)


def _norm_proj(x2d, gain, w_bf16, tm=1024, tn=2048):
    t, d = x2d.shape
    n = w_bf16.shape[1]
    return pl.pallas_call(
        _norm_proj_kernel,
        grid=(t // tm, n // tn),
        in_specs=[pl.BlockSpec((tm, d), lambda i, j: (i, 0)),
                  pl.BlockSpec((1, d), lambda i, j: (0, 0)),
                  pl.BlockSpec((d, tn), lambda i, j: (0, j))],
        out_specs=pl.BlockSpec((None, tm, tn), lambda i, j: (j, i, 0)),
        out_shape=jax.ShapeDtypeStruct((n // tn, t, tn), BF16),
        scratch_shapes=[pltpu.VMEM((tm, d), BF16)],
        compiler_params=_params("parallel", "arbitrary"),
        name="l0_norm_proj",
    )(x2d, gain, w_bf16)


def _rglru_kernel(x_ref, g_ref, cw_ref, cb_ref, wa_ref, ba_ref, wx_ref, bx_ref, lam_ref,
                  o_ref, h_ref, ext_ref, a_ref, b_ref):
    @pl.when(pl.program_id(1) == 0)
    def _():
        h_ref[...] = jnp.zeros_like(h_ref)
        ext_ref[:SUBLANES, :] = jnp.zeros((SUBLANES, ext_ref.shape[1]), F32)

    x = x_ref[...].astype(F32)
    ts, width = x.shape
    ext_ref[SUBLANES:, :] = x
    xc = cb_ref[...] + cw_ref[RG_CONV - 1:RG_CONV, :] * x
    for k in range(1, RG_CONV):
        xc = xc + cw_ref[RG_CONV - 1 - k:RG_CONV - k, :] * ext_ref[SUBLANES - k:SUBLANES - k + ts, :]
    ext_ref[:SUBLANES, :] = x[ts - SUBLANES:, :]

    xcb = xc.astype(BF16)
    pre_r, pre_i = [], []
    for g in range(width // MXU_DIM):
        sl = slice(g * MXU_DIM, (g + 1) * MXU_DIM)
        pre_r.append(jnp.dot(xcb[:, sl], wa_ref[g], preferred_element_type=F32))
        pre_i.append(jnp.dot(xcb[:, sl], wx_ref[g], preferred_element_type=F32))
    gate_r = _sigmoid(jnp.concatenate(pre_r, axis=-1) + ba_ref[...])
    gate_i = _sigmoid(jnp.concatenate(pre_i, axis=-1) + bx_ref[...])

    log_a = (-RG_C) * gate_r * _softplus(-lam_ref[...])
    a = jnp.exp(log_a)
    b = jnp.sqrt(1.0 - a * a) * (gate_i * xc)

    a = a.reshape(ts // SUBLANES, SUBLANES, width)
    b = b.reshape(ts // SUBLANES, SUBLANES, width)
    in_group = lax.broadcasted_iota(jnp.int32, (1, SUBLANES, 1), 1)
    d = 1
    while d < SUBLANES:
        m = in_group >= d
        b = jnp.where(m, a, 0.0) * pltpu.roll(b, d, 1) + b
        a = a * jnp.where(m, pltpu.roll(a, d, 1), 1.0)
        d *= 2
    a_ref[...] = a.reshape(ts, width)
    b_ref[...] = b.reshape(ts, width)

    def carry_group(i, h):
        rows = pl.ds(pl.multiple_of(i * SUBLANES, SUBLANES), SUBLANES)
        hg = a_ref[rows, :] * h + b_ref[rows, :]
        b_ref[rows, :] = hg
        return hg[SUBLANES - 1:, :]

    h_ref[...] = lax.fori_loop(0, ts // SUBLANES, carry_group, h_ref[...], unroll=8)
    o_ref[...] = (b_ref[...] * _silu(g_ref[...].astype(F32))).astype(o_ref.dtype)


def _proj_block(proj4, split, block, offset=0):
    per_slab = proj4.shape[-1] // block
    col = split * (D_MODEL // block) + offset
    return col // per_slab, col % per_slab


def _rglru(proj4, conv_w, conv_b, wa_bd, b_a, wx_bd, b_x, lam, ts=256):
    _, bsz, seq, _ = proj4.shape
    width = D_MODEL
    row = lambda: pl.BlockSpec((1, width), lambda b, s: (0, 0))

    def split_spec(split):
        slab, blk = _proj_block(proj4, split, width)
        return pl.BlockSpec((None, None, ts, width), lambda b, s: (slab, b, s, blk))

    return pl.pallas_call(
        _rglru_kernel,
        grid=(bsz, seq // ts),
        in_specs=[split_spec(0), split_spec(1),
                  pl.BlockSpec((RG_CONV, width), lambda b, s: (0, 0)),
                  row(),
                  pl.BlockSpec(wa_bd.shape, lambda b, s: (0, 0, 0)),
                  row(),
                  pl.BlockSpec(wx_bd.shape, lambda b, s: (0, 0, 0)),
                  row(), row()],
        out_specs=pl.BlockSpec((None, ts, width), lambda b, s: (b, s, 0)),
        out_shape=jax.ShapeDtypeStruct((bsz, seq, width), BF16),
        scratch_shapes=[pltpu.VMEM((1, width), F32), pltpu.VMEM((SUBLANES + ts, width), F32),
                        pltpu.VMEM((ts, width), F32), pltpu.VMEM((ts, width), F32)],
        compiler_params=_params("parallel", "arbitrary"),
        name="l0_rglru",
    )(proj4, proj4, conv_w, conv_b, wa_bd, b_a, wx_bd, b_x, lam)


def _hgrn2_kernel(q_ref, f_ref, v_ref, g_ref, lbp_ref, gn_ref, o_ref, st_ref):
    @pl.when(pl.program_id(2) == 0)
    def _():
        st_ref[...] = jnp.zeros_like(st_ref)

    lbp = lbp_ref[...]
    e = jnp.exp(lbp - jnp.max(lbp, axis=0, keepdims=True))
    p = e / jnp.sum(e, axis=0, keepdims=True)
    lb = (p[0:1] + p[1:2]) - p[0:1]
    gn = gn_ref[...]
    ts = q_ref.shape[0]
    srow = lax.broadcasted_iota(jnp.int32, (CHUNK, CHUNK), 0)
    scol = lax.broadcasted_iota(jnp.int32, (CHUNK, CHUNK), 1)
    levels = []
    h = CHUNK // 2
    while h >= 1:
        levels.append((h, (srow & ~(2 * h - 1)) == (scol & ~(2 * h - 1))))
        h //= 2
    sub_row = lax.broadcasted_iota(jnp.int32, (SUBLANES, 1), 0)
    chunk_row = lax.broadcasted_iota(jnp.int32, (CHUNK, 1), 0)
    tril3 = jnp.where(srow >= scol, 1.0, 0.0).astype(BF16)
    tril3 = jnp.concatenate([tril3, tril3, tril3], axis=1)

    def cumsum_rows(x):
        hi = x.astype(BF16)
        r1 = x - hi.astype(F32)
        mid = r1.astype(BF16)
        lo = (r1 - mid.astype(F32)).astype(BF16)
        return jnp.dot(tril3, jnp.concatenate([hi, mid, lo], axis=0), preferred_element_type=F32)

    def block_rows(x, size, r):
        out = []
        for lo in range(0, CHUNK, SUBLANES):
            acc = jnp.broadcast_to(x[lo + r:lo + r + 1], (SUBLANES, LANES))
            for off in range(size, SUBLANES, size):
                acc = jnp.where(sub_row >= off,
                                jnp.broadcast_to(x[lo + off + r:lo + off + r + 1], (SUBLANES, LANES)), acc)
            out.append(acc)
        return jnp.concatenate(out, axis=0)

    def level_factors(q, k, f, g2, h):
        if h < SUBLANES:
            upper = (chunk_row & (2 * h - 1)) >= h
            if h == 1:
                return jnp.where(upper, q * f, 0.0), jnp.where(upper, 0.0, k)
            d = g2 - block_rows(g2, 2 * h, h - 1)
            e = jnp.exp2(jnp.where(upper, d, -d))
            return jnp.where(upper, q * e, 0.0), jnp.where(upper, 0.0, k * e)
        zero = jnp.zeros((h, LANES), F32)
        qt, kt = [], []
        for lo in range(0, CHUNK, 2 * h):
            mid = lo + h
            rho = g2[mid - 1:mid, :]
            kt += [k[lo:mid] * jnp.exp2(rho - g2[lo:mid]), zero]
            qt += [zero, q[mid:mid + h] * jnp.exp2(g2[mid:mid + h] - rho)]
        return jnp.concatenate(qt, axis=0), jnp.concatenate(kt, axis=0)

    def group(i, carry):
        idx = range(HG_GROUP)
        rows = [pl.ds(pl.multiple_of((i * HG_GROUP + j) * CHUNK, CHUNK), CHUNK) for j in idx]
        q = [_silu(q_ref[r, :].astype(F32)) for r in rows]
        sig = [_sigmoid(f_ref[r, :].astype(F32)) for r in rows]
        k = [(1.0 - lb) * (1.0 - x) for x in sig]
        f = [lb + (1.0 - lb) * x for x in sig]
        v = [v_ref[r, :].astype(BF16) for r in rows]
        g2 = [cumsum_rows(jnp.log2(x)) for x in f]
        gend = [x[CHUNK - 1:CHUNK, :] for x in g2]
        qg = [(a * jnp.exp2(b)).astype(BF16) for a, b in zip(q, g2)]
        vtk = [_dot_tn(a, b * jnp.exp2(e - c)) for a, b, c, e in zip(v, k, g2, gend)]

        scores = [jnp.where(srow == scol, jnp.sum(a * b, axis=-1, keepdims=True), 0.0)
                  for a, b in zip(q, k)]
        for h, same_block in levels:
            for j in idx:
                qt, kt = level_factors(q[j], k[j], f[j], g2[j], h)
                scores[j] = scores[j] + jnp.where(same_block, _dot_nt(qt, kt), 0.0)
        intra = [_dot(a, b) for a, b in zip(scores, v)]

        st = st_ref[...]
        for j in idx:
            o = _dot_nt(qg[j], st) + intra[j]
            st = st * jnp.exp2(gend[j]) + vtk[j]
            o = _rms(o, gn) * _silu(g_ref[rows[j], :].astype(F32))
            o_ref[rows[j], :] = o.astype(o_ref.dtype)
        st_ref[...] = st
        return carry

    lax.fori_loop(0, ts // (CHUNK * HG_GROUP), group, 0)


def _hgrn2(proj4, lb_params, g_norm, ts=1024):
    _, bsz, seq, _ = proj4.shape
    width = D_MODEL

    def spec(split):
        slab, blk = _proj_block(proj4, split, HG_DK)
        return pl.BlockSpec((None, None, ts, HG_DK), lambda b, h, s: (slab, b, s, blk + h))

    return pl.pallas_call(
        _hgrn2_kernel,
        grid=(bsz, HG_HEADS, seq // ts),
        in_specs=[spec(2), spec(3), spec(4), spec(5),
                  pl.BlockSpec((lb_params.shape[0], HG_DK), lambda b, h, s: (0, h)),
                  pl.BlockSpec((1, HG_DK), lambda b, h, s: (0, 0))],
        out_specs=pl.BlockSpec((None, ts, HG_DK), lambda b, h, s: (b, s, h)),
        out_shape=jax.ShapeDtypeStruct((bsz, seq, width), BF16),
        scratch_shapes=[pltpu.VMEM((HG_DK, HG_DK), F32)],
        compiler_params=_params("parallel", "parallel", "arbitrary"),
        name="l0_hgrn2",
    )(proj4, proj4, proj4, proj4, lb_params, g_norm)


def _out_proj0_kernel(hg_ref, og_ref, w_ref, pn_ref, x_ref, o_ref):
    half = hg_ref.shape[1]
    y = (jnp.dot(hg_ref[...], w_ref[:half, :], preferred_element_type=F32)
         + jnp.dot(og_ref[...], w_ref[half:, :], preferred_element_type=F32))
    o_ref[...] = x_ref[...] + _rms(y, pn_ref[...])


def _out_proj0(hg, og, w_bf16, post_gain, x2d, tm=512):
    t, d = x2d.shape
    tok = lambda: pl.BlockSpec((tm, d), lambda i: (i, 0))
    return pl.pallas_call(
        _out_proj0_kernel,
        grid=(t // tm,),
        in_specs=[tok(), tok(),
                  pl.BlockSpec(w_bf16.shape, lambda i: (0, 0)),
                  pl.BlockSpec((1, d), lambda i: (0, 0)),
                  tok()],
        out_specs=tok(),
        out_shape=jax.ShapeDtypeStruct((t, d), F32),
        compiler_params=_params("parallel"),
        name="l0_out_proj",
    )(hg, og, w_bf16, post_gain, x2d)


def _rwkv_front_kernel(x_ref, pn_ref, mu_ref, win_ref, w0_ref, w1_ref, w2_ref, a0_ref, a1_ref,
                       a2_ref, kk_ref, ka_ref, ones_ref,
                       r_out, lw_out, k_out, v_out, kk_out, b_out, g_out, prev_ref):
    @pl.when(pl.program_id(1) == 0)
    def _():
        prev_ref[...] = jnp.zeros_like(prev_ref)

    u = _rms(x_ref[...], pn_ref[...])
    tm = u.shape[0]
    row = lax.broadcasted_iota(jnp.int32, (tm, 1), 0)
    delta = jnp.where(row == 0, prev_ref[...], pltpu.roll(u, 1, 0)) - u
    prev_ref[...] = u[tm - 1:tm, :]
    mix = lambda p: u + delta * mu_ref[p:p + 1, :]

    r_out[...] = _dot(mix(0), win_ref[0]).astype(r_out.dtype)
    k = _dot(mix(1), win_ref[1])
    v_out[...] = _dot(mix(2), win_ref[2]).astype(v_out.dtype)
    g_out[...] = _dot(mix(3), win_ref[3]).astype(g_out.dtype)
    lora_w = _dot(jnp.tanh(_dot(mix(4), w1_ref[...])), w2_ref[...])
    lora_a = _dot(_dot(mix(5), a1_ref[...]), a2_ref[...])
    lw_out[...] = (-EXP_NEG_HALF) * _sigmoid(w0_ref[...] + lora_w)
    iclr = _sigmoid(a0_ref[...] + lora_a)
    k_out[...] = (k * (1.0 + (iclr - 1.0) * ka_ref[...])).astype(k_out.dtype)
    kk = k * kk_ref[...]
    kk = kk * lax.rsqrt(jnp.maximum(_seg_sum(kk * kk, ones_ref[...]), 1e-24))
    kk_out[...] = kk.astype(kk_out.dtype)
    b_out[...] = (kk * iclr).astype(b_out.dtype)


def _rwkv_front(x1, pre_gain, mu, win, w0, w1, w2, a0, a1, a2, k_k, k_a, ones_bd, tm=512):
    bsz, seq, d = x1.shape
    tok = lambda: pl.BlockSpec((None, tm, d), lambda b, s: (b, s, 0))
    full = lambda a: pl.BlockSpec(a.shape, lambda b, s: (0,) * a.ndim)
    args = (pre_gain, mu, win, w0, w1, w2, a0, a1, a2, k_k, k_a, ones_bd)
    return pl.pallas_call(
        _rwkv_front_kernel,
        grid=(bsz, seq // tm),
        in_specs=[tok()] + [full(a) for a in args],
        out_specs=[tok() for _ in range(7)],
        out_shape=[jax.ShapeDtypeStruct((bsz, seq, d), F32 if i == 1 else BF16) for i in range(7)],
        scratch_shapes=[pltpu.VMEM((1, d), F32)],
        compiler_params=_params("parallel", "arbitrary"),
        name="l1_front",
    )(x1, *args)


def _rwkv_scan_kernel(r_ref, lw_ref, k_ref, v_ref, kk_ref, b_ref, y_ref,
                      ht_ref, rt_s, y0_s, ml_s, ha_s, gm_s):
    @pl.when(pl.program_id(2) == 0)
    def _():
        ht_ref[...] = jnp.zeros_like(ht_ref)

    ts, width = r_ref.shape
    nchunk = ts // CHUNK
    npair = width // LANES
    lanes_of = [slice(p * LANES, (p + 1) * LANES) for p in range(npair)]
    two = 2 * CHUNK
    head0 = lax.broadcasted_iota(jnp.int32, (CHUNK, LANES), 1) < RW_HEAD
    row = lax.broadcasted_iota(jnp.int32, (two, two), 0)
    col = lax.broadcasted_iota(jnp.int32, (two, two), 1)
    strict = (row & (CHUNK - 1)) > (col & (CHUNK - 1))
    incl = (row & (CHUNK - 1)) >= (col & (CHUNK - 1))
    eye = jnp.where(row == col, 1.0, 0.0)

    def stack(x):
        return jnp.concatenate([jnp.where(head0, x, 0.0), jnp.where(head0, 0.0, x)], axis=0)

    def prep(c):
        rows = pl.ds(pl.multiple_of(c * CHUNK, CHUNK), CHUNK)
        r, lw, k, v, kk, b = (ref[rows, :].astype(F32)
                              for ref in (r_ref, lw_ref, k_ref, v_ref, kk_ref, b_ref))
        lcum = _cumsum_rows(lw)
        lend = lcum[CHUNK - 1:CHUNK, :]
        e_neg = jnp.exp(-lcum)
        e_end = jnp.exp(lend - lcum)
        gm_s[c] = jnp.exp(lend)
        at, rt = -kk * jnp.exp(lcum - lw), r * jnp.exp(lcum)
        bt, kt, bh, kh = b * e_neg, k * e_neg, b * e_end, k * e_end
        return [dict(rows=rows, c=c, p=p, sl=sl,
                     at2=stack(at[:, sl]).astype(BF16),
                     rt2=stack(rt[:, sl]),
                     bk2=jnp.concatenate([stack(bt[:, sl]), stack(kt[:, sl])], axis=0).astype(BF16),
                     v2=stack(v[:, sl]).astype(BF16),
                     bkh2=jnp.concatenate([stack(bh[:, sl]), stack(kh[:, sl])], axis=0).astype(BF16))
                for p, sl in enumerate(lanes_of)]

    def local_group(i, between=()):
        pending = list(between)
        passed = [0]

        def boundary():
            fired = len(between) - len(pending)
            if pending and passed[0] * len(between) >= fired * SCAN_STAGES:
                pending.pop(0)()
            passed[0] += 1

        g = [q for j in range(SCAN_GROUP) for q in prep(i * SCAN_GROUP + j)]
        s = [_dot_nt(jnp.concatenate([q["at2"], q["rt2"].astype(BF16)], axis=0), q["bk2"]) for q in g]
        boundary()
        n = [jnp.where(strict, x[:two, :two], 0.0) for x in s]
        a_ak = [jnp.where(strict, x[:two, two:], 0.0).astype(BF16) for x in s]
        a_rbk = [jnp.concatenate([jnp.where(incl, x[two:, :two], 0.0),
                                  jnp.where(incl, x[two:, two:], 0.0)], axis=1).astype(BF16) for x in s]
        t = [eye + x for x in n]
        pw = [_dot(x, x).astype(BF16) for x in n]
        boundary()
        d = 2
        while d < CHUNK // 2:
            m = [_dot(jnp.concatenate([a.astype(BF16), b], axis=0), b) for a, b in zip(t, pw)]
            t = [a + x[:two] for a, x in zip(t, m)]
            pw = [x[two:].astype(BF16) for x in m]
            boundary()
            d *= 2
        t = [a + _dot(a, b) for a, b in zip(t, pw)]
        boundary()
        akv = [_dot(x, q["v2"]) for x, q in zip(a_ak, g)]
        tw = [_dot(x, jnp.concatenate([q["at2"], y.astype(BF16)], axis=1)).astype(BF16)
              for x, y, q in zip(t, akv, g)]
        boundary()
        rw = [_dot(x[:, :two], y[:, :LANES]) for x, y in zip(a_rbk, tw)]
        y0 = [_dot(x, jnp.concatenate([y[:, LANES:], q["v2"]], axis=0)) for x, y, q in zip(a_rbk, tw, g)]
        ml = [_dot_tn(q["bkh2"][:two], y[:, :LANES]) for q, y in zip(g, tw)]
        ha = [_dot_tn(jnp.concatenate([y[:, LANES:], q["v2"]], axis=0), q["bkh2"]) for q, y in zip(g, tw)]
        while pending:
            pending.pop(0)()
        for j, q in enumerate(g):
            rtp = q["rt2"] + rw[j]
            rt_s[q["rows"], q["sl"]] = (rtp[:CHUNK] + rtp[CHUNK:]).astype(BF16)
            y0_s[q["rows"], q["sl"]] = y0[j][:CHUNK] + y0[j][CHUNK:]
            ml_s[q["c"], q["p"]] = ml[j].astype(BF16)
            ha_s[q["c"], q["p"]] = ha[j]

    def state_step(c, hts):
        rows = pl.ds(pl.multiple_of(c * CHUNK, CHUNK), CHUNK)
        rt, gm = rt_s[rows, :], gm_s[c]
        htb = [h.astype(BF16) for h in hts]
        ys = [_dot_nt(rt[:, sl], hb) for sl, hb in zip(lanes_of, htb)]
        new = [h * gm[:, sl] + _dot_nt(hb, ml_s[c, p]) + ha_s[c, p]
               for p, (sl, h, hb) in enumerate(zip(lanes_of, hts, htb))]
        y_ref[rows, :] = (jnp.concatenate(ys, axis=1) + y0_s[rows, :]).astype(y_ref.dtype)
        return tuple(new)

    def pipelined(i, hts):
        box = [hts]

        def advance(j):
            def run():
                box[0] = state_step((i - 1) * SCAN_GROUP + j, box[0])
            return run

        local_group(i, between=[advance(j) for j in range(SCAN_GROUP)])
        return box[0]

    ngroup = nchunk // SCAN_GROUP
    local_group(0)
    hts = lax.fori_loop(1, ngroup, pipelined, tuple(ht_ref[p] for p in range(npair)))
    for j in range(SCAN_GROUP):
        hts = state_step((ngroup - 1) * SCAN_GROUP + j, hts)
    for p in range(npair):
        ht_ref[p] = hts[p]


def _rwkv_scan(r, lw, k, v, kk, b, ts=1024, width=512):
    bsz, seq, d = r.shape
    nchunk = ts // CHUNK
    npair = width // LANES
    spec = lambda: pl.BlockSpec((None, ts, width), lambda bb, h, s: (bb, s, h))
    return pl.pallas_call(
        _rwkv_scan_kernel,
        grid=(bsz, d // width, seq // ts),
        in_specs=[spec() for _ in range(6)],
        out_specs=spec(),
        out_shape=jax.ShapeDtypeStruct((bsz, seq, d), BF16),
        scratch_shapes=[pltpu.VMEM((npair, LANES, LANES), F32),
                        pltpu.VMEM((ts, width), BF16),
                        pltpu.VMEM((ts, width), F32),
                        pltpu.VMEM((nchunk, npair, LANES, LANES), BF16),
                        pltpu.VMEM((nchunk, npair, LANES, LANES), F32),
                        pltpu.VMEM((nchunk, 1, width), F32)],
        compiler_params=_params("parallel", "parallel", "arbitrary"),
        name="l1_scan",
    )(r, lw, k, v, kk, b)


def _rwkv_back_kernel(y_ref, r_ref, k_ref, v_ref, g_ref, x_ref, rk_ref, lnw_ref, lnb_ref,
                      w_ref, pn_ref, ones_ref, o_ref):
    ones = ones_ref[...]
    inv_n = 1.0 / RW_HEAD
    y = y_ref[...].astype(F32)
    yc = y - _seg_sum(y, ones, split=False) * inv_n
    var = _seg_sum(yc * yc, ones, split=False) * inv_n
    yn = yc * lax.rsqrt(var + RW_GN_EPS) * lnw_ref[...] + lnb_ref[...]
    rk = r_ref[...].astype(F32) * k_ref[...].astype(F32) * rk_ref[...]
    yn = yn + _seg_sum(rk, ones, split=False) * v_ref[...].astype(F32)
    out = _dot(yn * _silu(g_ref[...].astype(F32)), w_ref[...])
    o_ref[...] = x_ref[...] + _rms(out, pn_ref[...])


def _rwkv_back(y, r, k, v, g, x1, r_k, ln_w, ln_b, w_bf16, post_gain, ones_bd, tm=512):
    t, d = y.shape
    tok = lambda: pl.BlockSpec((tm, d), lambda i: (i, 0))
    full = lambda a: pl.BlockSpec(a.shape, lambda i: (0,) * a.ndim)
    params = (r_k, ln_w, ln_b, w_bf16, post_gain, ones_bd)
    return pl.pallas_call(
        _rwkv_back_kernel,
        grid=(t // tm,),
        in_specs=[tok() for _ in range(6)] + [full(a) for a in params],
        out_specs=tok(),
        out_shape=jax.ShapeDtypeStruct((t, d), F32),
        compiler_params=_params("parallel"),
        name="l1_back",
    )(y, r, k, v, g, x1, *params)


def _block_diag(w, group):
    n, bi, bj = w.shape
    w = w.reshape(n // group, group, bi, bj)
    eye = jnp.eye(group, dtype=w.dtype)
    return jnp.einsum("gaij,ab->gaibj", w, eye).reshape(n // group, group * bi, group * bj)


def kernel(x, pre_norm, post_norm, ab_w_in, ab_w_out, rg_conv_w, rg_conv_b, rg_w_a, rg_b_a,
           rg_w_x, rg_b_x, rg_lambda, hg_lower_bounds, hg_out_norm, rw_mu, rw_w_in, rw_w_out,
           rw_w0, rw_w1, rw_w2, rw_a0, rw_a1, rw_a2, rw_k_k, rw_k_a, rw_r_k, rw_ln_w, rw_ln_b):
    bsz, seq, d = x.shape
    t = bsz * seq
    row = lambda a: a.reshape(1, -1).astype(F32)
    x2d = x.reshape(t, d)
    heads_per_tile = MXU_DIM // RW_HEAD
    ones_bd = _block_diag(jnp.ones((heads_per_tile, RW_HEAD, RW_HEAD), F32), heads_per_tile)[0].astype(BF16)

    proj = _norm_proj(x2d, row(pre_norm[0]), ab_w_in[0].astype(BF16))
    proj4 = proj.reshape(proj.shape[0], bsz, seq, proj.shape[-1])
    rg_group = MXU_DIM // rg_w_a.shape[-1]
    hg = _rglru(proj4, rg_conv_w[0].astype(F32), row(rg_conv_b[0]),
                _block_diag(rg_w_a[0], rg_group).astype(BF16), row(rg_b_a[0]),
                _block_diag(rg_w_x[0], rg_group).astype(BF16), row(rg_b_x[0]), row(rg_lambda[0]))
    og = _hgrn2(proj4, hg_lower_bounds.astype(F32), row(hg_out_norm[0]))
    x1 = _out_proj0(hg.reshape(t, d), og.reshape(t, d), ab_w_out[0].astype(BF16),
                    row(post_norm[0]), x2d)

    r, lw, k, v, kk, b, g = _rwkv_front(
        x1.reshape(bsz, seq, d), row(pre_norm[1]), rw_mu[0].astype(F32), rw_w_in[0].astype(BF16),
        row(rw_w0[0]), rw_w1[0].astype(BF16), rw_w2[0].astype(BF16), row(rw_a0[0]),
        rw_a1[0].astype(BF16), rw_a2[0].astype(BF16), row(rw_k_k[0]), row(rw_k_a[0]), ones_bd)
    y = _rwkv_scan(r, lw, k, v, kk, b)
    flat = lambda a: a.reshape(t, d)
    x2 = _rwkv_back(flat(y), flat(r), flat(k), flat(v), flat(g), x1, row(rw_r_k[0]),
                    row(rw_ln_w[0]), row(rw_ln_b[0]), rw_w_out[0].astype(BF16),
                    row(post_norm[1]), ones_bd)
    return x2.reshape(bsz, seq, d)
```

```python
import functools

import jax
import jax.numpy as jnp
from jax import lax
from jax.experimental import pallas as pl
from jax.experimental.pallas import tpu as pltpu

F32 = jnp.float32
BF16 = jnp.bfloat16

D_MODEL = 1024
RMS_EPS = 1e-6
RG_C = 8.0
RG_CONV = 4
HG_HEADS = 8
HG_DK = 128
RW_HEAD = 64
RW_GN_EPS = 64e-5

LANES = 128
SUBLANES = 8
MXU_DIM = 256
VMEM_LIMIT = 56 * 1024 * 1024

CHUNK = 64
SCAN_GROUP = 4
HG_GROUP = 8
SCAN_STAGES = 8
EXP_NEG_HALF = 0.6065306597126334


def _params(*sem):
    return pltpu.CompilerParams(dimension_semantics=sem, vmem_limit_bytes=VMEM_LIMIT)


def _dot(a, b):
    return jnp.dot(a.astype(BF16), b.astype(BF16), preferred_element_type=F32)


def _dot_nt(a, b):
    return lax.dot_general(a.astype(BF16), b.astype(BF16), (((1,), (1,)), ((), ())),
                           preferred_element_type=F32)


def _dot_tn(a, b):
    return lax.dot_general(a.astype(BF16), b.astype(BF16), (((0,), (0,)), ((), ())),
                           preferred_element_type=F32)


def _sigmoid(x):
    return 0.5 * jnp.tanh(0.5 * x) + 0.5


def _silu(x):
    return x * _sigmoid(x)


def _softplus(z):
    return jnp.maximum(z, 0.0) + jnp.log(1.0 + jnp.exp(-jnp.abs(z)))


def _rms(x, gain):
    return x * lax.rsqrt(jnp.mean(x * x, axis=-1, keepdims=True) + RMS_EPS) * gain


def _cumsum_rows(x):
    n = x.shape[0]
    row = lax.broadcasted_iota(jnp.int32, x.shape, 0)
    d = 1
    while d < n:
        x = x + jnp.where(row >= d, pltpu.roll(x, d, 0), 0.0)
        d *= 2
    return x


def _seg_sum(x, ones_bd, split=True):
    hi = x.astype(BF16)
    lo = (x - hi.astype(F32)).astype(BF16) if split else None
    outs = []
    for g in range(x.shape[1] // MXU_DIM):
        sl = slice(g * MXU_DIM, (g + 1) * MXU_DIM)
        acc = jnp.dot(hi[:, sl], ones_bd, preferred_element_type=F32)
        if split:
            acc = acc + jnp.dot(lo[:, sl], ones_bd, preferred_element_type=F32)
        outs.append(acc)
    return jnp.concatenate(outs, axis=-1)


def _norm_proj_kernel(x_ref, g_ref, w_ref, o_ref, u_ref):
    @pl.when(pl.program_id(1) == 0)
    def _():
        u_ref[...] = _rms(x_ref[...], g_ref[...]).astype(BF16)

    o_ref[...] = jnp.dot(u_ref[...], w_ref[...], preferred_element_type=F32).astype(o_ref.dtype)


def _norm_proj(x2d, gain, w_bf16, n, tm=1024, tn=2048):
    t, d = x2d.shape
    return pl.pallas_call(
        _norm_proj_kernel,
        grid=(t // tm, n // tn),
        in_specs=[pl.BlockSpec((tm, d), lambda i, j: (i, 0)),
                  pl.BlockSpec((1, d), lambda i, j: (0, 0)),
                  pl.BlockSpec((d, tn), lambda i, j: (0, j))],
        out_specs=[pl.BlockSpec((None, tm, tn), lambda i, j: (j, i, 0)),
                   pl.BlockSpec((tm, d), lambda i, j: (i, 0))],
        out_shape=[jax.ShapeDtypeStruct((n // tn, t, tn), BF16),
                   jax.ShapeDtypeStruct((t, d), BF16)],
        compiler_params=_params("parallel", "arbitrary"),
        name="l0_norm_proj",
    )(x2d, gain, w_bf16)


def _rglru_kernel(x_ref, g_ref, cw_ref, cb_ref, wa_ref, ba_ref, wx_ref, bx_ref, lam_ref,
                  o_ref, h_ref, ext_ref, a_ref, b_ref):
    @pl.when(pl.program_id(1) == 0)
    def _():
        h_ref[...] = jnp.zeros_like(h_ref)
        ext_ref[:SUBLANES, :] = jnp.zeros((SUBLANES, ext_ref.shape[1]), F32)

    x = x_ref[...].astype(F32)
    ts, width = x.shape
    ext_ref[SUBLANES:, :] = x
    xc = cb_ref[...] + cw_ref[RG_CONV - 1:RG_CONV, :] * x
    for k in range(1, RG_CONV):
        xc = xc + cw_ref[RG_CONV - 1 - k:RG_CONV - k, :] * ext_ref[SUBLANES - k:SUBLANES - k + ts, :]
    ext_ref[:SUBLANES, :] = x[ts - SUBLANES:, :]

    xcb = xc.astype(BF16)
    pre_r, pre_i = [], []
    for g in range(width // MXU_DIM):
        sl = slice(g * MXU_DIM, (g + 1) * MXU_DIM)
        pre_r.append(jnp.dot(xcb[:, sl], wa_ref[g], preferred_element_type=F32))
        pre_i.append(jnp.dot(xcb[:, sl], wx_ref[g], preferred_element_type=F32))
    gate_r = _sigmoid(jnp.concatenate(pre_r, axis=-1) + ba_ref[...])
    gate_i = _sigmoid(jnp.concatenate(pre_i, axis=-1) + bx_ref[...])

    log_a = (-RG_C) * gate_r * _softplus(-lam_ref[...])
    a = jnp.exp(log_a)
    b = jnp.sqrt(1.0 - a * a) * (gate_i * xc)

    a = a.reshape(ts // SUBLANES, SUBLANES, width)
    b = b.reshape(ts // SUBLANES, SUBLANES, width)
    in_group = lax.broadcasted_iota(jnp.int32, (1, SUBLANES, 1), 1)
    d = 1
    while d < SUBLANES:
        m = in_group >= d
        b = jnp.where(m, a, 0.0) * pltpu.roll(b, d, 1) + b
        a = a * jnp.where(m, pltpu.roll(a, d, 1), 1.0)
        d *= 2
    a_ref[...] = a.reshape(ts, width)
    b_ref[...] = b.reshape(ts, width)

    def carry_group(i, h):
        rows = pl.ds(pl.multiple_of(i * SUBLANES, SUBLANES), SUBLANES)
        hg = a_ref[rows, :] * h + b_ref[rows, :]
        b_ref[rows, :] = hg
        return hg[SUBLANES - 1:, :]

    h_ref[...] = lax.fori_loop(0, ts // SUBLANES, carry_group, h_ref[...], unroll=8)
    o_ref[...] = (b_ref[...] * _silu(g_ref[...].astype(F32))).astype(o_ref.dtype)


def _proj_block(proj4, split, block, offset=0):
    per_slab = proj4.shape[-1] // block
    col = split * (D_MODEL // block) + offset
    return col // per_slab, col % per_slab


def _rglru(proj4, conv_w, conv_b, wa_bd, b_a, wx_bd, b_x, lam, ts=256):
    _, bsz, seq, _ = proj4.shape
    width = D_MODEL
    row = lambda: pl.BlockSpec((1, width), lambda b, s: (0, 0))

    def split_spec(split):
        slab, blk = _proj_block(proj4, split, width)
        return pl.BlockSpec((None, None, ts, width), lambda b, s: (slab, b, s, blk))

    return pl.pallas_call(
        _rglru_kernel,
        grid=(bsz, seq // ts),
        in_specs=[split_spec(0), split_spec(1),
                  pl.BlockSpec((RG_CONV, width), lambda b, s: (0, 0)),
                  row(),
                  pl.BlockSpec(wa_bd.shape, lambda b, s: (0, 0, 0)),
                  row(),
                  pl.BlockSpec(wx_bd.shape, lambda b, s: (0, 0, 0)),
                  row(), row()],
        out_specs=pl.BlockSpec((None, ts, width), lambda b, s: (b, s, 0)),
        out_shape=jax.ShapeDtypeStruct((bsz, seq, width), BF16),
        scratch_shapes=[pltpu.VMEM((1, width), F32), pltpu.VMEM((SUBLANES + ts, width), F32),
                        pltpu.VMEM((ts, width), F32), pltpu.VMEM((ts, width), F32)],
        compiler_params=_params("parallel", "arbitrary"),
        name="l0_rglru",
    )(proj4, proj4, conv_w, conv_b, wa_bd, b_a, wx_bd, b_x, lam)


def _hgrn2_kernel(u_ref, w_ref, lbp_ref, gn_ref, o_ref, st_ref, p_ref):
    @pl.when(pl.program_id(2) == 0)
    def _():
        st_ref[...] = jnp.zeros_like(st_ref)

    group_rows = CHUNK * HG_GROUP
    col = lambda j: slice(j * HG_DK, (j + 1) * HG_DK)

    def project(g, slot, half):
        rows = pl.ds(pl.multiple_of(g * group_rows, group_rows), group_rows)
        cols = slice(half * 2 * HG_DK, (half + 1) * 2 * HG_DK)
        p_ref[slot, :, cols] = jnp.dot(u_ref[rows, :], w_ref[:, cols], preferred_element_type=F32)

    lbp = lbp_ref[...]
    e = jnp.exp(lbp - jnp.max(lbp, axis=0, keepdims=True))
    p = e / jnp.sum(e, axis=0, keepdims=True)
    lb = (p[0:1] + p[1:2]) - p[0:1]
    gn = gn_ref[...]
    ngroup = u_ref.shape[0] // group_rows
    srow = lax.broadcasted_iota(jnp.int32, (CHUNK, CHUNK), 0)
    scol = lax.broadcasted_iota(jnp.int32, (CHUNK, CHUNK), 1)
    levels = []
    h = CHUNK // 2
    while h >= 1:
        levels.append((h, (srow & ~(2 * h - 1)) == (scol & ~(2 * h - 1))))
        h //= 2
    sub_row = lax.broadcasted_iota(jnp.int32, (SUBLANES, 1), 0)
    chunk_row = lax.broadcasted_iota(jnp.int32, (CHUNK, 1), 0)
    tril3 = jnp.where(srow >= scol, 1.0, 0.0).astype(BF16)
    tril3 = jnp.concatenate([tril3, tril3, tril3], axis=1)

    def cumsum_rows(x):
        hi = x.astype(BF16)
        r1 = x - hi.astype(F32)
        mid = r1.astype(BF16)
        lo = (r1 - mid.astype(F32)).astype(BF16)
        return jnp.dot(tril3, jnp.concatenate([hi, mid, lo], axis=0), preferred_element_type=F32)

    def block_rows(x, size, r):
        out = []
        for lo in range(0, CHUNK, SUBLANES):
            acc = jnp.broadcast_to(x[lo + r:lo + r + 1], (SUBLANES, LANES))
            for off in range(size, SUBLANES, size):
                acc = jnp.where(sub_row >= off,
                                jnp.broadcast_to(x[lo + off + r:lo + off + r + 1], (SUBLANES, LANES)), acc)
            out.append(acc)
        return jnp.concatenate(out, axis=0)

    def level_factors(q, k, f, g2, h):
        if h < SUBLANES:
            upper = (chunk_row & (2 * h - 1)) >= h
            if h == 1:
                return jnp.where(upper, q * f, 0.0), jnp.where(upper, 0.0, k)
            d = g2 - block_rows(g2, 2 * h, h - 1)
            e = jnp.exp2(jnp.where(upper, d, -d))
            return jnp.where(upper, q * e, 0.0), jnp.where(upper, 0.0, k * e)
        zero = jnp.zeros((h, LANES), F32)
        qt, kt = [], []
        for lo in range(0, CHUNK, 2 * h):
            mid = lo + h
            rho = g2[mid - 1:mid, :]
            kt += [k[lo:mid] * jnp.exp2(rho - g2[lo:mid]), zero]
            qt += [zero, q[mid:mid + h] * jnp.exp2(g2[mid:mid + h] - rho)]
        return jnp.concatenate(qt, axis=0), jnp.concatenate(kt, axis=0)

    def group(i, carry):
        idx = range(HG_GROUP)
        slot = i % 2
        nxt = jnp.minimum(i + 1, ngroup - 1)
        local = [pl.ds(j * CHUNK, CHUNK) for j in idx]
        rows = [pl.ds(pl.multiple_of((i * HG_GROUP + j) * CHUNK, CHUNK), CHUNK) for j in idx]
        q = [_silu(p_ref[slot, r, col(0)]) for r in local]
        sig = [_sigmoid(p_ref[slot, r, col(1)]) for r in local]
        v = [p_ref[slot, r, col(2)].astype(BF16) for r in local]
        gate = [p_ref[slot, r, col(3)] for r in local]
        k = [(1.0 - lb) * (1.0 - x) for x in sig]
        f = [lb + (1.0 - lb) * x for x in sig]
        g2 = [cumsum_rows(jnp.log2(x)) for x in f]
        project(nxt, 1 - slot, 0)
        gend = [x[CHUNK - 1:CHUNK, :] for x in g2]
        qg = [(a * jnp.exp2(b)).astype(BF16) for a, b in zip(q, g2)]
        vtk = [_dot_tn(a, b * jnp.exp2(e - c)) for a, b, c, e in zip(v, k, g2, gend)]

        scores = [jnp.where(srow == scol, jnp.sum(a * b, axis=-1, keepdims=True), 0.0)
                  for a, b in zip(q, k)]
        for h, same_block in levels:
            for j in idx:
                qt, kt = level_factors(q[j], k[j], f[j], g2[j], h)
                scores[j] = scores[j] + jnp.where(same_block, _dot_nt(qt, kt), 0.0)
        out = [_dot(a, b) for a, b in zip(scores, v)]

        st = st_ref[...]
        for j in idx:
            out[j] = out[j] + _dot_nt(qg[j], st)
            st = st * jnp.exp2(gend[j]) + vtk[j]
        st_ref[...] = st
        project(nxt, 1 - slot, 1)
        for j in idx:
            o_ref[rows[j], :] = (_rms(out[j], gn) * _silu(gate[j])).astype(o_ref.dtype)
        return carry

    project(0, 0, 0)
    project(0, 0, 1)
    lax.fori_loop(0, ngroup, group, 0)


def _hgrn2(u3, w_heads, lb_params, g_norm, ts=8192):
    bsz, seq, d = u3.shape
    per_head = 4 * HG_DK
    return pl.pallas_call(
        _hgrn2_kernel,
        grid=(bsz, HG_HEADS, seq // ts),
        in_specs=[pl.BlockSpec((None, ts, d), lambda b, h, s: (b, s, 0)),
                  pl.BlockSpec((d, per_head), lambda b, h, s: (0, h)),
                  pl.BlockSpec((lb_params.shape[0], HG_DK), lambda b, h, s: (0, h)),
                  pl.BlockSpec((1, HG_DK), lambda b, h, s: (0, 0))],
        out_specs=pl.BlockSpec((None, ts, HG_DK), lambda b, h, s: (b, s, h)),
        out_shape=jax.ShapeDtypeStruct((bsz, seq, d), BF16),
        scratch_shapes=[pltpu.VMEM((HG_DK, HG_DK), F32),
                        pltpu.VMEM((2, CHUNK * HG_GROUP, per_head), F32)],
        compiler_params=_params("parallel", "parallel", "arbitrary"),
        name="l0_hgrn2",
    )(u3, w_heads, lb_params, g_norm)


def _out_proj0_kernel(hg_ref, og_ref, w_ref, pn_ref, x_ref, o_ref):
    half = hg_ref.shape[1]
    y = (jnp.dot(hg_ref[...], w_ref[:half, :], preferred_element_type=F32)
         + jnp.dot(og_ref[...], w_ref[half:, :], preferred_element_type=F32))
    o_ref[...] = x_ref[...] + _rms(y, pn_ref[...])


def _out_proj0(hg, og, w_bf16, post_gain, x2d, tm=512):
    t, d = x2d.shape
    tok = lambda: pl.BlockSpec((tm, d), lambda i: (i, 0))
    return pl.pallas_call(
        _out_proj0_kernel,
        grid=(t // tm,),
        in_specs=[tok(), tok(),
                  pl.BlockSpec(w_bf16.shape, lambda i: (0, 0)),
                  pl.BlockSpec((1, d), lambda i: (0, 0)),
                  tok()],
        out_specs=tok(),
        out_shape=jax.ShapeDtypeStruct((t, d), F32),
        compiler_params=_params("parallel"),
        name="l0_out_proj",
    )(hg, og, w_bf16, post_gain, x2d)


def _rwkv_front_kernel(x_ref, pn_ref, mu_ref, win_ref, w0_ref, w1_ref, w2_ref, a0_ref, a1_ref,
                       a2_ref, kk_ref, ka_ref, ones_ref,
                       r_out, lw_out, k_out, v_out, kk_out, b_out, g_out, prev_ref):
    @pl.when(pl.program_id(1) == 0)
    def _():
        prev_ref[...] = jnp.zeros_like(prev_ref)

    u = _rms(x_ref[...], pn_ref[...])
    tm = u.shape[0]
    row = lax.broadcasted_iota(jnp.int32, (tm, 1), 0)
    delta = jnp.where(row == 0, prev_ref[...], pltpu.roll(u, 1, 0)) - u
    prev_ref[...] = u[tm - 1:tm, :]
    mix = lambda p: u + delta * mu_ref[p:p + 1, :]

    r_out[...] = _dot(mix(0), win_ref[0]).astype(r_out.dtype)
    k = _dot(mix(1), win_ref[1])
    v_out[...] = _dot(mix(2), win_ref[2]).astype(v_out.dtype)
    g_out[...] = _dot(mix(3), win_ref[3]).astype(g_out.dtype)
    lora_w = _dot(jnp.tanh(_dot(mix(4), w1_ref[...])), w2_ref[...])
    lora_a = _dot(_dot(mix(5), a1_ref[...]), a2_ref[...])
    lw_out[...] = (-EXP_NEG_HALF) * _sigmoid(w0_ref[...] + lora_w)
    iclr = _sigmoid(a0_ref[...] + lora_a)
    k_out[...] = (k * (1.0 + (iclr - 1.0) * ka_ref[...])).astype(k_out.dtype)
    kk = k * kk_ref[...]
    kk = kk * lax.rsqrt(jnp.maximum(_seg_sum(kk * kk, ones_ref[...]), 1e-24))
    kk_out[...] = kk.astype(kk_out.dtype)
    b_out[...] = (kk * iclr).astype(b_out.dtype)


def _rwkv_front(x1, pre_gain, mu, win, w0, w1, w2, a0, a1, a2, k_k, k_a, ones_bd, tm=512):
    bsz, seq, d = x1.shape
    tok = lambda: pl.BlockSpec((None, tm, d), lambda b, s: (b, s, 0))
    full = lambda a: pl.BlockSpec(a.shape, lambda b, s: (0,) * a.ndim)
    args = (pre_gain, mu, win, w0, w1, w2, a0, a1, a2, k_k, k_a, ones_bd)
    return pl.pallas_call(
        _rwkv_front_kernel,
        grid=(bsz, seq // tm),
        in_specs=[tok()] + [full(a) for a in args],
        out_specs=[tok() for _ in range(7)],
        out_shape=[jax.ShapeDtypeStruct((bsz, seq, d), F32 if i == 1 else BF16) for i in range(7)],
        scratch_shapes=[pltpu.VMEM((1, d), F32)],
        compiler_params=_params("parallel", "arbitrary"),
        name="l1_front",
    )(x1, *args)


def _rwkv_scan_kernel(r_ref, lw_ref, k_ref, v_ref, kk_ref, b_ref, y_ref,
                      ht_ref, rt_s, y0_s, ml_s, ha_s, gm_s):
    @pl.when(pl.program_id(2) == 0)
    def _():
        ht_ref[...] = jnp.zeros_like(ht_ref)

    ts, width = r_ref.shape
    nchunk = ts // CHUNK
    npair = width // LANES
    lanes_of = [slice(p * LANES, (p + 1) * LANES) for p in range(npair)]
    two = 2 * CHUNK
    head0 = lax.broadcasted_iota(jnp.int32, (CHUNK, LANES), 1) < RW_HEAD
    row = lax.broadcasted_iota(jnp.int32, (two, two), 0)
    col = lax.broadcasted_iota(jnp.int32, (two, two), 1)
    strict = (row & (CHUNK - 1)) > (col & (CHUNK - 1))
    incl = (row & (CHUNK - 1)) >= (col & (CHUNK - 1))
    eye = jnp.where(row == col, 1.0, 0.0)

    def stack(x):
        return jnp.concatenate([jnp.where(head0, x, 0.0), jnp.where(head0, 0.0, x)], axis=0)

    def prep(c):
        rows = pl.ds(pl.multiple_of(c * CHUNK, CHUNK), CHUNK)
        r, lw, k, v, kk, b = (ref[rows, :].astype(F32)
                              for ref in (r_ref, lw_ref, k_ref, v_ref, kk_ref, b_ref))
        lcum = _cumsum_rows(lw)
        lend = lcum[CHUNK - 1:CHUNK, :]
        e_neg = jnp.exp(-lcum)
        e_end = jnp.exp(lend - lcum)
        gm_s[c] = jnp.exp(lend)
        at, rt = -kk * jnp.exp(lcum - lw), r * jnp.exp(lcum)
        bt, kt, bh, kh = b * e_neg, k * e_neg, b * e_end, k * e_end
        return [dict(rows=rows, c=c, p=p, sl=sl,
                     at2=stack(at[:, sl]).astype(BF16),
                     rt2=stack(rt[:, sl]),
                     bk2=jnp.concatenate([stack(bt[:, sl]), stack(kt[:, sl])], axis=0).astype(BF16),
                     v2=stack(v[:, sl]).astype(BF16),
                     bkh2=jnp.concatenate([stack(bh[:, sl]), stack(kh[:, sl])], axis=0).astype(BF16))
                for p, sl in enumerate(lanes_of)]

    def local_group(i, between=()):
        pending = list(between)
        passed = [0]

        def boundary():
            fired = len(between) - len(pending)
            if pending and passed[0] * len(between) >= fired * SCAN_STAGES:
                pending.pop(0)()
            passed[0] += 1

        g = [q for j in range(SCAN_GROUP) for q in prep(i * SCAN_GROUP + j)]
        s = [_dot_nt(jnp.concatenate([q["at2"], q["rt2"].astype(BF16)], axis=0), q["bk2"]) for q in g]
        boundary()
        n = [jnp.where(strict, x[:two, :two], 0.0) for x in s]
        a_ak = [jnp.where(strict, x[:two, two:], 0.0).astype(BF16) for x in s]
        a_rbk = [jnp.concatenate([jnp.where(incl, x[two:, :two], 0.0),
                                  jnp.where(incl, x[two:, two:], 0.0)], axis=1).astype(BF16) for x in s]
        t = [eye + x for x in n]
        pw = [_dot(x, x).astype(BF16) for x in n]
        boundary()
        d = 2
        while d < CHUNK // 2:
            m = [_dot(jnp.concatenate([a.astype(BF16), b], axis=0), b) for a, b in zip(t, pw)]
            t = [a + x[:two] for a, x in zip(t, m)]
            pw = [x[two:].astype(BF16) for x in m]
            boundary()
            d *= 2
        t = [a + _dot(a, b) for a, b in zip(t, pw)]
        boundary()
        akv = [_dot(x, q["v2"]) for x, q in zip(a_ak, g)]
        tw = [_dot(x, jnp.concatenate([q["at2"], y.astype(BF16)], axis=1)).astype(BF16)
              for x, y, q in zip(t, akv, g)]
        boundary()
        rw = [_dot(x[:, :two], y[:, :LANES]) for x, y in zip(a_rbk, tw)]
        y0 = [_dot(x, jnp.concatenate([y[:, LANES:], q["v2"]], axis=0)) for x, y, q in zip(a_rbk, tw, g)]
        ml = [_dot_tn(q["bkh2"][:two], y[:, :LANES]) for q, y in zip(g, tw)]
        ha = [_dot_tn(jnp.concatenate([y[:, LANES:], q["v2"]], axis=0), q["bkh2"]) for q, y in zip(g, tw)]
        while pending:
            pending.pop(0)()
        for j, q in enumerate(g):
            rtp = q["rt2"] + rw[j]
            rt_s[q["rows"], q["sl"]] = (rtp[:CHUNK] + rtp[CHUNK:]).astype(BF16)
            y0_s[q["rows"], q["sl"]] = y0[j][:CHUNK] + y0[j][CHUNK:]
            ml_s[q["c"], q["p"]] = ml[j].astype(BF16)
            ha_s[q["c"], q["p"]] = ha[j]

    def state_step(c, hts):
        rows = pl.ds(pl.multiple_of(c * CHUNK, CHUNK), CHUNK)
        rt, gm = rt_s[rows, :], gm_s[c]
        htb = [h.astype(BF16) for h in hts]
        ys = [_dot_nt(rt[:, sl], hb) for sl, hb in zip(lanes_of, htb)]
        new = [h * gm[:, sl] + _dot_nt(hb, ml_s[c, p]) + ha_s[c, p]
               for p, (sl, h, hb) in enumerate(zip(lanes_of, hts, htb))]
        y_ref[rows, :] = (jnp.concatenate(ys, axis=1) + y0_s[rows, :]).astype(y_ref.dtype)
        return tuple(new)

    def pipelined(i, hts):
        box = [hts]

        def advance(j):
            def run():
                box[0] = state_step((i - 1) * SCAN_GROUP + j, box[0])
            return run

        local_group(i, between=[advance(j) for j in range(SCAN_GROUP)])
        return box[0]

    ngroup = nchunk // SCAN_GROUP
    local_group(0)
    hts = lax.fori_loop(1, ngroup, pipelined, tuple(ht_ref[p] for p in range(npair)))
    for j in range(SCAN_GROUP):
        hts = state_step((ngroup - 1) * SCAN_GROUP + j, hts)
    for p in range(npair):
        ht_ref[p] = hts[p]


def _rwkv_scan(r, lw, k, v, kk, b, ts=1024, width=512):
    bsz, seq, d = r.shape
    nchunk = ts // CHUNK
    npair = width // LANES
    spec = lambda: pl.BlockSpec((None, ts, width), lambda bb, h, s: (bb, s, h))
    return pl.pallas_call(
        _rwkv_scan_kernel,
        grid=(bsz, d // width, seq // ts),
        in_specs=[spec() for _ in range(6)],
        out_specs=spec(),
        out_shape=jax.ShapeDtypeStruct((bsz, seq, d), BF16),
        scratch_shapes=[pltpu.VMEM((npair, LANES, LANES), F32),
                        pltpu.VMEM((ts, width), BF16),
                        pltpu.VMEM((ts, width), F32),
                        pltpu.VMEM((nchunk, npair, LANES, LANES), BF16),
                        pltpu.VMEM((nchunk, npair, LANES, LANES), F32),
                        pltpu.VMEM((nchunk, 1, width), F32)],
        compiler_params=_params("parallel", "parallel", "arbitrary"),
        name="l1_scan",
    )(r, lw, k, v, kk, b)


def _rwkv_back_kernel(y_ref, r_ref, k_ref, v_ref, g_ref, x_ref, rk_ref, lnw_ref, lnb_ref,
                      w_ref, pn_ref, ones_ref, o_ref):
    ones = ones_ref[...]
    inv_n = 1.0 / RW_HEAD
    y = y_ref[...].astype(F32)
    yc = y - _seg_sum(y, ones, split=False) * inv_n
    var = _seg_sum(yc * yc, ones, split=False) * inv_n
    yn = yc * lax.rsqrt(var + RW_GN_EPS) * lnw_ref[...] + lnb_ref[...]
    rk = r_ref[...].astype(F32) * k_ref[...].astype(F32) * rk_ref[...]
    yn = yn + _seg_sum(rk, ones, split=False) * v_ref[...].astype(F32)
    out = _dot(yn * _silu(g_ref[...].astype(F32)), w_ref[...])
    o_ref[...] = x_ref[...] + _rms(out, pn_ref[...])


def _rwkv_back(y, r, k, v, g, x1, r_k, ln_w, ln_b, w_bf16, post_gain, ones_bd, tm=512):
    t, d = y.shape
    tok = lambda: pl.BlockSpec((tm, d), lambda i: (i, 0))
    full = lambda a: pl.BlockSpec(a.shape, lambda i: (0,) * a.ndim)
    params = (r_k, ln_w, ln_b, w_bf16, post_gain, ones_bd)
    return pl.pallas_call(
        _rwkv_back_kernel,
        grid=(t // tm,),
        in_specs=[tok() for _ in range(6)] + [full(a) for a in params],
        out_specs=tok(),
        out_shape=jax.ShapeDtypeStruct((t, d), F32),
        compiler_params=_params("parallel"),
        name="l1_back",
    )(y, r, k, v, g, x1, *params)


def _block_diag(w, group):
    n, bi, bj = w.shape
    w = w.reshape(n // group, group, bi, bj)
    eye = jnp.eye(group, dtype=w.dtype)
    return jnp.einsum("gaij,ab->gaibj", w, eye).reshape(n // group, group * bi, group * bj)


def kernel(x, pre_norm, post_norm, ab_w_in, ab_w_out, rg_conv_w, rg_conv_b, rg_w_a, rg_b_a,
           rg_w_x, rg_b_x, rg_lambda, hg_lower_bounds, hg_out_norm, rw_mu, rw_w_in, rw_w_out,
           rw_w0, rw_w1, rw_w2, rw_a0, rw_a1, rw_a2, rw_k_k, rw_k_a, rw_r_k, rw_ln_w, rw_ln_b):
    bsz, seq, d = x.shape
    t = bsz * seq
    row = lambda a: a.reshape(1, -1).astype(F32)
    x2d = x.reshape(t, d)
    heads_per_tile = MXU_DIM // RW_HEAD
    ones_bd = _block_diag(jnp.ones((heads_per_tile, RW_HEAD, RW_HEAD), F32), heads_per_tile)[0].astype(BF16)

    w_in = ab_w_in[0].astype(BF16)
    rg_splits = 2
    proj, u0 = _norm_proj(x2d, row(pre_norm[0]), w_in, rg_splits * d)
    proj4 = proj.reshape(proj.shape[0], bsz, seq, proj.shape[-1])
    rg_group = MXU_DIM // rg_w_a.shape[-1]
    hg = _rglru(proj4, rg_conv_w[0].astype(F32), row(rg_conv_b[0]),
                _block_diag(rg_w_a[0], rg_group).astype(BF16), row(rg_b_a[0]),
                _block_diag(rg_w_x[0], rg_group).astype(BF16), row(rg_b_x[0]), row(rg_lambda[0]))
    w_hg = w_in[:, rg_splits * d:].reshape(d, 4, HG_HEADS, HG_DK).transpose(0, 2, 1, 3).reshape(d, 4 * d)
    og = _hgrn2(u0.reshape(bsz, seq, d), w_hg, hg_lower_bounds.astype(F32), row(hg_out_norm[0]))
    x1 = _out_proj0(hg.reshape(t, d), og.reshape(t, d), ab_w_out[0].astype(BF16),
                    row(post_norm[0]), x2d)

    r, lw, k, v, kk, b, g = _rwkv_front(
        x1.reshape(bsz, seq, d), row(pre_norm[1]), rw_mu[0].astype(F32), rw_w_in[0].astype(BF16),
        row(rw_w0[0]), rw_w1[0].astype(BF16), rw_w2[0].astype(BF16), row(rw_a0[0]),
        rw_a1[0].astype(BF16), rw_a2[0].astype(BF16), row(rw_k_k[0]), row(rw_k_a[0]), ones_bd)
    y = _rwkv_scan(r, lw, k, v, kk, b)
    flat = lambda a: a.reshape(t, d)
    x2 = _rwkv_back(flat(y), flat(r), flat(k), flat(v), flat(g), x1, row(rw_r_k[0]),
                    row(rw_ln_w[0]), row(rw_ln_b[0]), rw_w_out[0].astype(BF16),
                    row(post_norm[1]), ones_bd)
    return x2.reshape(bsz, seq, d)
```

```python
import functools

import jax
import jax.numpy as jnp
from jax import lax
from jax.experimental import pallas as pl
from jax.experimental.pallas import tpu as pltpu

F32 = jnp.float32
BF16 = jnp.bfloat16

D_MODEL = 1024
RMS_EPS = 1e-6
RG_C = 8.0
RG_CONV = 4
HG_HEADS = 8
HG_DK = 128
RW_HEAD = 64
RW_GN_EPS = 64e-5

LANES = 128
SUBLANES = 8
MXU_DIM = 256
VMEM_LIMIT = 56 * 1024 * 1024

CHUNK = 64
SCAN_GROUP = 4
HG_GROUP = 8
SCAN_STAGES = 8
EXP_NEG_HALF = 0.6065306597126334


def _params(*sem):
    return pltpu.CompilerParams(dimension_semantics=sem, vmem_limit_bytes=VMEM_LIMIT)


def _dot(a, b):
    return jnp.dot(a.astype(BF16), b.astype(BF16), preferred_element_type=F32)


def _dot_nt(a, b):
    return lax.dot_general(a.astype(BF16), b.astype(BF16), (((1,), (1,)), ((), ())),
                           preferred_element_type=F32)


def _dot_tn(a, b):
    return lax.dot_general(a.astype(BF16), b.astype(BF16), (((0,), (0,)), ((), ())),
                           preferred_element_type=F32)


def _sigmoid(x):
    return 0.5 * jnp.tanh(0.5 * x) + 0.5


def _silu(x):
    return x * _sigmoid(x)


def _softplus(z):
    return jnp.maximum(z, 0.0) + jnp.log(1.0 + jnp.exp(-jnp.abs(z)))


def _rms(x, gain):
    return x * lax.rsqrt(jnp.mean(x * x, axis=-1, keepdims=True) + RMS_EPS) * gain


def _cumsum_rows(x):
    n = x.shape[0]
    row = lax.broadcasted_iota(jnp.int32, x.shape, 0)
    d = 1
    while d < n:
        x = x + jnp.where(row >= d, pltpu.roll(x, d, 0), 0.0)
        d *= 2
    return x


def _seg_sum(x, ones_bd, split=True):
    hi = x.astype(BF16)
    lo = (x - hi.astype(F32)).astype(BF16) if split else None
    outs = []
    for g in range(x.shape[1] // MXU_DIM):
        sl = slice(g * MXU_DIM, (g + 1) * MXU_DIM)
        acc = jnp.dot(hi[:, sl], ones_bd, preferred_element_type=F32)
        if split:
            acc = acc + jnp.dot(lo[:, sl], ones_bd, preferred_element_type=F32)
        outs.append(acc)
    return jnp.concatenate(outs, axis=-1)


def _norm_proj_kernel(x_ref, g_ref, w_ref, o_ref, u_ref):
    @pl.when(pl.program_id(1) == 0)
    def _():
        u_ref[...] = _rms(x_ref[...], g_ref[...]).astype(BF16)

    o_ref[...] = jnp.dot(u_ref[...], w_ref[...], preferred_element_type=F32).astype(o_ref.dtype)


def _norm_proj(x2d, gain, w_bf16, n, tm=1024, tn=2048):
    t, d = x2d.shape
    return pl.pallas_call(
        _norm_proj_kernel,
        grid=(t // tm, n // tn),
        in_specs=[pl.BlockSpec((tm, d), lambda i, j: (i, 0)),
                  pl.BlockSpec((1, d), lambda i, j: (0, 0)),
                  pl.BlockSpec((d, tn), lambda i, j: (0, j))],
        out_specs=[pl.BlockSpec((None, tm, tn), lambda i, j: (j, i, 0)),
                   pl.BlockSpec((tm, d), lambda i, j: (i, 0))],
        out_shape=[jax.ShapeDtypeStruct((n // tn, t, tn), BF16),
                   jax.ShapeDtypeStruct((t, d), BF16)],
        compiler_params=_params("parallel", "arbitrary"),
        name="l0_norm_proj",
    )(x2d, gain, w_bf16)


def _rglru_kernel(x_ref, g_ref, cw_ref, cb_ref, wa_ref, ba_ref, wx_ref, bx_ref, lam_ref,
                  o_ref, h_ref, ext_ref, a_ref, b_ref):
    @pl.when(pl.program_id(1) == 0)
    def _():
        h_ref[...] = jnp.zeros_like(h_ref)
        ext_ref[:SUBLANES, :] = jnp.zeros((SUBLANES, ext_ref.shape[1]), F32)

    x = x_ref[...].astype(F32)
    ts, width = x.shape
    ext_ref[SUBLANES:, :] = x
    xc = cb_ref[...] + cw_ref[RG_CONV - 1:RG_CONV, :] * x
    for k in range(1, RG_CONV):
        xc = xc + cw_ref[RG_CONV - 1 - k:RG_CONV - k, :] * ext_ref[SUBLANES - k:SUBLANES - k + ts, :]
    ext_ref[:SUBLANES, :] = x[ts - SUBLANES:, :]

    xcb = xc.astype(BF16)
    pre_r, pre_i = [], []
    for g in range(width // MXU_DIM):
        sl = slice(g * MXU_DIM, (g + 1) * MXU_DIM)
        pre_r.append(jnp.dot(xcb[:, sl], wa_ref[g], preferred_element_type=F32))
        pre_i.append(jnp.dot(xcb[:, sl], wx_ref[g], preferred_element_type=F32))
    gate_r = _sigmoid(jnp.concatenate(pre_r, axis=-1) + ba_ref[...])
    gate_i = _sigmoid(jnp.concatenate(pre_i, axis=-1) + bx_ref[...])

    log_a = (-RG_C) * gate_r * _softplus(-lam_ref[...])
    a = jnp.exp(log_a)
    b = jnp.sqrt(1.0 - a * a) * (gate_i * xc)

    a = a.reshape(ts // SUBLANES, SUBLANES, width)
    b = b.reshape(ts // SUBLANES, SUBLANES, width)
    in_group = lax.broadcasted_iota(jnp.int32, (1, SUBLANES, 1), 1)
    d = 1
    while d < SUBLANES:
        m = in_group >= d
        b = jnp.where(m, a, 0.0) * pltpu.roll(b, d, 1) + b
        a = a * jnp.where(m, pltpu.roll(a, d, 1), 1.0)
        d *= 2
    a_ref[...] = a.reshape(ts, width)
    b_ref[...] = b.reshape(ts, width)

    def carry_group(i, h):
        rows = pl.ds(pl.multiple_of(i * SUBLANES, SUBLANES), SUBLANES)
        hg = a_ref[rows, :] * h + b_ref[rows, :]
        b_ref[rows, :] = hg
        return hg[SUBLANES - 1:, :]

    h_ref[...] = lax.fori_loop(0, ts // SUBLANES, carry_group, h_ref[...], unroll=8)
    o_ref[...] = (b_ref[...] * _silu(g_ref[...].astype(F32))).astype(o_ref.dtype)


def _proj_block(proj4, split, block, offset=0):
    per_slab = proj4.shape[-1] // block
    col = split * (D_MODEL // block) + offset
    return col // per_slab, col % per_slab


def _rglru(proj4, conv_w, conv_b, wa_bd, b_a, wx_bd, b_x, lam, ts=256):
    _, bsz, seq, _ = proj4.shape
    width = D_MODEL
    row = lambda: pl.BlockSpec((1, width), lambda b, s: (0, 0))

    def split_spec(split):
        slab, blk = _proj_block(proj4, split, width)
        return pl.BlockSpec((None, None, ts, width), lambda b, s: (slab, b, s, blk))

    return pl.pallas_call(
        _rglru_kernel,
        grid=(bsz, seq // ts),
        in_specs=[split_spec(0), split_spec(1),
                  pl.BlockSpec((RG_CONV, width), lambda b, s: (0, 0)),
                  row(),
                  pl.BlockSpec(wa_bd.shape, lambda b, s: (0, 0, 0)),
                  row(),
                  pl.BlockSpec(wx_bd.shape, lambda b, s: (0, 0, 0)),
                  row(), row()],
        out_specs=pl.BlockSpec((None, ts, width), lambda b, s: (b, s, 0)),
        out_shape=jax.ShapeDtypeStruct((bsz, seq, width), BF16),
        scratch_shapes=[pltpu.VMEM((1, width), F32), pltpu.VMEM((SUBLANES + ts, width), F32),
                        pltpu.VMEM((ts, width), F32), pltpu.VMEM((ts, width), F32)],
        compiler_params=_params("parallel", "arbitrary"),
        name="l0_rglru",
    )(proj4, proj4, conv_w, conv_b, wa_bd, b_a, wx_bd, b_x, lam)


def _hgrn2_kernel(u_ref, wq_ref, wf_ref, wv_ref, wg_ref, lbp_ref, gn_ref, o_ref, st_ref, p_ref):
    @pl.when(pl.program_id(2) == 0)
    def _():
        st_ref[...] = jnp.zeros_like(st_ref)

    group_rows = CHUNK * HG_GROUP
    col = lambda j: slice(j * HG_DK, (j + 1) * HG_DK)
    w_halves = (jnp.concatenate([wq_ref[...], wf_ref[...]], axis=1),
                jnp.concatenate([wv_ref[...], wg_ref[...]], axis=1))

    def project(g, slot, half):
        rows = pl.ds(pl.multiple_of(g * group_rows, group_rows), group_rows)
        cols = slice(half * 2 * HG_DK, (half + 1) * 2 * HG_DK)
        p_ref[slot, :, cols] = jnp.dot(u_ref[rows, :], w_halves[half], preferred_element_type=F32)

    lbp = lbp_ref[...]
    e = jnp.exp(lbp - jnp.max(lbp, axis=0, keepdims=True))
    p = e / jnp.sum(e, axis=0, keepdims=True)
    lb = (p[0:1] + p[1:2]) - p[0:1]
    gn = gn_ref[...]
    ngroup = u_ref.shape[0] // group_rows
    srow = lax.broadcasted_iota(jnp.int32, (CHUNK, CHUNK), 0)
    scol = lax.broadcasted_iota(jnp.int32, (CHUNK, CHUNK), 1)
    levels = []
    h = CHUNK // 2
    while h >= 1:
        levels.append((h, (srow & ~(2 * h - 1)) == (scol & ~(2 * h - 1))))
        h //= 2
    sub_row = lax.broadcasted_iota(jnp.int32, (SUBLANES, 1), 0)
    chunk_row = lax.broadcasted_iota(jnp.int32, (CHUNK, 1), 0)
    tril3 = jnp.where(srow >= scol, 1.0, 0.0).astype(BF16)
    tril3 = jnp.concatenate([tril3, tril3, tril3], axis=1)

    def cumsum_rows(x):
        hi = x.astype(BF16)
        r1 = x - hi.astype(F32)
        mid = r1.astype(BF16)
        lo = (r1 - mid.astype(F32)).astype(BF16)
        return jnp.dot(tril3, jnp.concatenate([hi, mid, lo], axis=0), preferred_element_type=F32)

    def block_rows(x, size, r):
        out = []
        for lo in range(0, CHUNK, SUBLANES):
            acc = jnp.broadcast_to(x[lo + r:lo + r + 1], (SUBLANES, LANES))
            for off in range(size, SUBLANES, size):
                acc = jnp.where(sub_row >= off,
                                jnp.broadcast_to(x[lo + off + r:lo + off + r + 1], (SUBLANES, LANES)), acc)
            out.append(acc)
        return jnp.concatenate(out, axis=0)

    def level_factors(q, k, f, g2, h):
        if h < SUBLANES:
            upper = (chunk_row & (2 * h - 1)) >= h
            if h == 1:
                return jnp.where(upper, q * f, 0.0), jnp.where(upper, 0.0, k)
            d = g2 - block_rows(g2, 2 * h, h - 1)
            e = jnp.exp2(jnp.where(upper, d, -d))
            return jnp.where(upper, q * e, 0.0), jnp.where(upper, 0.0, k * e)
        zero = jnp.zeros((h, LANES), F32)
        qt, kt = [], []
        for lo in range(0, CHUNK, 2 * h):
            mid = lo + h
            rho = g2[mid - 1:mid, :]
            kt += [k[lo:mid] * jnp.exp2(rho - g2[lo:mid]), zero]
            qt += [zero, q[mid:mid + h] * jnp.exp2(g2[mid:mid + h] - rho)]
        return jnp.concatenate(qt, axis=0), jnp.concatenate(kt, axis=0)

    def group(i, carry):
        idx = range(HG_GROUP)
        slot = i % 2
        nxt = jnp.minimum(i + 1, ngroup - 1)
        local = [pl.ds(j * CHUNK, CHUNK) for j in idx]
        rows = [pl.ds(pl.multiple_of((i * HG_GROUP + j) * CHUNK, CHUNK), CHUNK) for j in idx]
        q = [_silu(p_ref[slot, r, col(0)]) for r in local]
        sig = [_sigmoid(p_ref[slot, r, col(1)]) for r in local]
        v = [p_ref[slot, r, col(2)].astype(BF16) for r in local]
        gate = [p_ref[slot, r, col(3)] for r in local]
        k = [(1.0 - lb) * (1.0 - x) for x in sig]
        f = [lb + (1.0 - lb) * x for x in sig]
        g2 = [cumsum_rows(jnp.log2(x)) for x in f]
        project(nxt, 1 - slot, 0)
        gend = [x[CHUNK - 1:CHUNK, :] for x in g2]
        qg = [(a * jnp.exp2(b)).astype(BF16) for a, b in zip(q, g2)]
        vtk = [_dot_tn(a, b * jnp.exp2(e - c)) for a, b, c, e in zip(v, k, g2, gend)]

        scores = [jnp.where(srow == scol, jnp.sum(a * b, axis=-1, keepdims=True), 0.0)
                  for a, b in zip(q, k)]
        for h, same_block in levels:
            for j in idx:
                qt, kt = level_factors(q[j], k[j], f[j], g2[j], h)
                scores[j] = scores[j] + jnp.where(same_block, _dot_nt(qt, kt), 0.0)
        out = [_dot(a, b) for a, b in zip(scores, v)]

        st = st_ref[...]
        for j in idx:
            out[j] = out[j] + _dot_nt(qg[j], st)
            st = st * jnp.exp2(gend[j]) + vtk[j]
        st_ref[...] = st
        project(nxt, 1 - slot, 1)
        for j in idx:
            o_ref[rows[j], :] = (_rms(out[j], gn) * _silu(gate[j])).astype(o_ref.dtype)
        return carry

    project(0, 0, 0)
    project(0, 0, 1)
    lax.fori_loop(0, ngroup, group, 0)


def _hgrn2(u3, w_bf16, first_split, lb_params, g_norm, ts=8192):
    bsz, seq, d = u3.shape

    def w_spec(split):
        base = split * (d // HG_DK)
        return pl.BlockSpec((d, HG_DK), lambda b, h, s: (0, base + h))

    return pl.pallas_call(
        _hgrn2_kernel,
        grid=(bsz, HG_HEADS, seq // ts),
        in_specs=[pl.BlockSpec((None, ts, d), lambda b, h, s: (b, s, 0))]
        + [w_spec(first_split + j) for j in range(4)]
        + [pl.BlockSpec((lb_params.shape[0], HG_DK), lambda b, h, s: (0, h)),
           pl.BlockSpec((1, HG_DK), lambda b, h, s: (0, 0))],
        out_specs=pl.BlockSpec((None, ts, HG_DK), lambda b, h, s: (b, s, h)),
        out_shape=jax.ShapeDtypeStruct((bsz, seq, d), BF16),
        scratch_shapes=[pltpu.VMEM((HG_DK, HG_DK), F32),
                        pltpu.VMEM((2, CHUNK * HG_GROUP, 4 * HG_DK), F32)],
        compiler_params=_params("parallel", "parallel", "arbitrary"),
        name="l0_hgrn2",
    )(u3, w_bf16, w_bf16, w_bf16, w_bf16, lb_params, g_norm)


def _out_proj0_kernel(hg_ref, og_ref, w_ref, pn_ref, x_ref, o_ref):
    half = hg_ref.shape[1]
    y = (jnp.dot(hg_ref[...], w_ref[:half, :], preferred_element_type=F32)
         + jnp.dot(og_ref[...], w_ref[half:, :], preferred_element_type=F32))
    o_ref[...] = x_ref[...] + _rms(y, pn_ref[...])


def _out_proj0(hg, og, w_bf16, post_gain, x2d, tm=512):
    t, d = x2d.shape
    tok = lambda: pl.BlockSpec((tm, d), lambda i: (i, 0))
    return pl.pallas_call(
        _out_proj0_kernel,
        grid=(t // tm,),
        in_specs=[tok(), tok(),
                  pl.BlockSpec(w_bf16.shape, lambda i: (0, 0)),
                  pl.BlockSpec((1, d), lambda i: (0, 0)),
                  tok()],
        out_specs=tok(),
        out_shape=jax.ShapeDtypeStruct((t, d), F32),
        compiler_params=_params("parallel"),
        name="l0_out_proj",
    )(hg, og, w_bf16, post_gain, x2d)


def _rwkv_front_kernel(x_ref, pn_ref, mu_ref, win_ref, w0_ref, w1_ref, w2_ref, a0_ref, a1_ref,
                       a2_ref, kk_ref, ka_ref, ones_ref,
                       r_out, lw_out, k_out, v_out, kk_out, b_out, g_out, prev_ref):
    @pl.when(pl.program_id(1) == 0)
    def _():
        prev_ref[...] = jnp.zeros_like(prev_ref)

    u = _rms(x_ref[...], pn_ref[...])
    tm = u.shape[0]
    row = lax.broadcasted_iota(jnp.int32, (tm, 1), 0)
    delta = jnp.where(row == 0, prev_ref[...], pltpu.roll(u, 1, 0)) - u
    prev_ref[...] = u[tm - 1:tm, :]
    mix = lambda p: u + delta * mu_ref[p:p + 1, :]

    r_out[...] = _dot(mix(0), win_ref[0]).astype(r_out.dtype)
    k = _dot(mix(1), win_ref[1])
    v_out[...] = _dot(mix(2), win_ref[2]).astype(v_out.dtype)
    g_out[...] = _dot(mix(3), win_ref[3]).astype(g_out.dtype)
    lora_w = _dot(jnp.tanh(_dot(mix(4), w1_ref[...])), w2_ref[...])
    lora_a = _dot(_dot(mix(5), a1_ref[...]), a2_ref[...])
    lw_out[...] = (-EXP_NEG_HALF) * _sigmoid(w0_ref[...] + lora_w)
    iclr = _sigmoid(a0_ref[...] + lora_a)
    k_out[...] = (k * (1.0 + (iclr - 1.0) * ka_ref[...])).astype(k_out.dtype)
    kk = k * kk_ref[...]
    kk = kk * lax.rsqrt(jnp.maximum(_seg_sum(kk * kk, ones_ref[...]), 1e-24))
    kk_out[...] = kk.astype(kk_out.dtype)
    b_out[...] = (kk * iclr).astype(b_out.dtype)


def _rwkv_front(x1, pre_gain, mu, win, w0, w1, w2, a0, a1, a2, k_k, k_a, ones_bd, tm=512):
    bsz, seq, d = x1.shape
    tok = lambda: pl.BlockSpec((None, tm, d), lambda b, s: (b, s, 0))
    full = lambda a: pl.BlockSpec(a.shape, lambda b, s: (0,) * a.ndim)
    args = (pre_gain, mu, win, w0, w1, w2, a0, a1, a2, k_k, k_a, ones_bd)
    return pl.pallas_call(
        _rwkv_front_kernel,
        grid=(bsz, seq // tm),
        in_specs=[tok()] + [full(a) for a in args],
        out_specs=[tok() for _ in range(7)],
        out_shape=[jax.ShapeDtypeStruct((bsz, seq, d), F32 if i == 1 else BF16) for i in range(7)],
        scratch_shapes=[pltpu.VMEM((1, d), F32)],
        compiler_params=_params("parallel", "arbitrary"),
        name="l1_front",
    )(x1, *args)


def _rwkv_scan_kernel(r_ref, lw_ref, k_ref, v_ref, kk_ref, b_ref, y_ref,
                      ht_ref, rt_s, y0_s, ml_s, ha_s, gm_s):
    @pl.when(pl.program_id(2) == 0)
    def _():
        ht_ref[...] = jnp.zeros_like(ht_ref)

    ts, width = r_ref.shape
    nchunk = ts // CHUNK
    npair = width // LANES
    lanes_of = [slice(p * LANES, (p + 1) * LANES) for p in range(npair)]
    two = 2 * CHUNK
    head0 = lax.broadcasted_iota(jnp.int32, (CHUNK, LANES), 1) < RW_HEAD
    row = lax.broadcasted_iota(jnp.int32, (two, two), 0)
    col = lax.broadcasted_iota(jnp.int32, (two, two), 1)
    strict = (row & (CHUNK - 1)) > (col & (CHUNK - 1))
    incl = (row & (CHUNK - 1)) >= (col & (CHUNK - 1))
    eye = jnp.where(row == col, 1.0, 0.0)

    def stack(x):
        return jnp.concatenate([jnp.where(head0, x, 0.0), jnp.where(head0, 0.0, x)], axis=0)

    def prep(c):
        rows = pl.ds(pl.multiple_of(c * CHUNK, CHUNK), CHUNK)
        r, lw, k, v, kk, b = (ref[rows, :].astype(F32)
                              for ref in (r_ref, lw_ref, k_ref, v_ref, kk_ref, b_ref))
        lcum = _cumsum_rows(lw)
        lend = lcum[CHUNK - 1:CHUNK, :]
        e_neg = jnp.exp(-lcum)
        e_end = jnp.exp(lend - lcum)
        gm_s[c] = jnp.exp(lend)
        at, rt = -kk * jnp.exp(lcum - lw), r * jnp.exp(lcum)
        bt, kt, bh, kh = b * e_neg, k * e_neg, b * e_end, k * e_end
        return [dict(rows=rows, c=c, p=p, sl=sl,
                     at2=stack(at[:, sl]).astype(BF16),
                     rt2=stack(rt[:, sl]),
                     bk2=jnp.concatenate([stack(bt[:, sl]), stack(kt[:, sl])], axis=0).astype(BF16),
                     v2=stack(v[:, sl]).astype(BF16),
                     bkh2=jnp.concatenate([stack(bh[:, sl]), stack(kh[:, sl])], axis=0).astype(BF16))
                for p, sl in enumerate(lanes_of)]

    def local_group(i, between=()):
        pending = list(between)
        passed = [0]

        def boundary():
            fired = len(between) - len(pending)
            if pending and passed[0] * len(between) >= fired * SCAN_STAGES:
                pending.pop(0)()
            passed[0] += 1

        g = [q for j in range(SCAN_GROUP) for q in prep(i * SCAN_GROUP + j)]
        s = [_dot_nt(jnp.concatenate([q["at2"], q["rt2"].astype(BF16)], axis=0), q["bk2"]) for q in g]
        boundary()
        n = [jnp.where(strict, x[:two, :two], 0.0) for x in s]
        a_ak = [jnp.where(strict, x[:two, two:], 0.0).astype(BF16) for x in s]
        a_rbk = [jnp.concatenate([jnp.where(incl, x[two:, :two], 0.0),
                                  jnp.where(incl, x[two:, two:], 0.0)], axis=1).astype(BF16) for x in s]
        t = [eye + x for x in n]
        pw = [_dot(x, x).astype(BF16) for x in n]
        boundary()
        d = 2
        while d < CHUNK // 2:
            m = [_dot(jnp.concatenate([a.astype(BF16), b], axis=0), b) for a, b in zip(t, pw)]
            t = [a + x[:two] for a, x in zip(t, m)]
            pw = [x[two:].astype(BF16) for x in m]
            boundary()
            d *= 2
        t = [a + _dot(a, b) for a, b in zip(t, pw)]
        boundary()
        akv = [_dot(x, q["v2"]) for x, q in zip(a_ak, g)]
        tw = [_dot(x, jnp.concatenate([q["at2"], y.astype(BF16)], axis=1)).astype(BF16)
              for x, y, q in zip(t, akv, g)]
        boundary()
        rw = [_dot(x[:, :two], y[:, :LANES]) for x, y in zip(a_rbk, tw)]
        y0 = [_dot(x, jnp.concatenate([y[:, LANES:], q["v2"]], axis=0)) for x, y, q in zip(a_rbk, tw, g)]
        ml = [_dot_tn(q["bkh2"][:two], y[:, :LANES]) for q, y in zip(g, tw)]
        ha = [_dot_tn(jnp.concatenate([y[:, LANES:], q["v2"]], axis=0), q["bkh2"]) for q, y in zip(g, tw)]
        while pending:
            pending.pop(0)()
        for j, q in enumerate(g):
            rtp = q["rt2"] + rw[j]
            rt_s[q["rows"], q["sl"]] = (rtp[:CHUNK] + rtp[CHUNK:]).astype(BF16)
            y0_s[q["rows"], q["sl"]] = y0[j][:CHUNK] + y0[j][CHUNK:]
            ml_s[q["c"], q["p"]] = ml[j].astype(BF16)
            ha_s[q["c"], q["p"]] = ha[j]

    def state_step(c, hts):
        rows = pl.ds(pl.multiple_of(c * CHUNK, CHUNK), CHUNK)
        rt, gm = rt_s[rows, :], gm_s[c]
        htb = [h.astype(BF16) for h in hts]
        ys = [_dot_nt(rt[:, sl], hb) for sl, hb in zip(lanes_of, htb)]
        new = [h * gm[:, sl] + _dot_nt(hb, ml_s[c, p]) + ha_s[c, p]
               for p, (sl, h, hb) in enumerate(zip(lanes_of, hts, htb))]
        y_ref[rows, :] = (jnp.concatenate(ys, axis=1) + y0_s[rows, :]).astype(y_ref.dtype)
        return tuple(new)

    def pipelined(i, hts):
        box = [hts]

        def advance(j):
            def run():
                box[0] = state_step((i - 1) * SCAN_GROUP + j, box[0])
            return run

        local_group(i, between=[advance(j) for j in range(SCAN_GROUP)])
        return box[0]

    ngroup = nchunk // SCAN_GROUP
    local_group(0)
    hts = lax.fori_loop(1, ngroup, pipelined, tuple(ht_ref[p] for p in range(npair)))
    for j in range(SCAN_GROUP):
        hts = state_step((ngroup - 1) * SCAN_GROUP + j, hts)
    for p in range(npair):
        ht_ref[p] = hts[p]


def _rwkv_scan(r, lw, k, v, kk, b, ts=1024, width=512):
    bsz, seq, d = r.shape
    nchunk = ts // CHUNK
    npair = width // LANES
    spec = lambda: pl.BlockSpec((None, ts, width), lambda bb, h, s: (bb, s, h))
    return pl.pallas_call(
        _rwkv_scan_kernel,
        grid=(bsz, d // width, seq // ts),
        in_specs=[spec() for _ in range(6)],
        out_specs=spec(),
        out_shape=jax.ShapeDtypeStruct((bsz, seq, d), BF16),
        scratch_shapes=[pltpu.VMEM((npair, LANES, LANES), F32),
                        pltpu.VMEM((ts, width), BF16),
                        pltpu.VMEM((ts, width), F32),
                        pltpu.VMEM((nchunk, npair, LANES, LANES), BF16),
                        pltpu.VMEM((nchunk, npair, LANES, LANES), F32),
                        pltpu.VMEM((nchunk, 1, width), F32)],
        compiler_params=_params("parallel", "parallel", "arbitrary"),
        name="l1_scan",
    )(r, lw, k, v, kk, b)


def _rwkv_back_kernel(y_ref, r_ref, k_ref, v_ref, g_ref, x_ref, rk_ref, lnw_ref, lnb_ref,
                      w_ref, pn_ref, ones_ref, o_ref):
    ones = ones_ref[...]
    inv_n = 1.0 / RW_HEAD
    y = y_ref[...].astype(F32)
    yc = y - _seg_sum(y, ones, split=False) * inv_n
    var = _seg_sum(yc * yc, ones, split=False) * inv_n
    yn = yc * lax.rsqrt(var + RW_GN_EPS) * lnw_ref[...] + lnb_ref[...]
    rk = r_ref[...].astype(F32) * k_ref[...].astype(F32) * rk_ref[...]
    yn = yn + _seg_sum(rk, ones, split=False) * v_ref[...].astype(F32)
    out = _dot(yn * _silu(g_ref[...].astype(F32)), w_ref[...])
    o_ref[...] = x_ref[...] + _rms(out, pn_ref[...])


def _rwkv_back(y, r, k, v, g, x1, r_k, ln_w, ln_b, w_bf16, post_gain, ones_bd, tm=512):
    t, d = y.shape
    tok = lambda: pl.BlockSpec((tm, d), lambda i: (i, 0))
    full = lambda a: pl.BlockSpec(a.shape, lambda i: (0,) * a.ndim)
    params = (r_k, ln_w, ln_b, w_bf16, post_gain, ones_bd)
    return pl.pallas_call(
        _rwkv_back_kernel,
        grid=(t // tm,),
        in_specs=[tok() for _ in range(6)] + [full(a) for a in params],
        out_specs=tok(),
        out_shape=jax.ShapeDtypeStruct((t, d), F32),
        compiler_params=_params("parallel"),
        name="l1_back",
    )(y, r, k, v, g, x1, *params)


def _block_diag(w, group):
    n, bi, bj = w.shape
    w = w.reshape(n // group, group, bi, bj)
    eye = jnp.eye(group, dtype=w.dtype)
    return jnp.einsum("gaij,ab->gaibj", w, eye).reshape(n // group, group * bi, group * bj)


def kernel(x, pre_norm, post_norm, ab_w_in, ab_w_out, rg_conv_w, rg_conv_b, rg_w_a, rg_b_a,
           rg_w_x, rg_b_x, rg_lambda, hg_lower_bounds, hg_out_norm, rw_mu, rw_w_in, rw_w_out,
           rw_w0, rw_w1, rw_w2, rw_a0, rw_a1, rw_a2, rw_k_k, rw_k_a, rw_r_k, rw_ln_w, rw_ln_b):
    bsz, seq, d = x.shape
    t = bsz * seq
    row = lambda a: a.reshape(1, -1).astype(F32)
    x2d = x.reshape(t, d)
    heads_per_tile = MXU_DIM // RW_HEAD
    ones_bd = _block_diag(jnp.ones((heads_per_tile, RW_HEAD, RW_HEAD), F32), heads_per_tile)[0].astype(BF16)

    w_in = ab_w_in[0].astype(BF16)
    rg_splits = 2
    proj, u0 = _norm_proj(x2d, row(pre_norm[0]), w_in, rg_splits * d)
    proj4 = proj.reshape(proj.shape[0], bsz, seq, proj.shape[-1])
    rg_group = MXU_DIM // rg_w_a.shape[-1]
    hg = _rglru(proj4, rg_conv_w[0].astype(F32), row(rg_conv_b[0]),
                _block_diag(rg_w_a[0], rg_group).astype(BF16), row(rg_b_a[0]),
                _block_diag(rg_w_x[0], rg_group).astype(BF16), row(rg_b_x[0]), row(rg_lambda[0]))
    og = _hgrn2(u0.reshape(bsz, seq, d), w_in, rg_splits, hg_lower_bounds.astype(F32),
                row(hg_out_norm[0]))
    x1 = _out_proj0(hg.reshape(t, d), og.reshape(t, d), ab_w_out[0].astype(BF16),
                    row(post_norm[0]), x2d)

    r, lw, k, v, kk, b, g = _rwkv_front(
        x1.reshape(bsz, seq, d), row(pre_norm[1]), rw_mu[0].astype(F32), rw_w_in[0].astype(BF16),
        row(rw_w0[0]), rw_w1[0].astype(BF16), rw_w2[0].astype(BF16), row(rw_a0[0]),
        rw_a1[0].astype(BF16), rw_a2[0].astype(BF16), row(rw_k_k[0]), row(rw_k_a[0]), ones_bd)
    y = _rwkv_scan(r, lw, k, v, kk, b)
    flat = lambda a: a.reshape(t, d)
    x2 = _rwkv_back(flat(y), flat(r), flat(k), flat(v), flat(g), x1, row(rw_r_k[0]),
                    row(rw_ln_w[0]), row(rw_ln_b[0]), rw_w_out[0].astype(BF16),
                    row(post_norm[1]), ones_bd)
    return x2.reshape(bsz, seq, d)
```

```python
import functools

import jax
import jax.numpy as jnp
from jax import lax
from jax.experimental import pallas as pl
from jax.experimental.pallas import tpu as pltpu

F32 = jnp.float32
BF16 = jnp.bfloat16

D_MODEL = 1024
RMS_EPS = 1e-6
RG_C = 8.0
RG_CONV = 4
HG_HEADS = 8
HG_DK = 128
RW_HEAD = 64
RW_GN_EPS = 64e-5

LANES = 128
SUBLANES = 8
MXU_DIM = 256
VMEM_LIMIT = 56 * 1024 * 1024

CHUNK = 64
SCAN_GROUP = 4
HG_GROUP = 16
SCAN_STAGES = 8
EXP_NEG_HALF = 0.6065306597126334


def _params(*sem):
    return pltpu.CompilerParams(dimension_semantics=sem, vmem_limit_bytes=VMEM_LIMIT)


def _dot(a, b):
    return jnp.dot(a.astype(BF16), b.astype(BF16), preferred_element_type=F32)


def _dot_nt(a, b):
    return lax.dot_general(a.astype(BF16), b.astype(BF16), (((1,), (1,)), ((), ())),
                           preferred_element_type=F32)


def _dot_tn(a, b):
    return lax.dot_general(a.astype(BF16), b.astype(BF16), (((0,), (0,)), ((), ())),
                           preferred_element_type=F32)


def _sigmoid(x):
    return 0.5 * jnp.tanh(0.5 * x) + 0.5


def _silu(x):
    return x * _sigmoid(x)


def _softplus(z):
    return jnp.maximum(z, 0.0) + jnp.log(1.0 + jnp.exp(-jnp.abs(z)))


def _rms(x, gain):
    return x * lax.rsqrt(jnp.mean(x * x, axis=-1, keepdims=True) + RMS_EPS) * gain


def _cumsum_rows(x):
    n = x.shape[0]
    row = lax.broadcasted_iota(jnp.int32, x.shape, 0)
    d = 1
    while d < n:
        x = x + jnp.where(row >= d, pltpu.roll(x, d, 0), 0.0)
        d *= 2
    return x


def _seg_sum(x, ones_bd, split=True):
    hi = x.astype(BF16)
    lo = (x - hi.astype(F32)).astype(BF16) if split else None
    outs = []
    for g in range(x.shape[1] // MXU_DIM):
        sl = slice(g * MXU_DIM, (g + 1) * MXU_DIM)
        acc = jnp.dot(hi[:, sl], ones_bd, preferred_element_type=F32)
        if split:
            acc = acc + jnp.dot(lo[:, sl], ones_bd, preferred_element_type=F32)
        outs.append(acc)
    return jnp.concatenate(outs, axis=-1)


def _norm_proj_kernel(x_ref, g_ref, w_ref, o_ref, u_ref):
    @pl.when(pl.program_id(1) == 0)
    def _():
        u_ref[...] = _rms(x_ref[...], g_ref[...]).astype(BF16)

    o_ref[...] = jnp.dot(u_ref[...], w_ref[...], preferred_element_type=F32).astype(o_ref.dtype)


def _norm_proj(x2d, gain, w_bf16, n, tm=1024, tn=2048):
    t, d = x2d.shape
    return pl.pallas_call(
        _norm_proj_kernel,
        grid=(t // tm, n // tn),
        in_specs=[pl.BlockSpec((tm, d), lambda i, j: (i, 0)),
                  pl.BlockSpec((1, d), lambda i, j: (0, 0)),
                  pl.BlockSpec((d, tn), lambda i, j: (0, j))],
        out_specs=[pl.BlockSpec((None, tm, tn), lambda i, j: (j, i, 0)),
                   pl.BlockSpec((tm, d), lambda i, j: (i, 0))],
        out_shape=[jax.ShapeDtypeStruct((n // tn, t, tn), BF16),
                   jax.ShapeDtypeStruct((t, d), BF16)],
        compiler_params=_params("parallel", "arbitrary"),
        name="l0_norm_proj",
    )(x2d, gain, w_bf16)


def _rglru_kernel(x_ref, g_ref, cw_ref, cb_ref, wa_ref, ba_ref, wx_ref, bx_ref, lam_ref,
                  o_ref, h_ref, ext_ref, a_ref, b_ref):
    @pl.when(pl.program_id(1) == 0)
    def _():
        h_ref[...] = jnp.zeros_like(h_ref)
        ext_ref[:SUBLANES, :] = jnp.zeros((SUBLANES, ext_ref.shape[1]), F32)

    x = x_ref[...].astype(F32)
    ts, width = x.shape
    ext_ref[SUBLANES:, :] = x
    xc = cb_ref[...] + cw_ref[RG_CONV - 1:RG_CONV, :] * x
    for k in range(1, RG_CONV):
        xc = xc + cw_ref[RG_CONV - 1 - k:RG_CONV - k, :] * ext_ref[SUBLANES - k:SUBLANES - k + ts, :]
    ext_ref[:SUBLANES, :] = x[ts - SUBLANES:, :]

    xcb = xc.astype(BF16)
    pre_r, pre_i = [], []
    for g in range(width // MXU_DIM):
        sl = slice(g * MXU_DIM, (g + 1) * MXU_DIM)
        pre_r.append(jnp.dot(xcb[:, sl], wa_ref[g], preferred_element_type=F32))
        pre_i.append(jnp.dot(xcb[:, sl], wx_ref[g], preferred_element_type=F32))
    gate_r = _sigmoid(jnp.concatenate(pre_r, axis=-1) + ba_ref[...])
    gate_i = _sigmoid(jnp.concatenate(pre_i, axis=-1) + bx_ref[...])

    log_a = (-RG_C) * gate_r * _softplus(-lam_ref[...])
    a = jnp.exp(log_a)
    b = jnp.sqrt(1.0 - a * a) * (gate_i * xc)

    a = a.reshape(ts // SUBLANES, SUBLANES, width)
    b = b.reshape(ts // SUBLANES, SUBLANES, width)
    in_group = lax.broadcasted_iota(jnp.int32, (1, SUBLANES, 1), 1)
    d = 1
    while d < SUBLANES:
        m = in_group >= d
        b = jnp.where(m, a, 0.0) * pltpu.roll(b, d, 1) + b
        a = a * jnp.where(m, pltpu.roll(a, d, 1), 1.0)
        d *= 2
    a_ref[...] = a.reshape(ts, width)
    b_ref[...] = b.reshape(ts, width)

    def carry_group(i, h):
        rows = pl.ds(pl.multiple_of(i * SUBLANES, SUBLANES), SUBLANES)
        hg = a_ref[rows, :] * h + b_ref[rows, :]
        b_ref[rows, :] = hg
        return hg[SUBLANES - 1:, :]

    h_ref[...] = lax.fori_loop(0, ts // SUBLANES, carry_group, h_ref[...], unroll=8)
    o_ref[...] = (b_ref[...] * _silu(g_ref[...].astype(F32))).astype(o_ref.dtype)


def _proj_block(proj4, split, block, offset=0):
    per_slab = proj4.shape[-1] // block
    col = split * (D_MODEL // block) + offset
    return col // per_slab, col % per_slab


def _rglru(proj4, conv_w, conv_b, wa_bd, b_a, wx_bd, b_x, lam, ts=256):
    _, bsz, seq, _ = proj4.shape
    width = D_MODEL
    row = lambda: pl.BlockSpec((1, width), lambda b, s: (0, 0))

    def split_spec(split):
        slab, blk = _proj_block(proj4, split, width)
        return pl.BlockSpec((None, None, ts, width), lambda b, s: (slab, b, s, blk))

    return pl.pallas_call(
        _rglru_kernel,
        grid=(bsz, seq // ts),
        in_specs=[split_spec(0), split_spec(1),
                  pl.BlockSpec((RG_CONV, width), lambda b, s: (0, 0)),
                  row(),
                  pl.BlockSpec(wa_bd.shape, lambda b, s: (0, 0, 0)),
                  row(),
                  pl.BlockSpec(wx_bd.shape, lambda b, s: (0, 0, 0)),
                  row(), row()],
        out_specs=pl.BlockSpec((None, ts, width), lambda b, s: (b, s, 0)),
        out_shape=jax.ShapeDtypeStruct((bsz, seq, width), BF16),
        scratch_shapes=[pltpu.VMEM((1, width), F32), pltpu.VMEM((SUBLANES + ts, width), F32),
                        pltpu.VMEM((ts, width), F32), pltpu.VMEM((ts, width), F32)],
        compiler_params=_params("parallel", "arbitrary"),
        name="l0_rglru",
    )(proj4, proj4, conv_w, conv_b, wa_bd, b_a, wx_bd, b_x, lam)


def _hgrn2_kernel(u_ref, wq_ref, wf_ref, wv_ref, wg_ref, lbp_ref, gn_ref, o_ref, st_ref, p_ref):
    @pl.when(pl.program_id(2) == 0)
    def _():
        st_ref[...] = jnp.zeros_like(st_ref)

    group_rows = CHUNK * HG_GROUP
    col = lambda j: slice(j * HG_DK, (j + 1) * HG_DK)
    w_halves = (jnp.concatenate([wq_ref[...], wf_ref[...]], axis=1),
                jnp.concatenate([wv_ref[...], wg_ref[...]], axis=1))

    def project(g, slot, half):
        rows = pl.ds(pl.multiple_of(g * group_rows, group_rows), group_rows)
        cols = slice(half * 2 * HG_DK, (half + 1) * 2 * HG_DK)
        p_ref[slot, :, cols] = jnp.dot(u_ref[rows, :], w_halves[half], preferred_element_type=F32)

    lbp = lbp_ref[...]
    e = jnp.exp(lbp - jnp.max(lbp, axis=0, keepdims=True))
    p = e / jnp.sum(e, axis=0, keepdims=True)
    lb = (p[0:1] + p[1:2]) - p[0:1]
    gn = gn_ref[...]
    ngroup = u_ref.shape[0] // group_rows
    srow = lax.broadcasted_iota(jnp.int32, (CHUNK, CHUNK), 0)
    scol = lax.broadcasted_iota(jnp.int32, (CHUNK, CHUNK), 1)
    levels = []
    h = CHUNK // 2
    while h >= 1:
        levels.append((h, (srow & ~(2 * h - 1)) == (scol & ~(2 * h - 1))))
        h //= 2
    sub_row = lax.broadcasted_iota(jnp.int32, (SUBLANES, 1), 0)
    chunk_row = lax.broadcasted_iota(jnp.int32, (CHUNK, 1), 0)
    tril3 = jnp.where(srow >= scol, 1.0, 0.0).astype(BF16)
    tril3 = jnp.concatenate([tril3, tril3, tril3], axis=1)

    def cumsum_rows(x):
        hi = x.astype(BF16)
        r1 = x - hi.astype(F32)
        mid = r1.astype(BF16)
        lo = (r1 - mid.astype(F32)).astype(BF16)
        return jnp.dot(tril3, jnp.concatenate([hi, mid, lo], axis=0), preferred_element_type=F32)

    def block_rows(x, size, r):
        out = []
        for lo in range(0, CHUNK, SUBLANES):
            acc = jnp.broadcast_to(x[lo + r:lo + r + 1], (SUBLANES, LANES))
            for off in range(size, SUBLANES, size):
                acc = jnp.where(sub_row >= off,
                                jnp.broadcast_to(x[lo + off + r:lo + off + r + 1], (SUBLANES, LANES)), acc)
            out.append(acc)
        return jnp.concatenate(out, axis=0)

    def level_factors(q, k, f, g2, h):
        if h < SUBLANES:
            upper = (chunk_row & (2 * h - 1)) >= h
            if h == 1:
                return jnp.where(upper, q * f, 0.0), jnp.where(upper, 0.0, k)
            d = g2 - block_rows(g2, 2 * h, h - 1)
            e = jnp.exp2(jnp.where(upper, d, -d))
            return jnp.where(upper, q * e, 0.0), jnp.where(upper, 0.0, k * e)
        zero = jnp.zeros((h, LANES), F32)
        qt, kt = [], []
        for lo in range(0, CHUNK, 2 * h):
            mid = lo + h
            rho = g2[mid - 1:mid, :]
            kt += [k[lo:mid] * jnp.exp2(rho - g2[lo:mid]), zero]
            qt += [zero, q[mid:mid + h] * jnp.exp2(g2[mid:mid + h] - rho)]
        return jnp.concatenate(qt, axis=0), jnp.concatenate(kt, axis=0)

    def group(i, carry):
        idx = range(HG_GROUP)
        slot = i % 2
        nxt = jnp.minimum(i + 1, ngroup - 1)
        local = [pl.ds(j * CHUNK, CHUNK) for j in idx]
        rows = [pl.ds(pl.multiple_of((i * HG_GROUP + j) * CHUNK, CHUNK), CHUNK) for j in idx]
        q = [_silu(p_ref[slot, r, col(0)]) for r in local]
        sig = [_sigmoid(p_ref[slot, r, col(1)]) for r in local]
        v = [p_ref[slot, r, col(2)].astype(BF16) for r in local]
        gate = [p_ref[slot, r, col(3)] for r in local]
        k = [(1.0 - lb) * (1.0 - x) for x in sig]
        f = [lb + (1.0 - lb) * x for x in sig]
        g2 = [cumsum_rows(jnp.log2(x)) for x in f]
        project(nxt, 1 - slot, 0)
        gend = [x[CHUNK - 1:CHUNK, :] for x in g2]
        qg = [(a * jnp.exp2(b)).astype(BF16) for a, b in zip(q, g2)]
        vtk = [_dot_tn(a, b * jnp.exp2(e - c)) for a, b, c, e in zip(v, k, g2, gend)]

        scores = [jnp.where(srow == scol, jnp.sum(a * b, axis=-1, keepdims=True), 0.0)
                  for a, b in zip(q, k)]
        for h, same_block in levels:
            for j in idx:
                qt, kt = level_factors(q[j], k[j], f[j], g2[j], h)
                scores[j] = scores[j] + jnp.where(same_block, _dot_nt(qt, kt), 0.0)
        out = [_dot(a, b) for a, b in zip(scores, v)]

        st = st_ref[...]
        for j in idx:
            out[j] = out[j] + _dot_nt(qg[j], st)
            st = st * jnp.exp2(gend[j]) + vtk[j]
        st_ref[...] = st
        project(nxt, 1 - slot, 1)
        for j in idx:
            o_ref[rows[j], :] = (_rms(out[j], gn) * _silu(gate[j])).astype(o_ref.dtype)
        return carry

    project(0, 0, 0)
    project(0, 0, 1)
    lax.fori_loop(0, ngroup, group, 0)


def _hgrn2(u3, w_bf16, first_split, lb_params, g_norm, ts=8192):
    bsz, seq, d = u3.shape

    def w_spec(split):
        base = split * (d // HG_DK)
        return pl.BlockSpec((d, HG_DK), lambda b, h, s: (0, base + h))

    return pl.pallas_call(
        _hgrn2_kernel,
        grid=(bsz, HG_HEADS, seq // ts),
        in_specs=[pl.BlockSpec((None, ts, d), lambda b, h, s: (b, s, 0))]
        + [w_spec(first_split + j) for j in range(4)]
        + [pl.BlockSpec((lb_params.shape[0], HG_DK), lambda b, h, s: (0, h)),
           pl.BlockSpec((1, HG_DK), lambda b, h, s: (0, 0))],
        out_specs=pl.BlockSpec((None, ts, HG_DK), lambda b, h, s: (b, s, h)),
        out_shape=jax.ShapeDtypeStruct((bsz, seq, d), BF16),
        scratch_shapes=[pltpu.VMEM((HG_DK, HG_DK), F32),
                        pltpu.VMEM((2, CHUNK * HG_GROUP, 4 * HG_DK), F32)],
        compiler_params=_params("parallel", "parallel", "arbitrary"),
        name="l0_hgrn2",
    )(u3, w_bf16, w_bf16, w_bf16, w_bf16, lb_params, g_norm)


def _out_proj0_kernel(hg_ref, og_ref, w_ref, pn_ref, x_ref, o_ref):
    half = hg_ref.shape[1]
    y = (jnp.dot(hg_ref[...], w_ref[:half, :], preferred_element_type=F32)
         + jnp.dot(og_ref[...], w_ref[half:, :], preferred_element_type=F32))
    o_ref[...] = x_ref[...] + _rms(y, pn_ref[...])


def _out_proj0(hg, og, w_bf16, post_gain, x2d, tm=512):
    t, d = x2d.shape
    tok = lambda: pl.BlockSpec((tm, d), lambda i: (i, 0))
    return pl.pallas_call(
        _out_proj0_kernel,
        grid=(t // tm,),
        in_specs=[tok(), tok(),
                  pl.BlockSpec(w_bf16.shape, lambda i: (0, 0)),
                  pl.BlockSpec((1, d), lambda i: (0, 0)),
                  tok()],
        out_specs=tok(),
        out_shape=jax.ShapeDtypeStruct((t, d), F32),
        compiler_params=_params("parallel"),
        name="l0_out_proj",
    )(hg, og, w_bf16, post_gain, x2d)


def _rwkv_front_kernel(x_ref, pn_ref, mu_ref, win_ref, w0_ref, w1_ref, w2_ref, a0_ref, a1_ref,
                       a2_ref, kk_ref, ka_ref, ones_ref,
                       r_out, lw_out, k_out, v_out, kk_out, b_out, g_out, prev_ref):
    @pl.when(pl.program_id(1) == 0)
    def _():
        prev_ref[...] = jnp.zeros_like(prev_ref)

    u = _rms(x_ref[...], pn_ref[...])
    tm = u.shape[0]
    row = lax.broadcasted_iota(jnp.int32, (tm, 1), 0)
    delta = jnp.where(row == 0, prev_ref[...], pltpu.roll(u, 1, 0)) - u
    prev_ref[...] = u[tm - 1:tm, :]
    mix = lambda p: u + delta * mu_ref[p:p + 1, :]

    r_out[...] = _dot(mix(0), win_ref[0]).astype(r_out.dtype)
    k = _dot(mix(1), win_ref[1])
    v_out[...] = _dot(mix(2), win_ref[2]).astype(v_out.dtype)
    g_out[...] = _dot(mix(3), win_ref[3]).astype(g_out.dtype)
    lora_w = _dot(jnp.tanh(_dot(mix(4), w1_ref[...])), w2_ref[...])
    lora_a = _dot(_dot(mix(5), a1_ref[...]), a2_ref[...])
    lw_out[...] = (-EXP_NEG_HALF) * _sigmoid(w0_ref[...] + lora_w)
    iclr = _sigmoid(a0_ref[...] + lora_a)
    k_out[...] = (k * (1.0 + (iclr - 1.0) * ka_ref[...])).astype(k_out.dtype)
    kk = k * kk_ref[...]
    kk = kk * lax.rsqrt(jnp.maximum(_seg_sum(kk * kk, ones_ref[...]), 1e-24))
    kk_out[...] = kk.astype(kk_out.dtype)
    b_out[...] = (kk * iclr).astype(b_out.dtype)


def _rwkv_front(x1, pre_gain, mu, win, w0, w1, w2, a0, a1, a2, k_k, k_a, ones_bd, tm=512):
    bsz, seq, d = x1.shape
    tok = lambda: pl.BlockSpec((None, tm, d), lambda b, s: (b, s, 0))
    full = lambda a: pl.BlockSpec(a.shape, lambda b, s: (0,) * a.ndim)
    args = (pre_gain, mu, win, w0, w1, w2, a0, a1, a2, k_k, k_a, ones_bd)
    return pl.pallas_call(
        _rwkv_front_kernel,
        grid=(bsz, seq // tm),
        in_specs=[tok()] + [full(a) for a in args],
        out_specs=[tok() for _ in range(7)],
        out_shape=[jax.ShapeDtypeStruct((bsz, seq, d), F32 if i == 1 else BF16) for i in range(7)],
        scratch_shapes=[pltpu.VMEM((1, d), F32)],
        compiler_params=_params("parallel", "arbitrary"),
        name="l1_front",
    )(x1, *args)


def _rwkv_scan_kernel(r_ref, lw_ref, k_ref, v_ref, kk_ref, b_ref, y_ref,
                      ht_ref, rt_s, y0_s, ml_s, ha_s, gm_s):
    @pl.when(pl.program_id(2) == 0)
    def _():
        ht_ref[...] = jnp.zeros_like(ht_ref)

    ts, width = r_ref.shape
    nchunk = ts // CHUNK
    npair = width // LANES
    lanes_of = [slice(p * LANES, (p + 1) * LANES) for p in range(npair)]
    two = 2 * CHUNK
    head0 = lax.broadcasted_iota(jnp.int32, (CHUNK, LANES), 1) < RW_HEAD
    row = lax.broadcasted_iota(jnp.int32, (two, two), 0)
    col = lax.broadcasted_iota(jnp.int32, (two, two), 1)
    strict = (row & (CHUNK - 1)) > (col & (CHUNK - 1))
    incl = (row & (CHUNK - 1)) >= (col & (CHUNK - 1))
    eye = jnp.where(row == col, 1.0, 0.0)

    def stack(x):
        return jnp.concatenate([jnp.where(head0, x, 0.0), jnp.where(head0, 0.0, x)], axis=0)

    def prep(c):
        rows = pl.ds(pl.multiple_of(c * CHUNK, CHUNK), CHUNK)
        r, lw, k, v, kk, b = (ref[rows, :].astype(F32)
                              for ref in (r_ref, lw_ref, k_ref, v_ref, kk_ref, b_ref))
        lcum = _cumsum_rows(lw)
        lend = lcum[CHUNK - 1:CHUNK, :]
        e_neg = jnp.exp(-lcum)
        e_end = jnp.exp(lend - lcum)
        gm_s[c] = jnp.exp(lend)
        at, rt = -kk * jnp.exp(lcum - lw), r * jnp.exp(lcum)
        bt, kt, bh, kh = b * e_neg, k * e_neg, b * e_end, k * e_end
        return [dict(rows=rows, c=c, p=p, sl=sl,
                     at2=stack(at[:, sl]).astype(BF16),
                     rt2=stack(rt[:, sl]),
                     bk2=jnp.concatenate([stack(bt[:, sl]), stack(kt[:, sl])], axis=0).astype(BF16),
                     v2=stack(v[:, sl]).astype(BF16),
                     bkh2=jnp.concatenate([stack(bh[:, sl]), stack(kh[:, sl])], axis=0).astype(BF16))
                for p, sl in enumerate(lanes_of)]

    def local_group(i, between=()):
        pending = list(between)
        passed = [0]

        def boundary():
            fired = len(between) - len(pending)
            if pending and passed[0] * len(between) >= fired * SCAN_STAGES:
                pending.pop(0)()
            passed[0] += 1

        g = [q for j in range(SCAN_GROUP) for q in prep(i * SCAN_GROUP + j)]
        s = [_dot_nt(jnp.concatenate([q["at2"], q["rt2"].astype(BF16)], axis=0), q["bk2"]) for q in g]
        boundary()
        n = [jnp.where(strict, x[:two, :two], 0.0) for x in s]
        a_ak = [jnp.where(strict, x[:two, two:], 0.0).astype(BF16) for x in s]
        a_rbk = [jnp.concatenate([jnp.where(incl, x[two:, :two], 0.0),
                                  jnp.where(incl, x[two:, two:], 0.0)], axis=1).astype(BF16) for x in s]
        t = [eye + x for x in n]
        pw = [_dot(x, x).astype(BF16) for x in n]
        boundary()
        d = 2
        while d < CHUNK // 2:
            m = [_dot(jnp.concatenate([a.astype(BF16), b], axis=0), b) for a, b in zip(t, pw)]
            t = [a + x[:two] for a, x in zip(t, m)]
            pw = [x[two:].astype(BF16) for x in m]
            boundary()
            d *= 2
        t = [a + _dot(a, b) for a, b in zip(t, pw)]
        boundary()
        akv = [_dot(x, q["v2"]) for x, q in zip(a_ak, g)]
        tw = [_dot(x, jnp.concatenate([q["at2"], y.astype(BF16)], axis=1)).astype(BF16)
              for x, y, q in zip(t, akv, g)]
        boundary()
        rw = [_dot(x[:, :two], y[:, :LANES]) for x, y in zip(a_rbk, tw)]
        y0 = [_dot(x, jnp.concatenate([y[:, LANES:], q["v2"]], axis=0)) for x, y, q in zip(a_rbk, tw, g)]
        ml = [_dot_tn(q["bkh2"][:two], y[:, :LANES]) for q, y in zip(g, tw)]
        ha = [_dot_tn(jnp.concatenate([y[:, LANES:], q["v2"]], axis=0), q["bkh2"]) for q, y in zip(g, tw)]
        while pending:
            pending.pop(0)()
        for j, q in enumerate(g):
            rtp = q["rt2"] + rw[j]
            rt_s[q["rows"], q["sl"]] = (rtp[:CHUNK] + rtp[CHUNK:]).astype(BF16)
            y0_s[q["rows"], q["sl"]] = y0[j][:CHUNK] + y0[j][CHUNK:]
            ml_s[q["c"], q["p"]] = ml[j].astype(BF16)
            ha_s[q["c"], q["p"]] = ha[j]

    def state_step(c, hts):
        rows = pl.ds(pl.multiple_of(c * CHUNK, CHUNK), CHUNK)
        rt, gm = rt_s[rows, :], gm_s[c]
        htb = [h.astype(BF16) for h in hts]
        ys = [_dot_nt(rt[:, sl], hb) for sl, hb in zip(lanes_of, htb)]
        new = [h * gm[:, sl] + _dot_nt(hb, ml_s[c, p]) + ha_s[c, p]
               for p, (sl, h, hb) in enumerate(zip(lanes_of, hts, htb))]
        y_ref[rows, :] = (jnp.concatenate(ys, axis=1) + y0_s[rows, :]).astype(y_ref.dtype)
        return tuple(new)

    def pipelined(i, hts):
        box = [hts]

        def advance(j):
            def run():
                box[0] = state_step((i - 1) * SCAN_GROUP + j, box[0])
            return run

        local_group(i, between=[advance(j) for j in range(SCAN_GROUP)])
        return box[0]

    ngroup = nchunk // SCAN_GROUP
    local_group(0)
    hts = lax.fori_loop(1, ngroup, pipelined, tuple(ht_ref[p] for p in range(npair)))
    for j in range(SCAN_GROUP):
        hts = state_step((ngroup - 1) * SCAN_GROUP + j, hts)
    for p in range(npair):
        ht_ref[p] = hts[p]


def _rwkv_scan(r, lw, k, v, kk, b, ts=1024, width=512):
    bsz, seq, d = r.shape
    nchunk = ts // CHUNK
    npair = width // LANES
    spec = lambda: pl.BlockSpec((None, ts, width), lambda bb, h, s: (bb, s, h))
    return pl.pallas_call(
        _rwkv_scan_kernel,
        grid=(bsz, d // width, seq // ts),
        in_specs=[spec() for _ in range(6)],
        out_specs=spec(),
        out_shape=jax.ShapeDtypeStruct((bsz, seq, d), BF16),
        scratch_shapes=[pltpu.VMEM((npair, LANES, LANES), F32),
                        pltpu.VMEM((ts, width), BF16),
                        pltpu.VMEM((ts, width), F32),
                        pltpu.VMEM((nchunk, npair, LANES, LANES), BF16),
                        pltpu.VMEM((nchunk, npair, LANES, LANES), F32),
                        pltpu.VMEM((nchunk, 1, width), F32)],
        compiler_params=_params("parallel", "parallel", "arbitrary"),
        name="l1_scan",
    )(r, lw, k, v, kk, b)


def _rwkv_back_kernel(y_ref, r_ref, k_ref, v_ref, g_ref, x_ref, rk_ref, lnw_ref, lnb_ref,
                      w_ref, pn_ref, ones_ref, o_ref):
    ones = ones_ref[...]
    inv_n = 1.0 / RW_HEAD
    y = y_ref[...].astype(F32)
    yc = y - _seg_sum(y, ones, split=False) * inv_n
    var = _seg_sum(yc * yc, ones, split=False) * inv_n
    yn = yc * lax.rsqrt(var + RW_GN_EPS) * lnw_ref[...] + lnb_ref[...]
    rk = r_ref[...].astype(F32) * k_ref[...].astype(F32) * rk_ref[...]
    yn = yn + _seg_sum(rk, ones, split=False) * v_ref[...].astype(F32)
    out = _dot(yn * _silu(g_ref[...].astype(F32)), w_ref[...])
    o_ref[...] = x_ref[...] + _rms(out, pn_ref[...])


def _rwkv_back(y, r, k, v, g, x1, r_k, ln_w, ln_b, w_bf16, post_gain, ones_bd, tm=512):
    t, d = y.shape
    tok = lambda: pl.BlockSpec((tm, d), lambda i: (i, 0))
    full = lambda a: pl.BlockSpec(a.shape, lambda i: (0,) * a.ndim)
    params = (r_k, ln_w, ln_b, w_bf16, post_gain, ones_bd)
    return pl.pallas_call(
        _rwkv_back_kernel,
        grid=(t // tm,),
        in_specs=[tok() for _ in range(6)] + [full(a) for a in params],
        out_specs=tok(),
        out_shape=jax.ShapeDtypeStruct((t, d), F32),
        compiler_params=_params("parallel"),
        name="l1_back",
    )(y, r, k, v, g, x1, *params)


def _block_diag(w, group):
    n, bi, bj = w.shape
    w = w.reshape(n // group, group, bi, bj)
    eye = jnp.eye(group, dtype=w.dtype)
    return jnp.einsum("gaij,ab->gaibj", w, eye).reshape(n // group, group * bi, group * bj)


def kernel(x, pre_norm, post_norm, ab_w_in, ab_w_out, rg_conv_w, rg_conv_b, rg_w_a, rg_b_a,
           rg_w_x, rg_b_x, rg_lambda, hg_lower_bounds, hg_out_norm, rw_mu, rw_w_in, rw_w_out,
           rw_w0, rw_w1, rw_w2, rw_a0, rw_a1, rw_a2, rw_k_k, rw_k_a, rw_r_k, rw_ln_w, rw_ln_b):
    bsz, seq, d = x.shape
    t = bsz * seq
    row = lambda a: a.reshape(1, -1).astype(F32)
    x2d = x.reshape(t, d)
    heads_per_tile = MXU_DIM // RW_HEAD
    ones_bd = _block_diag(jnp.ones((heads_per_tile, RW_HEAD, RW_HEAD), F32), heads_per_tile)[0].astype(BF16)

    w_in = ab_w_in[0].astype(BF16)
    rg_splits = 2
    proj, u0 = _norm_proj(x2d, row(pre_norm[0]), w_in, rg_splits * d)
    proj4 = proj.reshape(proj.shape[0], bsz, seq, proj.shape[-1])
    rg_group = MXU_DIM // rg_w_a.shape[-1]
    hg = _rglru(proj4, rg_conv_w[0].astype(F32), row(rg_conv_b[0]),
                _block_diag(rg_w_a[0], rg_group).astype(BF16), row(rg_b_a[0]),
                _block_diag(rg_w_x[0], rg_group).astype(BF16), row(rg_b_x[0]), row(rg_lambda[0]))
    og = _hgrn2(u0.reshape(bsz, seq, d), w_in, rg_splits, hg_lower_bounds.astype(F32),
                row(hg_out_norm[0]))
    x1 = _out_proj0(hg.reshape(t, d), og.reshape(t, d), ab_w_out[0].astype(BF16),
                    row(post_norm[0]), x2d)

    r, lw, k, v, kk, b, g = _rwkv_front(
        x1.reshape(bsz, seq, d), row(pre_norm[1]), rw_mu[0].astype(F32), rw_w_in[0].astype(BF16),
        row(rw_w0[0]), rw_w1[0].astype(BF16), rw_w2[0].astype(BF16), row(rw_a0[0]),
        rw_a1[0].astype(BF16), rw_a2[0].astype(BF16), row(rw_k_k[0]), row(rw_k_a[0]), ones_bd)
    y = _rwkv_scan(r, lw, k, v, kk, b)
    flat = lambda a: a.reshape(t, d)
    x2 = _rwkv_back(flat(y), flat(r), flat(k), flat(v), flat(g), x1, row(rw_r_k[0]),
                    row(rw_ln_w[0]), row(rw_ln_b[0]), rw_w_out[0].astype(BF16),
                    row(post_norm[1]), ones_bd)
    return x2.reshape(bsz, seq, d)
```

```python
import functools

import jax
import jax.numpy as jnp
from jax import lax
from jax.experimental import pallas as pl
from jax.experimental.pallas import tpu as pltpu

F32 = jnp.float32
BF16 = jnp.bfloat16

D_MODEL = 1024
RMS_EPS = 1e-6
RG_C = 8.0
RG_CONV = 4
HG_HEADS = 8
HG_DK = 128
RW_HEAD = 64
RW_GN_EPS = 64e-5

LANES = 128
SUBLANES = 8
MXU_DIM = 256
VMEM_LIMIT = 56 * 1024 * 1024

CHUNK = 64
SCAN_GROUP = 4
HG_GROUP = 16
SCAN_STAGES = 8
EXP_NEG_HALF = 0.6065306597126334


def _params(*sem):
    return pltpu.CompilerParams(dimension_semantics=sem, vmem_limit_bytes=VMEM_LIMIT)


def _dot(a, b):
    return jnp.dot(a.astype(BF16), b.astype(BF16), preferred_element_type=F32)


def _dot_nt(a, b):
    return lax.dot_general(a.astype(BF16), b.astype(BF16), (((1,), (1,)), ((), ())),
                           preferred_element_type=F32)


def _dot_tn(a, b):
    return lax.dot_general(a.astype(BF16), b.astype(BF16), (((0,), (0,)), ((), ())),
                           preferred_element_type=F32)


def _sigmoid(x):
    return 0.5 * jnp.tanh(0.5 * x) + 0.5


def _silu(x):
    return x * _sigmoid(x)


def _softplus(z):
    return jnp.maximum(z, 0.0) + jnp.log(1.0 + jnp.exp(-jnp.abs(z)))


def _rms(x, gain):
    return x * lax.rsqrt(jnp.mean(x * x, axis=-1, keepdims=True) + RMS_EPS) * gain


def _cumsum_rows(x):
    n = x.shape[0]
    row = lax.broadcasted_iota(jnp.int32, x.shape, 0)
    d = 1
    while d < n:
        x = x + jnp.where(row >= d, pltpu.roll(x, d, 0), 0.0)
        d *= 2
    return x


def _seg_sum(x, ones_bd, split=True):
    hi = x.astype(BF16)
    lo = (x - hi.astype(F32)).astype(BF16) if split else None
    outs = []
    for g in range(x.shape[1] // MXU_DIM):
        sl = slice(g * MXU_DIM, (g + 1) * MXU_DIM)
        acc = jnp.dot(hi[:, sl], ones_bd, preferred_element_type=F32)
        if split:
            acc = acc + jnp.dot(lo[:, sl], ones_bd, preferred_element_type=F32)
        outs.append(acc)
    return jnp.concatenate(outs, axis=-1)


def _norm_proj_kernel(x_ref, g_ref, w_ref, o_ref, u_ref):
    @pl.when(pl.program_id(1) == 0)
    def _():
        u_ref[...] = _rms(x_ref[...], g_ref[...]).astype(BF16)

    o_ref[...] = jnp.dot(u_ref[...], w_ref[...], preferred_element_type=F32).astype(o_ref.dtype)


def _norm_proj(x2d, gain, w_bf16, n, tm=1024, tn=2048):
    t, d = x2d.shape
    return pl.pallas_call(
        _norm_proj_kernel,
        grid=(t // tm, n // tn),
        in_specs=[pl.BlockSpec((tm, d), lambda i, j: (i, 0)),
                  pl.BlockSpec((1, d), lambda i, j: (0, 0)),
                  pl.BlockSpec((d, tn), lambda i, j: (0, j))],
        out_specs=[pl.BlockSpec((None, tm, tn), lambda i, j: (j, i, 0)),
                   pl.BlockSpec((tm, d), lambda i, j: (i, 0))],
        out_shape=[jax.ShapeDtypeStruct((n // tn, t, tn), BF16),
                   jax.ShapeDtypeStruct((t, d), BF16)],
        compiler_params=_params("parallel", "arbitrary"),
        name="l0_norm_proj",
    )(x2d, gain, w_bf16)


def _rglru_kernel(x_ref, g_ref, cw_ref, cb_ref, wa_ref, ba_ref, wx_ref, bx_ref, lam_ref,
                  o_ref, h_ref, ext_ref, a_ref, b_ref):
    @pl.when(pl.program_id(1) == 0)
    def _():
        h_ref[...] = jnp.zeros_like(h_ref)
        ext_ref[:SUBLANES, :] = jnp.zeros((SUBLANES, ext_ref.shape[1]), F32)

    x = x_ref[...].astype(F32)
    ts, width = x.shape
    ext_ref[SUBLANES:, :] = x
    xc = cb_ref[...] + cw_ref[RG_CONV - 1:RG_CONV, :] * x
    for k in range(1, RG_CONV):
        xc = xc + cw_ref[RG_CONV - 1 - k:RG_CONV - k, :] * ext_ref[SUBLANES - k:SUBLANES - k + ts, :]
    ext_ref[:SUBLANES, :] = x[ts - SUBLANES:, :]

    xcb = xc.astype(BF16)
    pre_r, pre_i = [], []
    for g in range(width // MXU_DIM):
        sl = slice(g * MXU_DIM, (g + 1) * MXU_DIM)
        pre_r.append(jnp.dot(xcb[:, sl], wa_ref[g], preferred_element_type=F32))
        pre_i.append(jnp.dot(xcb[:, sl], wx_ref[g], preferred_element_type=F32))
    gate_r = _sigmoid(jnp.concatenate(pre_r, axis=-1) + ba_ref[...])
    gate_i = _sigmoid(jnp.concatenate(pre_i, axis=-1) + bx_ref[...])

    log_a = (-RG_C) * gate_r * _softplus(-lam_ref[...])
    a = jnp.exp(log_a)
    b = jnp.sqrt(1.0 - a * a) * (gate_i * xc)

    a = a.reshape(ts // SUBLANES, SUBLANES, width)
    b = b.reshape(ts // SUBLANES, SUBLANES, width)
    in_group = lax.broadcasted_iota(jnp.int32, (1, SUBLANES, 1), 1)
    d = 1
    while d < SUBLANES:
        m = in_group >= d
        b = jnp.where(m, a, 0.0) * pltpu.roll(b, d, 1) + b
        a = a * jnp.where(m, pltpu.roll(a, d, 1), 1.0)
        d *= 2
    a_ref[...] = a.reshape(ts, width)
    b_ref[...] = b.reshape(ts, width)

    def carry_group(i, h):
        rows = pl.ds(pl.multiple_of(i * SUBLANES, SUBLANES), SUBLANES)
        hg = a_ref[rows, :] * h + b_ref[rows, :]
        b_ref[rows, :] = hg
        return hg[SUBLANES - 1:, :]

    h_ref[...] = lax.fori_loop(0, ts // SUBLANES, carry_group, h_ref[...], unroll=8)
    o_ref[...] = (b_ref[...] * _silu(g_ref[...].astype(F32))).astype(o_ref.dtype)


def _proj_block(proj4, split, block, offset=0):
    per_slab = proj4.shape[-1] // block
    col = split * (D_MODEL // block) + offset
    return col // per_slab, col % per_slab


def _rglru(proj4, conv_w, conv_b, wa_bd, b_a, wx_bd, b_x, lam, ts=256):
    _, bsz, seq, _ = proj4.shape
    width = D_MODEL
    row = lambda: pl.BlockSpec((1, width), lambda b, s: (0, 0))

    def split_spec(split):
        slab, blk = _proj_block(proj4, split, width)
        return pl.BlockSpec((None, None, ts, width), lambda b, s: (slab, b, s, blk))

    return pl.pallas_call(
        _rglru_kernel,
        grid=(bsz, seq // ts),
        in_specs=[split_spec(0), split_spec(1),
                  pl.BlockSpec((RG_CONV, width), lambda b, s: (0, 0)),
                  row(),
                  pl.BlockSpec(wa_bd.shape, lambda b, s: (0, 0, 0)),
                  row(),
                  pl.BlockSpec(wx_bd.shape, lambda b, s: (0, 0, 0)),
                  row(), row()],
        out_specs=pl.BlockSpec((None, ts, width), lambda b, s: (b, s, 0)),
        out_shape=jax.ShapeDtypeStruct((bsz, seq, width), BF16),
        scratch_shapes=[pltpu.VMEM((1, width), F32), pltpu.VMEM((SUBLANES + ts, width), F32),
                        pltpu.VMEM((ts, width), F32), pltpu.VMEM((ts, width), F32)],
        compiler_params=_params("parallel", "arbitrary"),
        name="l0_rglru",
    )(proj4, proj4, conv_w, conv_b, wa_bd, b_a, wx_bd, b_x, lam)


def _hgrn2_kernel(u_ref, wq_ref, wf_ref, wv_ref, wg_ref, lbp_ref, gn_ref, o_ref, st_ref, p_ref):
    @pl.when(pl.program_id(2) == 0)
    def _():
        st_ref[...] = jnp.zeros_like(st_ref)

    group_rows = CHUNK * HG_GROUP
    col = lambda j: slice(j * HG_DK, (j + 1) * HG_DK)
    w_halves = (jnp.concatenate([wq_ref[...], wf_ref[...]], axis=1),
                jnp.concatenate([wv_ref[...], wg_ref[...]], axis=1))

    def project(g, slot, half):
        rows = pl.ds(pl.multiple_of(g * group_rows, group_rows), group_rows)
        cols = slice(half * 2 * HG_DK, (half + 1) * 2 * HG_DK)
        p_ref[slot, :, cols] = jnp.dot(u_ref[rows, :], w_halves[half], preferred_element_type=F32)

    lbp = lbp_ref[...]
    e = jnp.exp(lbp - jnp.max(lbp, axis=0, keepdims=True))
    p = e / jnp.sum(e, axis=0, keepdims=True)
    lb = (p[0:1] + p[1:2]) - p[0:1]
    gn = gn_ref[...]
    ngroup = u_ref.shape[0] // group_rows
    srow = lax.broadcasted_iota(jnp.int32, (CHUNK, CHUNK), 0)
    scol = lax.broadcasted_iota(jnp.int32, (CHUNK, CHUNK), 1)
    levels = []
    h = CHUNK // 2
    while h >= 1:
        levels.append((h, (srow & ~(2 * h - 1)) == (scol & ~(2 * h - 1))))
        h //= 2
    sub_row = lax.broadcasted_iota(jnp.int32, (SUBLANES, 1), 0)
    chunk_row = lax.broadcasted_iota(jnp.int32, (CHUNK, 1), 0)
    tril3 = jnp.where(srow >= scol, 1.0, 0.0).astype(BF16)
    tril3 = jnp.concatenate([tril3, tril3, tril3], axis=1)

    def cumsum_rows(x):
        hi = x.astype(BF16)
        r1 = x - hi.astype(F32)
        mid = r1.astype(BF16)
        lo = (r1 - mid.astype(F32)).astype(BF16)
        return jnp.dot(tril3, jnp.concatenate([hi, mid, lo], axis=0), preferred_element_type=F32)

    def block_rows(x, size, r):
        out = []
        for lo in range(0, CHUNK, SUBLANES):
            acc = jnp.broadcast_to(x[lo + r:lo + r + 1], (SUBLANES, LANES))
            for off in range(size, SUBLANES, size):
                acc = jnp.where(sub_row >= off,
                                jnp.broadcast_to(x[lo + off + r:lo + off + r + 1], (SUBLANES, LANES)), acc)
            out.append(acc)
        return jnp.concatenate(out, axis=0)

    def level_factors(q, k, f, g2, h):
        if h < SUBLANES:
            upper = (chunk_row & (2 * h - 1)) >= h
            if h == 1:
                return jnp.where(upper, q * f, 0.0), jnp.where(upper, 0.0, k)
            d = g2 - block_rows(g2, 2 * h, h - 1)
            e = jnp.exp2(jnp.where(upper, d, -d))
            return jnp.where(upper, q * e, 0.0), jnp.where(upper, 0.0, k * e)
        zero = jnp.zeros((h, LANES), F32)
        qt, kt = [], []
        for lo in range(0, CHUNK, 2 * h):
            mid = lo + h
            rho = g2[mid - 1:mid, :]
            kt += [k[lo:mid] * jnp.exp2(rho - g2[lo:mid]), zero]
            qt += [zero, q[mid:mid + h] * jnp.exp2(g2[mid:mid + h] - rho)]
        return jnp.concatenate(qt, axis=0), jnp.concatenate(kt, axis=0)

    def group(i, carry):
        idx = range(HG_GROUP)
        slot = i % 2
        nxt = jnp.minimum(i + 1, ngroup - 1)
        local = [pl.ds(j * CHUNK, CHUNK) for j in idx]
        rows = [pl.ds(pl.multiple_of((i * HG_GROUP + j) * CHUNK, CHUNK), CHUNK) for j in idx]
        q = [_silu(p_ref[slot, r, col(0)]) for r in local]
        sig = [_sigmoid(p_ref[slot, r, col(1)]) for r in local]
        v = [p_ref[slot, r, col(2)].astype(BF16) for r in local]
        gate = [p_ref[slot, r, col(3)] for r in local]
        k = [(1.0 - lb) * (1.0 - x) for x in sig]
        f = [lb + (1.0 - lb) * x for x in sig]
        g2 = [cumsum_rows(jnp.log2(x)) for x in f]
        project(nxt, 1 - slot, 0)
        gend = [x[CHUNK - 1:CHUNK, :] for x in g2]
        qg = [(a * jnp.exp2(b)).astype(BF16) for a, b in zip(q, g2)]
        vtk = [_dot_tn(a, b * jnp.exp2(e - c)) for a, b, c, e in zip(v, k, g2, gend)]

        scores = [jnp.where(srow == scol, jnp.sum(a * b, axis=-1, keepdims=True), 0.0)
                  for a, b in zip(q, k)]
        for h, same_block in levels:
            for j in idx:
                qt, kt = level_factors(q[j], k[j], f[j], g2[j], h)
                scores[j] = scores[j] + jnp.where(same_block, _dot_nt(qt, kt), 0.0)
        out = [_dot(a, b) for a, b in zip(scores, v)]

        st = st_ref[...]
        for j in idx:
            out[j] = out[j] + _dot_nt(qg[j], st)
            st = st * jnp.exp2(gend[j]) + vtk[j]
        st_ref[...] = st
        project(nxt, 1 - slot, 1)
        for j in idx:
            o_ref[rows[j], :] = (_rms(out[j], gn) * _silu(gate[j])).astype(o_ref.dtype)
        return carry

    project(0, 0, 0)
    project(0, 0, 1)
    lax.fori_loop(0, ngroup, group, 0)


def _hgrn2(u3, w_bf16, first_split, lb_params, g_norm, ts=8192):
    bsz, seq, d = u3.shape

    def w_spec(split):
        base = split * (d // HG_DK)
        return pl.BlockSpec((d, HG_DK), lambda b, h, s: (0, base + h))

    return pl.pallas_call(
        _hgrn2_kernel,
        grid=(bsz, HG_HEADS, seq // ts),
        in_specs=[pl.BlockSpec((None, ts, d), lambda b, h, s: (b, s, 0))]
        + [w_spec(first_split + j) for j in range(4)]
        + [pl.BlockSpec((lb_params.shape[0], HG_DK), lambda b, h, s: (0, h)),
           pl.BlockSpec((1, HG_DK), lambda b, h, s: (0, 0))],
        out_specs=pl.BlockSpec((None, ts, HG_DK), lambda b, h, s: (b, s, h)),
        out_shape=jax.ShapeDtypeStruct((bsz, seq, d), BF16),
        scratch_shapes=[pltpu.VMEM((HG_DK, HG_DK), F32),
                        pltpu.VMEM((2, CHUNK * HG_GROUP, 4 * HG_DK), F32)],
        compiler_params=_params("parallel", "parallel", "arbitrary"),
        name="l0_hgrn2",
    )(u3, w_bf16, w_bf16, w_bf16, w_bf16, lb_params, g_norm)


def _out_proj0_kernel(hg_ref, og_ref, w_ref, pn_ref, x_ref, o_ref):
    half = hg_ref.shape[1]
    y = (jnp.dot(hg_ref[...], w_ref[:half, :], preferred_element_type=F32)
         + jnp.dot(og_ref[...], w_ref[half:, :], preferred_element_type=F32))
    o_ref[...] = x_ref[...] + _rms(y, pn_ref[...])


def _out_proj0(hg, og, w_bf16, post_gain, x2d, tm=512):
    t, d = x2d.shape
    tok = lambda: pl.BlockSpec((tm, d), lambda i: (i, 0))
    return pl.pallas_call(
        _out_proj0_kernel,
        grid=(t // tm,),
        in_specs=[tok(), tok(),
                  pl.BlockSpec(w_bf16.shape, lambda i: (0, 0)),
                  pl.BlockSpec((1, d), lambda i: (0, 0)),
                  tok()],
        out_specs=tok(),
        out_shape=jax.ShapeDtypeStruct((t, d), F32),
        compiler_params=_params("parallel"),
        name="l0_out_proj",
    )(hg, og, w_bf16, post_gain, x2d)


def _rwkv_front_kernel(x_ref, pn_ref, mu_ref, win_ref, w0_ref, w1_ref, w2_ref, a0_ref, a1_ref,
                       a2_ref, kk_ref, ka_ref, ones_ref,
                       r_out, lw_out, k_out, v_out, kk_out, b_out, g_out, prev_ref):
    @pl.when(pl.program_id(1) == 0)
    def _():
        prev_ref[...] = jnp.zeros_like(prev_ref)

    u = _rms(x_ref[...], pn_ref[...])
    tm = u.shape[0]
    row = lax.broadcasted_iota(jnp.int32, (tm, 1), 0)
    delta = jnp.where(row == 0, prev_ref[...], pltpu.roll(u, 1, 0)) - u
    prev_ref[...] = u[tm - 1:tm, :]
    mix = lambda p: u + delta * mu_ref[p:p + 1, :]

    r_out[...] = _dot(mix(0), win_ref[0]).astype(r_out.dtype)
    k = _dot(mix(1), win_ref[1])
    v_out[...] = _dot(mix(2), win_ref[2]).astype(v_out.dtype)
    g_out[...] = _dot(mix(3), win_ref[3]).astype(g_out.dtype)
    lora_w = _dot(jnp.tanh(_dot(mix(4), w1_ref[...])), w2_ref[...])
    lora_a = _dot(_dot(mix(5), a1_ref[...]), a2_ref[...])
    lw_out[...] = (-EXP_NEG_HALF) * _sigmoid(w0_ref[...] + lora_w)
    iclr = _sigmoid(a0_ref[...] + lora_a)
    k_out[...] = (k * (1.0 + (iclr - 1.0) * ka_ref[...])).astype(k_out.dtype)
    kk = k * kk_ref[...]
    kk = kk * lax.rsqrt(jnp.maximum(_seg_sum(kk * kk, ones_ref[...]), 1e-24))
    kk_out[...] = kk.astype(kk_out.dtype)
    b_out[...] = (kk * iclr).astype(b_out.dtype)


def _rwkv_front(x1, pre_gain, mu, win, w0, w1, w2, a0, a1, a2, k_k, k_a, ones_bd, tm=512):
    bsz, seq, d = x1.shape
    tok = lambda: pl.BlockSpec((None, tm, d), lambda b, s: (b, s, 0))
    full = lambda a: pl.BlockSpec(a.shape, lambda b, s: (0,) * a.ndim)
    args = (pre_gain, mu, win, w0, w1, w2, a0, a1, a2, k_k, k_a, ones_bd)
    return pl.pallas_call(
        _rwkv_front_kernel,
        grid=(bsz, seq // tm),
        in_specs=[tok()] + [full(a) for a in args],
        out_specs=[tok() for _ in range(7)],
        out_shape=[jax.ShapeDtypeStruct((bsz, seq, d), F32 if i == 1 else BF16) for i in range(7)],
        scratch_shapes=[pltpu.VMEM((1, d), F32)],
        compiler_params=_params("parallel", "arbitrary"),
        name="l1_front",
    )(x1, *args)


def _rwkv_scan_kernel(r_ref, lw_ref, k_ref, v_ref, kk_ref, b_ref, y_ref,
                      ht_ref, rt_s, y0_s, ml_s, ha_s, gm_s):
    @pl.when(pl.program_id(2) == 0)
    def _():
        ht_ref[...] = jnp.zeros_like(ht_ref)

    ts, width = r_ref.shape
    nchunk = ts // CHUNK
    npair = width // LANES
    lanes_of = [slice(p * LANES, (p + 1) * LANES) for p in range(npair)]
    head0 = lax.broadcasted_iota(jnp.int32, (CHUNK, LANES), 1) < RW_HEAD
    row = lax.broadcasted_iota(jnp.int32, (CHUNK, LANES), 0)
    tok = lax.broadcasted_iota(jnp.int32, (CHUNK, LANES), 1) & (CHUNK - 1)
    strict = tok < row
    incl = tok <= row
    eye = jnp.where(tok == row, 1.0, 0.0)
    same_head = ((lax.broadcasted_iota(jnp.int32, (LANES, LANES), 0) < RW_HEAD)
                 == (lax.broadcasted_iota(jnp.int32, (LANES, LANES), 1) < RW_HEAD))

    def stack(x):
        return jnp.concatenate([jnp.where(head0, x, 0.0), jnp.where(head0, 0.0, x)], axis=0)

    def prep(c):
        rows = pl.ds(pl.multiple_of(c * CHUNK, CHUNK), CHUNK)
        r, lw, k, v, kk, b = (ref[rows, :].astype(F32)
                              for ref in (r_ref, lw_ref, k_ref, v_ref, kk_ref, b_ref))
        lcum = _cumsum_rows(lw)
        lend = lcum[CHUNK - 1:CHUNK, :]
        e_neg = jnp.exp(-lcum)
        e_end = jnp.exp(lend - lcum)
        gm_s[c] = jnp.exp(lend)
        at, rt = -kk * jnp.exp(lcum - lw), r * jnp.exp(lcum)
        bt, kt, bh, kh = b * e_neg, k * e_neg, b * e_end, k * e_end
        return [dict(rows=rows, c=c, p=p, sl=sl,
                     art=jnp.concatenate([at[:, sl], rt[:, sl]], axis=0).astype(BF16),
                     at2=stack(at[:, sl]).astype(BF16),
                     rt=rt[:, sl],
                     bk2=jnp.concatenate([stack(bt[:, sl]), stack(kt[:, sl])], axis=0).astype(BF16),
                     v=v[:, sl].astype(BF16),
                     v2=stack(v[:, sl]).astype(BF16),
                     bh=bh[:, sl].astype(BF16),
                     bkh=jnp.concatenate([bh[:, sl], kh[:, sl]], axis=0).astype(BF16))
                for p, sl in enumerate(lanes_of)]

    def local_group(i, between=()):
        pending = list(between)
        passed = [0]

        def boundary():
            fired = len(between) - len(pending)
            if pending and passed[0] * len(between) >= fired * SCAN_STAGES:
                pending.pop(0)()
            passed[0] += 1

        g = [q for j in range(SCAN_GROUP) for q in prep(i * SCAN_GROUP + j)]
        s = [_dot_nt(q["art"], q["bk2"]) for q in g]
        boundary()
        n = [jnp.where(strict, x[:CHUNK, :LANES], 0.0) for x in s]
        a_ak = [jnp.where(strict, x[:CHUNK, LANES:], 0.0).astype(BF16) for x in s]
        a_rbk = [jnp.concatenate([jnp.where(incl, x[CHUNK:, :LANES], 0.0),
                                  jnp.where(incl, x[CHUNK:, LANES:], 0.0)], axis=1).astype(BF16) for x in s]
        t = [eye + x for x in n]
        pw = [_dot(x, stack(x)) for x in n]
        boundary()
        d = 2
        while d < CHUNK // 2:
            m = [_dot(jnp.concatenate([a, b], axis=0), stack(b)) for a, b in zip(t, pw)]
            t = [a + x[:CHUNK] for a, x in zip(t, m)]
            pw = [x[CHUNK:] for x in m]
            boundary()
            d *= 2
        t = [a + _dot(a, stack(b)) for a, b in zip(t, pw)]
        boundary()
        akv = [_dot(x, q["v2"]) for x, q in zip(a_ak, g)]
        tw = [_dot(x, jnp.concatenate([q["at2"], stack(y).astype(BF16)], axis=1))
              for x, y, q in zip(t, akv, g)]
        boundary()
        wt = [x[:, :LANES] for x in tw]
        ut = [x[:, LANES:] for x in tw]
        rw = [_dot(x[:, :LANES], stack(y)) for x, y in zip(a_rbk, wt)]
        y0 = [_dot(x, jnp.concatenate([stack(y).astype(BF16), q["v2"]], axis=0))
              for x, y, q in zip(a_rbk, ut, g)]
        ml = [_dot_tn(q["bh"], y) for q, y in zip(g, wt)]
        ha = [_dot_tn(jnp.concatenate([y.astype(BF16), q["v"]], axis=0), q["bkh"]) for q, y in zip(g, ut)]
        while pending:
            pending.pop(0)()
        for j, q in enumerate(g):
            rt_s[q["rows"], q["sl"]] = (q["rt"] + rw[j]).astype(BF16)
            y0_s[q["rows"], q["sl"]] = y0[j]
            ml_s[q["c"], q["p"]] = jnp.where(same_head, ml[j], 0.0).astype(BF16)
            ha_s[q["c"], q["p"]] = jnp.where(same_head, ha[j], 0.0)

    def state_step(c, hts):
        rows = pl.ds(pl.multiple_of(c * CHUNK, CHUNK), CHUNK)
        rt, gm = rt_s[rows, :], gm_s[c]
        htb = [h.astype(BF16) for h in hts]
        ys = [_dot_nt(rt[:, sl], hb) for sl, hb in zip(lanes_of, htb)]
        new = [h * gm[:, sl] + _dot_nt(hb, ml_s[c, p]) + ha_s[c, p]
               for p, (sl, h, hb) in enumerate(zip(lanes_of, hts, htb))]
        y_ref[rows, :] = (jnp.concatenate(ys, axis=1) + y0_s[rows, :]).astype(y_ref.dtype)
        return tuple(new)

    def pipelined(i, hts):
        box = [hts]

        def advance(j):
            def run():
                box[0] = state_step((i - 1) * SCAN_GROUP + j, box[0])
            return run

        local_group(i, between=[advance(j) for j in range(SCAN_GROUP)])
        return box[0]

    ngroup = nchunk // SCAN_GROUP
    local_group(0)
    hts = lax.fori_loop(1, ngroup, pipelined, tuple(ht_ref[p] for p in range(npair)))
    for j in range(SCAN_GROUP):
        hts = state_step((ngroup - 1) * SCAN_GROUP + j, hts)
    for p in range(npair):
        ht_ref[p] = hts[p]


def _rwkv_scan(r, lw, k, v, kk, b, ts=1024, width=512):
    bsz, seq, d = r.shape
    nchunk = ts // CHUNK
    npair = width // LANES
    spec = lambda: pl.BlockSpec((None, ts, width), lambda bb, h, s: (bb, s, h))
    return pl.pallas_call(
        _rwkv_scan_kernel,
        grid=(bsz, d // width, seq // ts),
        in_specs=[spec() for _ in range(6)],
        out_specs=spec(),
        out_shape=jax.ShapeDtypeStruct((bsz, seq, d), BF16),
        scratch_shapes=[pltpu.VMEM((npair, LANES, LANES), F32),
                        pltpu.VMEM((ts, width), BF16),
                        pltpu.VMEM((ts, width), F32),
                        pltpu.VMEM((nchunk, npair, LANES, LANES), BF16),
                        pltpu.VMEM((nchunk, npair, LANES, LANES), F32),
                        pltpu.VMEM((nchunk, 1, width), F32)],
        compiler_params=_params("parallel", "parallel", "arbitrary"),
        name="l1_scan",
    )(r, lw, k, v, kk, b)


def _rwkv_back_kernel(y_ref, r_ref, k_ref, v_ref, g_ref, x_ref, rk_ref, lnw_ref, lnb_ref,
                      w_ref, pn_ref, ones_ref, o_ref):
    ones = ones_ref[...]
    inv_n = 1.0 / RW_HEAD
    y = y_ref[...].astype(F32)
    yc = y - _seg_sum(y, ones, split=False) * inv_n
    var = _seg_sum(yc * yc, ones, split=False) * inv_n
    yn = yc * lax.rsqrt(var + RW_GN_EPS) * lnw_ref[...] + lnb_ref[...]
    rk = r_ref[...].astype(F32) * k_ref[...].astype(F32) * rk_ref[...]
    yn = yn + _seg_sum(rk, ones, split=False) * v_ref[...].astype(F32)
    out = _dot(yn * _silu(g_ref[...].astype(F32)), w_ref[...])
    o_ref[...] = x_ref[...] + _rms(out, pn_ref[...])


def _rwkv_back(y, r, k, v, g, x1, r_k, ln_w, ln_b, w_bf16, post_gain, ones_bd, tm=512):
    t, d = y.shape
    tok = lambda: pl.BlockSpec((tm, d), lambda i: (i, 0))
    full = lambda a: pl.BlockSpec(a.shape, lambda i: (0,) * a.ndim)
    params = (r_k, ln_w, ln_b, w_bf16, post_gain, ones_bd)
    return pl.pallas_call(
        _rwkv_back_kernel,
        grid=(t // tm,),
        in_specs=[tok() for _ in range(6)] + [full(a) for a in params],
        out_specs=tok(),
        out_shape=jax.ShapeDtypeStruct((t, d), F32),
        compiler_params=_params("parallel"),
        name="l1_back",
    )(y, r, k, v, g, x1, *params)


def _block_diag(w, group):
    n, bi, bj = w.shape
    w = w.reshape(n // group, group, bi, bj)
    eye = jnp.eye(group, dtype=w.dtype)
    return jnp.einsum("gaij,ab->gaibj", w, eye).reshape(n // group, group * bi, group * bj)


def kernel(x, pre_norm, post_norm, ab_w_in, ab_w_out, rg_conv_w, rg_conv_b, rg_w_a, rg_b_a,
           rg_w_x, rg_b_x, rg_lambda, hg_lower_bounds, hg_out_norm, rw_mu, rw_w_in, rw_w_out,
           rw_w0, rw_w1, rw_w2, rw_a0, rw_a1, rw_a2, rw_k_k, rw_k_a, rw_r_k, rw_ln_w, rw_ln_b):
    bsz, seq, d = x.shape
    t = bsz * seq
    row = lambda a: a.reshape(1, -1).astype(F32)
    x2d = x.reshape(t, d)
    heads_per_tile = MXU_DIM // RW_HEAD
    ones_bd = _block_diag(jnp.ones((heads_per_tile, RW_HEAD, RW_HEAD), F32), heads_per_tile)[0].astype(BF16)

    w_in = ab_w_in[0].astype(BF16)
    rg_splits = 2
    proj, u0 = _norm_proj(x2d, row(pre_norm[0]), w_in, rg_splits * d)
    proj4 = proj.reshape(proj.shape[0], bsz, seq, proj.shape[-1])
    rg_group = MXU_DIM // rg_w_a.shape[-1]
    hg = _rglru(proj4, rg_conv_w[0].astype(F32), row(rg_conv_b[0]),
                _block_diag(rg_w_a[0], rg_group).astype(BF16), row(rg_b_a[0]),
                _block_diag(rg_w_x[0], rg_group).astype(BF16), row(rg_b_x[0]), row(rg_lambda[0]))
    og = _hgrn2(u0.reshape(bsz, seq, d), w_in, rg_splits, hg_lower_bounds.astype(F32),
                row(hg_out_norm[0]))
    x1 = _out_proj0(hg.reshape(t, d), og.reshape(t, d), ab_w_out[0].astype(BF16),
                    row(post_norm[0]), x2d)

    r, lw, k, v, kk, b, g = _rwkv_front(
        x1.reshape(bsz, seq, d), row(pre_norm[1]), rw_mu[0].astype(F32), rw_w_in[0].astype(BF16),
        row(rw_w0[0]), rw_w1[0].astype(BF16), rw_w2[0].astype(BF16), row(rw_a0[0]),
        rw_a1[0].astype(BF16), rw_a2[0].astype(BF16), row(rw_k_k[0]), row(rw_k_a[0]), ones_bd)
    y = _rwkv_scan(r, lw, k, v, kk, b)
    flat = lambda a: a.reshape(t, d)
    x2 = _rwkv_back(flat(y), flat(r), flat(k), flat(v), flat(g), x1, row(rw_r_k[0]),
                    row(rw_ln_w[0]), row(rw_ln_b[0]), rw_w_out[0].astype(BF16),
                    row(post_norm[1]), ones_bd)
    return x2.reshape(bsz, seq, d)
```

```python
import functools

import jax
import jax.numpy as jnp
from jax import lax
from jax.experimental import pallas as pl
from jax.experimental.pallas import tpu as pltpu

F32 = jnp.float32
BF16 = jnp.bfloat16

D_MODEL = 1024
RMS_EPS = 1e-6
RG_C = 8.0
RG_CONV = 4
HG_HEADS = 8
HG_DK = 128
RW_HEAD = 64
RW_GN_EPS = 64e-5

LANES = 128
SUBLANES = 8
MXU_DIM = 256
VMEM_LIMIT = 56 * 1024 * 1024

CHUNK = 64
SCAN_GROUP = 4
HG_GROUP = 16
SCAN_STAGES = 8
EXP_NEG_HALF = 0.6065306597126334


def _params(*sem):
    return pltpu.CompilerParams(dimension_semantics=sem, vmem_limit_bytes=VMEM_LIMIT)


def _dot(a, b):
    return jnp.dot(a.astype(BF16), b.astype(BF16), preferred_element_type=F32)


def _dot_nt(a, b):
    return lax.dot_general(a.astype(BF16), b.astype(BF16), (((1,), (1,)), ((), ())),
                           preferred_element_type=F32)


def _dot_tn(a, b):
    return lax.dot_general(a.astype(BF16), b.astype(BF16), (((0,), (0,)), ((), ())),
                           preferred_element_type=F32)


def _sigmoid(x):
    return 0.5 * jnp.tanh(0.5 * x) + 0.5


def _silu(x):
    return x * _sigmoid(x)


def _softplus(z):
    return jnp.maximum(z, 0.0) + jnp.log(1.0 + jnp.exp(-jnp.abs(z)))


def _rms(x, gain):
    return x * lax.rsqrt(jnp.mean(x * x, axis=-1, keepdims=True) + RMS_EPS) * gain


def _cumsum_rows(x):
    n = x.shape[0]
    row = lax.broadcasted_iota(jnp.int32, x.shape, 0)
    d = 1
    while d < n:
        x = x + jnp.where(row >= d, pltpu.roll(x, d, 0), 0.0)
        d *= 2
    return x


def _seg_sum(x, ones_bd, split=True):
    hi = x.astype(BF16)
    lo = (x - hi.astype(F32)).astype(BF16) if split else None
    outs = []
    for g in range(x.shape[1] // MXU_DIM):
        sl = slice(g * MXU_DIM, (g + 1) * MXU_DIM)
        acc = jnp.dot(hi[:, sl], ones_bd, preferred_element_type=F32)
        if split:
            acc = acc + jnp.dot(lo[:, sl], ones_bd, preferred_element_type=F32)
        outs.append(acc)
    return jnp.concatenate(outs, axis=-1)


def _norm_proj_kernel(x_ref, g_ref, w_ref, o_ref, u_ref):
    @pl.when(pl.program_id(1) == 0)
    def _():
        u_ref[...] = _rms(x_ref[...], g_ref[...]).astype(BF16)

    o_ref[...] = jnp.dot(u_ref[...], w_ref[...], preferred_element_type=F32).astype(o_ref.dtype)


def _norm_proj(x2d, gain, w_bf16, n, tm=1024, tn=2048):
    t, d = x2d.shape
    return pl.pallas_call(
        _norm_proj_kernel,
        grid=(t // tm, n // tn),
        in_specs=[pl.BlockSpec((tm, d), lambda i, j: (i, 0)),
                  pl.BlockSpec((1, d), lambda i, j: (0, 0)),
                  pl.BlockSpec((d, tn), lambda i, j: (0, j))],
        out_specs=[pl.BlockSpec((None, tm, tn), lambda i, j: (j, i, 0)),
                   pl.BlockSpec((tm, d), lambda i, j: (i, 0))],
        out_shape=[jax.ShapeDtypeStruct((n // tn, t, tn), BF16),
                   jax.ShapeDtypeStruct((t, d), BF16)],
        compiler_params=_params("parallel", "arbitrary"),
        name="l0_norm_proj",
    )(x2d, gain, w_bf16)


def _rglru_kernel(x_ref, g_ref, cw_ref, cb_ref, wa_ref, ba_ref, wx_ref, bx_ref, lam_ref,
                  o_ref, h_ref, ext_ref, a_ref, b_ref):
    @pl.when(pl.program_id(1) == 0)
    def _():
        h_ref[...] = jnp.zeros_like(h_ref)
        ext_ref[:SUBLANES, :] = jnp.zeros((SUBLANES, ext_ref.shape[1]), F32)

    x = x_ref[...].astype(F32)
    ts, width = x.shape
    ext_ref[SUBLANES:, :] = x
    xc = cb_ref[...] + cw_ref[RG_CONV - 1:RG_CONV, :] * x
    for k in range(1, RG_CONV):
        xc = xc + cw_ref[RG_CONV - 1 - k:RG_CONV - k, :] * ext_ref[SUBLANES - k:SUBLANES - k + ts, :]
    ext_ref[:SUBLANES, :] = x[ts - SUBLANES:, :]

    xcb = xc.astype(BF16)
    pre_r, pre_i = [], []
    for g in range(width // MXU_DIM):
        sl = slice(g * MXU_DIM, (g + 1) * MXU_DIM)
        pre_r.append(jnp.dot(xcb[:, sl], wa_ref[g], preferred_element_type=F32))
        pre_i.append(jnp.dot(xcb[:, sl], wx_ref[g], preferred_element_type=F32))
    gate_r = _sigmoid(jnp.concatenate(pre_r, axis=-1) + ba_ref[...])
    gate_i = _sigmoid(jnp.concatenate(pre_i, axis=-1) + bx_ref[...])

    log_a = (-RG_C) * gate_r * _softplus(-lam_ref[...])
    a = jnp.exp(log_a)
    b = jnp.sqrt(1.0 - a * a) * (gate_i * xc)

    a = a.reshape(ts // SUBLANES, SUBLANES, width)
    b = b.reshape(ts // SUBLANES, SUBLANES, width)
    in_group = lax.broadcasted_iota(jnp.int32, (1, SUBLANES, 1), 1)
    d = 1
    while d < SUBLANES:
        m = in_group >= d
        b = jnp.where(m, a, 0.0) * pltpu.roll(b, d, 1) + b
        a = a * jnp.where(m, pltpu.roll(a, d, 1), 1.0)
        d *= 2
    a_ref[...] = a.reshape(ts, width)
    b_ref[...] = b.reshape(ts, width)

    def carry_group(i, h):
        rows = pl.ds(pl.multiple_of(i * SUBLANES, SUBLANES), SUBLANES)
        hg = a_ref[rows, :] * h + b_ref[rows, :]
        b_ref[rows, :] = hg
        return hg[SUBLANES - 1:, :]

    h_ref[...] = lax.fori_loop(0, ts // SUBLANES, carry_group, h_ref[...], unroll=8)
    o_ref[...] = (b_ref[...] * _silu(g_ref[...].astype(F32))).astype(o_ref.dtype)


def _proj_block(proj4, split, block, offset=0):
    per_slab = proj4.shape[-1] // block
    col = split * (D_MODEL // block) + offset
    return col // per_slab, col % per_slab


def _rglru(proj4, conv_w, conv_b, wa_bd, b_a, wx_bd, b_x, lam, ts=512):
    _, bsz, seq, _ = proj4.shape
    width = D_MODEL
    row = lambda: pl.BlockSpec((1, width), lambda b, s: (0, 0))

    def split_spec(split):
        slab, blk = _proj_block(proj4, split, width)
        return pl.BlockSpec((None, None, ts, width), lambda b, s: (slab, b, s, blk))

    return pl.pallas_call(
        _rglru_kernel,
        grid=(bsz, seq // ts),
        in_specs=[split_spec(0), split_spec(1),
                  pl.BlockSpec((RG_CONV, width), lambda b, s: (0, 0)),
                  row(),
                  pl.BlockSpec(wa_bd.shape, lambda b, s: (0, 0, 0)),
                  row(),
                  pl.BlockSpec(wx_bd.shape, lambda b, s: (0, 0, 0)),
                  row(), row()],
        out_specs=pl.BlockSpec((None, ts, width), lambda b, s: (b, s, 0)),
        out_shape=jax.ShapeDtypeStruct((bsz, seq, width), BF16),
        scratch_shapes=[pltpu.VMEM((1, width), F32), pltpu.VMEM((SUBLANES + ts, width), F32),
                        pltpu.VMEM((ts, width), F32), pltpu.VMEM((ts, width), F32)],
        compiler_params=_params("parallel", "arbitrary"),
        name="l0_rglru",
    )(proj4, proj4, conv_w, conv_b, wa_bd, b_a, wx_bd, b_x, lam)


def _hgrn2_kernel(u_ref, wq_ref, wf_ref, wv_ref, wg_ref, lbp_ref, gn_ref, o_ref, st_ref, p_ref):
    @pl.when(pl.program_id(2) == 0)
    def _():
        st_ref[...] = jnp.zeros_like(st_ref)

    group_rows = CHUNK * HG_GROUP
    col = lambda j: slice(j * HG_DK, (j + 1) * HG_DK)
    w_halves = (jnp.concatenate([wq_ref[...], wf_ref[...]], axis=1),
                jnp.concatenate([wv_ref[...], wg_ref[...]], axis=1))

    def project(g, slot, half):
        rows = pl.ds(pl.multiple_of(g * group_rows, group_rows), group_rows)
        cols = slice(half * 2 * HG_DK, (half + 1) * 2 * HG_DK)
        p_ref[slot, :, cols] = jnp.dot(u_ref[rows, :], w_halves[half], preferred_element_type=F32)

    lbp = lbp_ref[...]
    e = jnp.exp(lbp - jnp.max(lbp, axis=0, keepdims=True))
    p = e / jnp.sum(e, axis=0, keepdims=True)
    lb = (p[0:1] + p[1:2]) - p[0:1]
    gn = gn_ref[...]
    ngroup = u_ref.shape[0] // group_rows
    srow = lax.broadcasted_iota(jnp.int32, (CHUNK, CHUNK), 0)
    scol = lax.broadcasted_iota(jnp.int32, (CHUNK, CHUNK), 1)
    levels = []
    h = CHUNK // 2
    while h >= 1:
        levels.append((h, (srow & ~(2 * h - 1)) == (scol & ~(2 * h - 1))))
        h //= 2
    sub_row = lax.broadcasted_iota(jnp.int32, (SUBLANES, 1), 0)
    chunk_row = lax.broadcasted_iota(jnp.int32, (CHUNK, 1), 0)
    tril3 = jnp.where(srow >= scol, 1.0, 0.0).astype(BF16)
    tril3 = jnp.concatenate([tril3, tril3, tril3], axis=1)

    def cumsum_rows(x):
        hi = x.astype(BF16)
        r1 = x - hi.astype(F32)
        mid = r1.astype(BF16)
        lo = (r1 - mid.astype(F32)).astype(BF16)
        return jnp.dot(tril3, jnp.concatenate([hi, mid, lo], axis=0), preferred_element_type=F32)

    def block_rows(x, size, r):
        out = []
        for lo in range(0, CHUNK, SUBLANES):
            acc = jnp.broadcast_to(x[lo + r:lo + r + 1], (SUBLANES, LANES))
            for off in range(size, SUBLANES, size):
                acc = jnp.where(sub_row >= off,
                                jnp.broadcast_to(x[lo + off + r:lo + off + r + 1], (SUBLANES, LANES)), acc)
            out.append(acc)
        return jnp.concatenate(out, axis=0)

    def level_factors(q, k, f, g2, h):
        if h < SUBLANES:
            upper = (chunk_row & (2 * h - 1)) >= h
            if h == 1:
                return jnp.where(upper, q * f, 0.0), jnp.where(upper, 0.0, k)
            d = g2 - block_rows(g2, 2 * h, h - 1)
            e = jnp.exp2(jnp.where(upper, d, -d))
            return jnp.where(upper, q * e, 0.0), jnp.where(upper, 0.0, k * e)
        zero = jnp.zeros((h, LANES), F32)
        qt, kt = [], []
        for lo in range(0, CHUNK, 2 * h):
            mid = lo + h
            rho = g2[mid - 1:mid, :]
            kt += [k[lo:mid] * jnp.exp2(rho - g2[lo:mid]), zero]
            qt += [zero, q[mid:mid + h] * jnp.exp2(g2[mid:mid + h] - rho)]
        return jnp.concatenate(qt, axis=0), jnp.concatenate(kt, axis=0)

    def group(i, carry):
        idx = range(HG_GROUP)
        slot = i % 2
        nxt = jnp.minimum(i + 1, ngroup - 1)
        local = [pl.ds(j * CHUNK, CHUNK) for j in idx]
        rows = [pl.ds(pl.multiple_of((i * HG_GROUP + j) * CHUNK, CHUNK), CHUNK) for j in idx]
        q = [_silu(p_ref[slot, r, col(0)]) for r in local]
        sig = [1.0 / (1.0 + jnp.exp(-p_ref[slot, r, col(1)])) for r in local]
        v = [p_ref[slot, r, col(2)].astype(BF16) for r in local]
        gate = [p_ref[slot, r, col(3)] for r in local]
        k = [(1.0 - lb) * (1.0 - x) for x in sig]
        f = [lb + (1.0 - lb) * x for x in sig]
        g2 = [cumsum_rows(jnp.log2(x)) for x in f]
        project(nxt, 1 - slot, 0)
        gend = [x[CHUNK - 1:CHUNK, :] for x in g2]
        qg = [(a * jnp.exp2(b)).astype(BF16) for a, b in zip(q, g2)]
        vtk = [_dot_tn(a, b * jnp.exp2(e - c)) for a, b, c, e in zip(v, k, g2, gend)]

        scores = [jnp.where(srow == scol, jnp.sum(a * b, axis=-1, keepdims=True), 0.0)
                  for a, b in zip(q, k)]
        for h, same_block in levels:
            for j in idx:
                qt, kt = level_factors(q[j], k[j], f[j], g2[j], h)
                scores[j] = scores[j] + jnp.where(same_block, _dot_nt(qt, kt), 0.0)
        out = [_dot(a, b) for a, b in zip(scores, v)]

        st = st_ref[...]
        for j in idx:
            out[j] = out[j] + _dot_nt(qg[j], st)
            st = st * jnp.exp2(gend[j]) + vtk[j]
        st_ref[...] = st
        project(nxt, 1 - slot, 1)
        for j in idx:
            o_ref[rows[j], :] = (_rms(out[j], gn) * _silu(gate[j])).astype(o_ref.dtype)
        return carry

    project(0, 0, 0)
    project(0, 0, 1)
    lax.fori_loop(0, ngroup, group, 0)


def _hgrn2(u3, w_bf16, first_split, lb_params, g_norm, ts=8192):
    bsz, seq, d = u3.shape

    def w_spec(split):
        base = split * (d // HG_DK)
        return pl.BlockSpec((d, HG_DK), lambda b, h, s: (0, base + h))

    return pl.pallas_call(
        _hgrn2_kernel,
        grid=(bsz, HG_HEADS, seq // ts),
        in_specs=[pl.BlockSpec((None, ts, d), lambda b, h, s: (b, s, 0))]
        + [w_spec(first_split + j) for j in range(4)]
        + [pl.BlockSpec((lb_params.shape[0], HG_DK), lambda b, h, s: (0, h)),
           pl.BlockSpec((1, HG_DK), lambda b, h, s: (0, 0))],
        out_specs=pl.BlockSpec((None, ts, HG_DK), lambda b, h, s: (b, s, h)),
        out_shape=jax.ShapeDtypeStruct((bsz, seq, d), BF16),
        scratch_shapes=[pltpu.VMEM((HG_DK, HG_DK), F32),
                        pltpu.VMEM((2, CHUNK * HG_GROUP, 4 * HG_DK), F32)],
        compiler_params=_params("parallel", "parallel", "arbitrary"),
        name="l0_hgrn2",
    )(u3, w_bf16, w_bf16, w_bf16, w_bf16, lb_params, g_norm)


def _out_proj0_kernel(hg_ref, og_ref, w_ref, pn_ref, x_ref, o_ref):
    half = hg_ref.shape[1]
    y = (jnp.dot(hg_ref[...], w_ref[:half, :], preferred_element_type=F32)
         + jnp.dot(og_ref[...], w_ref[half:, :], preferred_element_type=F32))
    o_ref[...] = x_ref[...] + _rms(y, pn_ref[...])


def _out_proj0(hg, og, w_bf16, post_gain, x2d, tm=1024):
    t, d = x2d.shape
    tok = lambda: pl.BlockSpec((tm, d), lambda i: (i, 0))
    return pl.pallas_call(
        _out_proj0_kernel,
        grid=(t // tm,),
        in_specs=[tok(), tok(),
                  pl.BlockSpec(w_bf16.shape, lambda i: (0, 0)),
                  pl.BlockSpec((1, d), lambda i: (0, 0)),
                  tok()],
        out_specs=tok(),
        out_shape=jax.ShapeDtypeStruct((t, d), F32),
        compiler_params=_params("parallel"),
        name="l0_out_proj",
    )(hg, og, w_bf16, post_gain, x2d)


def _rwkv_front_kernel(x_ref, pn_ref, mu_ref, win_ref, w0_ref, w1_ref, w2_ref, a0_ref, a1_ref,
                       a2_ref, kk_ref, ka_ref, ones_ref,
                       r_out, lw_out, k_out, v_out, kk_out, b_out, g_out, prev_ref):
    @pl.when(pl.program_id(1) == 0)
    def _():
        prev_ref[...] = jnp.zeros_like(prev_ref)

    u = _rms(x_ref[...], pn_ref[...])
    tm = u.shape[0]
    row = lax.broadcasted_iota(jnp.int32, (tm, 1), 0)
    delta = jnp.where(row == 0, prev_ref[...], pltpu.roll(u, 1, 0)) - u
    prev_ref[...] = u[tm - 1:tm, :]
    mix = lambda p: u + delta * mu_ref[p:p + 1, :]

    r_out[...] = _dot(mix(0), win_ref[0]).astype(r_out.dtype)
    k = _dot(mix(1), win_ref[1])
    v_out[...] = _dot(mix(2), win_ref[2]).astype(v_out.dtype)
    g_out[...] = _dot(mix(3), win_ref[3]).astype(g_out.dtype)
    lora_w = _dot(jnp.tanh(_dot(mix(4), w1_ref[...])), w2_ref[...])
    lora_a = _dot(_dot(mix(5), a1_ref[...]), a2_ref[...])
    lw_out[...] = (-EXP_NEG_HALF) * _sigmoid(w0_ref[...] + lora_w)
    iclr = _sigmoid(a0_ref[...] + lora_a)
    k_out[...] = (k * (1.0 + (iclr - 1.0) * ka_ref[...])).astype(k_out.dtype)
    kk = k * kk_ref[...]
    kk = kk * lax.rsqrt(jnp.maximum(_seg_sum(kk * kk, ones_ref[...]), 1e-24))
    kk_out[...] = kk.astype(kk_out.dtype)
    b_out[...] = (kk * iclr).astype(b_out.dtype)


def _rwkv_front(x1, pre_gain, mu, win, w0, w1, w2, a0, a1, a2, k_k, k_a, ones_bd, tm=512):
    bsz, seq, d = x1.shape
    tok = lambda: pl.BlockSpec((None, tm, d), lambda b, s: (b, s, 0))
    full = lambda a: pl.BlockSpec(a.shape, lambda b, s: (0,) * a.ndim)
    args = (pre_gain, mu, win, w0, w1, w2, a0, a1, a2, k_k, k_a, ones_bd)
    return pl.pallas_call(
        _rwkv_front_kernel,
        grid=(bsz, seq // tm),
        in_specs=[tok()] + [full(a) for a in args],
        out_specs=[tok() for _ in range(7)],
        out_shape=[jax.ShapeDtypeStruct((bsz, seq, d), F32 if i == 1 else BF16) for i in range(7)],
        scratch_shapes=[pltpu.VMEM((1, d), F32)],
        compiler_params=_params("parallel", "arbitrary"),
        name="l1_front",
    )(x1, *args)


def _rwkv_scan_kernel(r_ref, lw_ref, k_ref, v_ref, kk_ref, b_ref, y_ref,
                      ht_ref, rt_s, y0_s, ml_s, ha_s, gm_s):
    @pl.when(pl.program_id(2) == 0)
    def _():
        ht_ref[...] = jnp.zeros_like(ht_ref)

    ts, width = r_ref.shape
    nchunk = ts // CHUNK
    npair = width // LANES
    lanes_of = [slice(p * LANES, (p + 1) * LANES) for p in range(npair)]
    head0 = lax.broadcasted_iota(jnp.int32, (CHUNK, LANES), 1) < RW_HEAD
    row = lax.broadcasted_iota(jnp.int32, (CHUNK, LANES), 0)
    tok = lax.broadcasted_iota(jnp.int32, (CHUNK, LANES), 1) & (CHUNK - 1)
    strict = tok < row
    incl = tok <= row
    eye = jnp.where(tok == row, 1.0, 0.0)
    same_head = ((lax.broadcasted_iota(jnp.int32, (LANES, LANES), 0) < RW_HEAD)
                 == (lax.broadcasted_iota(jnp.int32, (LANES, LANES), 1) < RW_HEAD))

    def stack(x):
        return jnp.concatenate([jnp.where(head0, x, 0.0), jnp.where(head0, 0.0, x)], axis=0)

    def prep(c):
        rows = pl.ds(pl.multiple_of(c * CHUNK, CHUNK), CHUNK)
        r, lw, k, v, kk, b = (ref[rows, :].astype(F32)
                              for ref in (r_ref, lw_ref, k_ref, v_ref, kk_ref, b_ref))
        lcum = _cumsum_rows(lw)
        lend = lcum[CHUNK - 1:CHUNK, :]
        e_neg = jnp.exp(-lcum)
        e_end = jnp.exp(lend - lcum)
        gm_s[c] = jnp.exp(lend)
        at, rt = -kk * jnp.exp(lcum - lw), r * jnp.exp(lcum)
        bt, kt, bh, kh = b * e_neg, k * e_neg, b * e_end, k * e_end
        return [dict(rows=rows, c=c, p=p, sl=sl,
                     art=jnp.concatenate([at[:, sl], rt[:, sl]], axis=0).astype(BF16),
                     at2=stack(at[:, sl]).astype(BF16),
                     rt=rt[:, sl],
                     bk2=jnp.concatenate([stack(bt[:, sl]), stack(kt[:, sl])], axis=0).astype(BF16),
                     v=v[:, sl].astype(BF16),
                     v2=stack(v[:, sl]).astype(BF16),
                     bh=bh[:, sl].astype(BF16),
                     bkh=jnp.concatenate([bh[:, sl], kh[:, sl]], axis=0).astype(BF16))
                for p, sl in enumerate(lanes_of)]

    def local_group(i, between=()):
        pending = list(between)
        passed = [0]

        def boundary():
            fired = len(between) - len(pending)
            if pending and passed[0] * len(between) >= fired * SCAN_STAGES:
                pending.pop(0)()
            passed[0] += 1

        g = [q for j in range(SCAN_GROUP) for q in prep(i * SCAN_GROUP + j)]
        s = [_dot_nt(q["art"], q["bk2"]) for q in g]
        boundary()
        n = [jnp.where(strict, x[:CHUNK, :LANES], 0.0) for x in s]
        a_ak = [jnp.where(strict, x[:CHUNK, LANES:], 0.0).astype(BF16) for x in s]
        a_rbk = [jnp.concatenate([jnp.where(incl, x[CHUNK:, :LANES], 0.0),
                                  jnp.where(incl, x[CHUNK:, LANES:], 0.0)], axis=1).astype(BF16) for x in s]
        t = [eye + x for x in n]
        pw = [_dot(x, stack(x)) for x in n]
        boundary()
        d = 2
        while d < CHUNK // 2:
            m = [_dot(jnp.concatenate([a, b], axis=0), stack(b)) for a, b in zip(t, pw)]
            t = [a + x[:CHUNK] for a, x in zip(t, m)]
            pw = [x[CHUNK:] for x in m]
            boundary()
            d *= 2
        t = [a + _dot(a, stack(b)) for a, b in zip(t, pw)]
        boundary()
        akv = [_dot(x, q["v2"]) for x, q in zip(a_ak, g)]
        tw = [_dot(x, jnp.concatenate([q["at2"], stack(y).astype(BF16)], axis=1))
              for x, y, q in zip(t, akv, g)]
        boundary()
        wt = [x[:, :LANES] for x in tw]
        ut = [x[:, LANES:] for x in tw]
        rw = [_dot(x[:, :LANES], stack(y)) for x, y in zip(a_rbk, wt)]
        y0 = [_dot(x, jnp.concatenate([stack(y).astype(BF16), q["v2"]], axis=0))
              for x, y, q in zip(a_rbk, ut, g)]
        ml = [_dot_tn(q["bh"], y) for q, y in zip(g, wt)]
        ha = [_dot_tn(jnp.concatenate([y.astype(BF16), q["v"]], axis=0), q["bkh"]) for q, y in zip(g, ut)]
        while pending:
            pending.pop(0)()
        for j, q in enumerate(g):
            rt_s[q["rows"], q["sl"]] = (q["rt"] + rw[j]).astype(BF16)
            y0_s[q["rows"], q["sl"]] = y0[j]
            ml_s[q["c"], q["p"]] = jnp.where(same_head, ml[j], 0.0).astype(BF16)
            ha_s[q["c"], q["p"]] = jnp.where(same_head, ha[j], 0.0)

    def state_step(c, hts):
        rows = pl.ds(pl.multiple_of(c * CHUNK, CHUNK), CHUNK)
        rt, gm = rt_s[rows, :], gm_s[c]
        htb = [h.astype(BF16) for h in hts]
        ys = [_dot_nt(rt[:, sl], hb) for sl, hb in zip(lanes_of, htb)]
        new = [h * gm[:, sl] + _dot_nt(hb, ml_s[c, p]) + ha_s[c, p]
               for p, (sl, h, hb) in enumerate(zip(lanes_of, hts, htb))]
        y_ref[rows, :] = (jnp.concatenate(ys, axis=1) + y0_s[rows, :]).astype(y_ref.dtype)
        return tuple(new)

    def pipelined(i, hts):
        box = [hts]

        def advance(j):
            def run():
                box[0] = state_step((i - 1) * SCAN_GROUP + j, box[0])
            return run

        local_group(i, between=[advance(j) for j in range(SCAN_GROUP)])
        return box[0]

    ngroup = nchunk // SCAN_GROUP
    local_group(0)
    hts = lax.fori_loop(1, ngroup, pipelined, tuple(ht_ref[p] for p in range(npair)))
    for j in range(SCAN_GROUP):
        hts = state_step((ngroup - 1) * SCAN_GROUP + j, hts)
    for p in range(npair):
        ht_ref[p] = hts[p]


def _rwkv_scan(r, lw, k, v, kk, b, ts=1024, width=512):
    bsz, seq, d = r.shape
    nchunk = ts // CHUNK
    npair = width // LANES
    spec = lambda: pl.BlockSpec((None, ts, width), lambda bb, h, s: (bb, s, h))
    return pl.pallas_call(
        _rwkv_scan_kernel,
        grid=(bsz, d // width, seq // ts),
        in_specs=[spec() for _ in range(6)],
        out_specs=spec(),
        out_shape=jax.ShapeDtypeStruct((bsz, seq, d), BF16),
        scratch_shapes=[pltpu.VMEM((npair, LANES, LANES), F32),
                        pltpu.VMEM((ts, width), BF16),
                        pltpu.VMEM((ts, width), F32),
                        pltpu.VMEM((nchunk, npair, LANES, LANES), BF16),
                        pltpu.VMEM((nchunk, npair, LANES, LANES), F32),
                        pltpu.VMEM((nchunk, 1, width), F32)],
        compiler_params=_params("parallel", "parallel", "arbitrary"),
        name="l1_scan",
    )(r, lw, k, v, kk, b)


def _rwkv_back_kernel(y_ref, r_ref, k_ref, v_ref, g_ref, x_ref, rk_ref, lnw_ref, lnb_ref,
                      w_ref, pn_ref, ones_ref, o_ref):
    ones = ones_ref[...]
    inv_n = 1.0 / RW_HEAD
    y = y_ref[...].astype(F32)
    yc = y - _seg_sum(y, ones, split=False) * inv_n
    var = _seg_sum(yc * yc, ones, split=False) * inv_n
    yn = yc * lax.rsqrt(var + RW_GN_EPS) * lnw_ref[...] + lnb_ref[...]
    rk = r_ref[...].astype(F32) * k_ref[...].astype(F32) * rk_ref[...]
    yn = yn + _seg_sum(rk, ones, split=False) * v_ref[...].astype(F32)
    out = _dot(yn * _silu(g_ref[...].astype(F32)), w_ref[...])
    o_ref[...] = x_ref[...] + _rms(out, pn_ref[...])


def _rwkv_back(y, r, k, v, g, x1, r_k, ln_w, ln_b, w_bf16, post_gain, ones_bd, tm=512):
    t, d = y.shape
    tok = lambda: pl.BlockSpec((tm, d), lambda i: (i, 0))
    full = lambda a: pl.BlockSpec(a.shape, lambda i: (0,) * a.ndim)
    params = (r_k, ln_w, ln_b, w_bf16, post_gain, ones_bd)
    return pl.pallas_call(
        _rwkv_back_kernel,
        grid=(t // tm,),
        in_specs=[tok() for _ in range(6)] + [full(a) for a in params],
        out_specs=tok(),
        out_shape=jax.ShapeDtypeStruct((t, d), F32),
        compiler_params=_params("parallel"),
        name="l1_back",
    )(y, r, k, v, g, x1, *params)


def _block_diag(w, group):
    n, bi, bj = w.shape
    w = w.reshape(n // group, group, bi, bj)
    eye = jnp.eye(group, dtype=w.dtype)
    return jnp.einsum("gaij,ab->gaibj", w, eye).reshape(n // group, group * bi, group * bj)


def kernel(x, pre_norm, post_norm, ab_w_in, ab_w_out, rg_conv_w, rg_conv_b, rg_w_a, rg_b_a,
           rg_w_x, rg_b_x, rg_lambda, hg_lower_bounds, hg_out_norm, rw_mu, rw_w_in, rw_w_out,
           rw_w0, rw_w1, rw_w2, rw_a0, rw_a1, rw_a2, rw_k_k, rw_k_a, rw_r_k, rw_ln_w, rw_ln_b):
    bsz, seq, d = x.shape
    t = bsz * seq
    row = lambda a: a.reshape(1, -1).astype(F32)
    x2d = x.reshape(t, d)
    heads_per_tile = MXU_DIM // RW_HEAD
    ones_bd = _block_diag(jnp.ones((heads_per_tile, RW_HEAD, RW_HEAD), F32), heads_per_tile)[0].astype(BF16)

    w_in = ab_w_in[0].astype(BF16)
    rg_splits = 2
    proj, u0 = _norm_proj(x2d, row(pre_norm[0]), w_in, rg_splits * d)
    proj4 = proj.reshape(proj.shape[0], bsz, seq, proj.shape[-1])
    rg_group = MXU_DIM // rg_w_a.shape[-1]
    hg = _rglru(proj4, rg_conv_w[0].astype(F32), row(rg_conv_b[0]),
                _block_diag(rg_w_a[0], rg_group).astype(BF16), row(rg_b_a[0]),
                _block_diag(rg_w_x[0], rg_group).astype(BF16), row(rg_b_x[0]), row(rg_lambda[0]))
    og = _hgrn2(u0.reshape(bsz, seq, d), w_in, rg_splits, hg_lower_bounds.astype(F32),
                row(hg_out_norm[0]))
    x1 = _out_proj0(hg.reshape(t, d), og.reshape(t, d), ab_w_out[0].astype(BF16),
                    row(post_norm[0]), x2d)

    r, lw, k, v, kk, b, g = _rwkv_front(
        x1.reshape(bsz, seq, d), row(pre_norm[1]), rw_mu[0].astype(F32), rw_w_in[0].astype(BF16),
        row(rw_w0[0]), rw_w1[0].astype(BF16), rw_w2[0].astype(BF16), row(rw_a0[0]),
        rw_a1[0].astype(BF16), rw_a2[0].astype(BF16), row(rw_k_k[0]), row(rw_k_a[0]), ones_bd)
    y = _rwkv_scan(r, lw, k, v, kk, b)
    flat = lambda a: a.reshape(t, d)
    x2 = _rwkv_back(flat(y), flat(r), flat(k), flat(v), flat(g), x1, row(rw_r_k[0]),
                    row(rw_ln_w[0]), row(rw_ln_b[0]), rw_w_out[0].astype(BF16),
                    row(post_norm[1]), ones_bd)
    return x2.reshape(bsz, seq, d)
```

```python
import jax
import jax.numpy as jnp
from jax import lax
from jax.experimental import pallas as pl
from jax.experimental.pallas import tpu as pltpu

F32 = jnp.float32
BF16 = jnp.bfloat16

D_MODEL = 1024
RMS_EPS = 1e-6
RG_C = 8.0
RG_CONV = 4
HG_HEADS = 8
HG_DK = 128
RW_HEAD = 64
RW_GN_EPS = 64e-5

LANES = 128
SUBLANES = 8
MXU_DIM = 256
VMEM_LIMIT = 56 * 1024 * 1024

CHUNK = 64
SCAN_GROUP = 4
HG_GROUP = 16
SCAN_STAGES = 8
EXP_NEG_HALF = 0.6065306597126334


def _params(*sem):
    return pltpu.CompilerParams(dimension_semantics=sem, vmem_limit_bytes=VMEM_LIMIT)


def _dot(a, b):
    return jnp.dot(a.astype(BF16), b.astype(BF16), preferred_element_type=F32)


def _dot_nt(a, b):
    return lax.dot_general(a.astype(BF16), b.astype(BF16), (((1,), (1,)), ((), ())),
                           preferred_element_type=F32)


def _dot_tn(a, b):
    return lax.dot_general(a.astype(BF16), b.astype(BF16), (((0,), (0,)), ((), ())),
                           preferred_element_type=F32)


def _sigmoid(x):
    return 0.5 * jnp.tanh(0.5 * x) + 0.5


def _silu(x):
    return x * _sigmoid(x)


def _softplus(z):
    return jnp.maximum(z, 0.0) + jnp.log(1.0 + jnp.exp(-jnp.abs(z)))


def _rms(x, gain):
    return x * lax.rsqrt(jnp.mean(x * x, axis=-1, keepdims=True) + RMS_EPS) * gain


def _cumsum_rows(x):
    n = x.shape[0]
    row = lax.broadcasted_iota(jnp.int32, x.shape, 0)
    d = 1
    while d < n:
        x = x + jnp.where(row >= d, pltpu.roll(x, d, 0), 0.0)
        d *= 2
    return x


def _seg_sum(x, ones_bd):
    xb = x.astype(BF16)
    return jnp.concatenate(
        [jnp.dot(xb[:, g * MXU_DIM:(g + 1) * MXU_DIM], ones_bd, preferred_element_type=F32)
         for g in range(x.shape[1] // MXU_DIM)], axis=-1)


def _norm_proj_kernel(x_ref, g_ref, w_ref, o_ref, u_ref):
    @pl.when(pl.program_id(1) == 0)
    def _():
        u_ref[...] = _rms(x_ref[...], g_ref[...]).astype(BF16)

    o_ref[...] = jnp.dot(u_ref[...], w_ref[...], preferred_element_type=F32).astype(o_ref.dtype)


def _norm_proj(x2d, gain, w_bf16, n, tm=1024, tn=2048):
    t, d = x2d.shape
    return pl.pallas_call(
        _norm_proj_kernel,
        grid=(t // tm, n // tn),
        in_specs=[pl.BlockSpec((tm, d), lambda i, j: (i, 0)),
                  pl.BlockSpec((1, d), lambda i, j: (0, 0)),
                  pl.BlockSpec((d, tn), lambda i, j: (0, j))],
        out_specs=[pl.BlockSpec((None, tm, tn), lambda i, j: (j, i, 0)),
                   pl.BlockSpec((tm, d), lambda i, j: (i, 0))],
        out_shape=[jax.ShapeDtypeStruct((n // tn, t, tn), BF16),
                   jax.ShapeDtypeStruct((t, d), BF16)],
        compiler_params=_params("parallel", "arbitrary"),
        name="l0_norm_proj",
    )(x2d, gain, w_bf16)


def _rglru_kernel(x_ref, g_ref, cw_ref, cb_ref, wa_ref, ba_ref, wx_ref, bx_ref, lam_ref,
                  o_ref, h_ref, ext_ref, a_ref, b_ref):
    @pl.when(pl.program_id(1) == 0)
    def _():
        h_ref[...] = jnp.zeros_like(h_ref)
        ext_ref[:SUBLANES, :] = jnp.zeros((SUBLANES, ext_ref.shape[1]), F32)

    x = x_ref[...].astype(F32)
    ts, width = x.shape
    ext_ref[SUBLANES:, :] = x
    xc = cb_ref[...] + cw_ref[RG_CONV - 1:RG_CONV, :] * x
    for k in range(1, RG_CONV):
        xc = xc + cw_ref[RG_CONV - 1 - k:RG_CONV - k, :] * ext_ref[SUBLANES - k:SUBLANES - k + ts, :]
    ext_ref[:SUBLANES, :] = x[ts - SUBLANES:, :]

    xcb = xc.astype(BF16)
    pre_r, pre_i = [], []
    for g in range(width // MXU_DIM):
        sl = slice(g * MXU_DIM, (g + 1) * MXU_DIM)
        pre_r.append(jnp.dot(xcb[:, sl], wa_ref[g], preferred_element_type=F32))
        pre_i.append(jnp.dot(xcb[:, sl], wx_ref[g], preferred_element_type=F32))
    gate_r = _sigmoid(jnp.concatenate(pre_r, axis=-1) + ba_ref[...])
    gate_i = _sigmoid(jnp.concatenate(pre_i, axis=-1) + bx_ref[...])

    log_a = (-RG_C) * gate_r * _softplus(-lam_ref[...])
    a = jnp.exp(log_a)
    b = jnp.sqrt(1.0 - a * a) * (gate_i * xc)

    a = a.reshape(ts // SUBLANES, SUBLANES, width)
    b = b.reshape(ts // SUBLANES, SUBLANES, width)
    in_group = lax.broadcasted_iota(jnp.int32, (1, SUBLANES, 1), 1)
    d = 1
    while d < SUBLANES:
        m = in_group >= d
        b = jnp.where(m, a, 0.0) * pltpu.roll(b, d, 1) + b
        a = a * jnp.where(m, pltpu.roll(a, d, 1), 1.0)
        d *= 2
    a_ref[...] = a.reshape(ts, width)
    b_ref[...] = b.reshape(ts, width)

    def carry_group(i, h):
        rows = pl.ds(pl.multiple_of(i * SUBLANES, SUBLANES), SUBLANES)
        hg = a_ref[rows, :] * h + b_ref[rows, :]
        b_ref[rows, :] = hg
        return hg[SUBLANES - 1:, :]

    h_ref[...] = lax.fori_loop(0, ts // SUBLANES, carry_group, h_ref[...], unroll=8)
    o_ref[...] = (b_ref[...] * _silu(g_ref[...].astype(F32))).astype(o_ref.dtype)


def _proj_block(proj4, split, block, offset=0):
    per_slab = proj4.shape[-1] // block
    col = split * (D_MODEL // block) + offset
    return col // per_slab, col % per_slab


def _rglru(proj4, conv_w, conv_b, wa_bd, b_a, wx_bd, b_x, lam, ts=512):
    _, bsz, seq, _ = proj4.shape
    width = D_MODEL
    row = lambda: pl.BlockSpec((1, width), lambda b, s: (0, 0))

    def split_spec(split):
        slab, blk = _proj_block(proj4, split, width)
        return pl.BlockSpec((None, None, ts, width), lambda b, s: (slab, b, s, blk))

    return pl.pallas_call(
        _rglru_kernel,
        grid=(bsz, seq // ts),
        in_specs=[split_spec(0), split_spec(1),
                  pl.BlockSpec((RG_CONV, width), lambda b, s: (0, 0)),
                  row(),
                  pl.BlockSpec(wa_bd.shape, lambda b, s: (0, 0, 0)),
                  row(),
                  pl.BlockSpec(wx_bd.shape, lambda b, s: (0, 0, 0)),
                  row(), row()],
        out_specs=pl.BlockSpec((None, ts, width), lambda b, s: (b, s, 0)),
        out_shape=jax.ShapeDtypeStruct((bsz, seq, width), BF16),
        scratch_shapes=[pltpu.VMEM((1, width), F32), pltpu.VMEM((SUBLANES + ts, width), F32),
                        pltpu.VMEM((ts, width), F32), pltpu.VMEM((ts, width), F32)],
        compiler_params=_params("parallel", "arbitrary"),
        name="l0_rglru",
    )(proj4, proj4, conv_w, conv_b, wa_bd, b_a, wx_bd, b_x, lam)


def _hgrn2_kernel(u_ref, wq_ref, wf_ref, wv_ref, wg_ref, lbp_ref, gn_ref, o_ref, st_ref, p_ref):
    @pl.when(pl.program_id(2) == 0)
    def _():
        st_ref[...] = jnp.zeros_like(st_ref)

    group_rows = CHUNK * HG_GROUP
    col = lambda j: slice(j * HG_DK, (j + 1) * HG_DK)
    w_halves = (jnp.concatenate([wq_ref[...], wf_ref[...]], axis=1),
                jnp.concatenate([wv_ref[...], wg_ref[...]], axis=1))

    def project(g, slot, half):
        rows = pl.ds(pl.multiple_of(g * group_rows, group_rows), group_rows)
        cols = slice(half * 2 * HG_DK, (half + 1) * 2 * HG_DK)
        p_ref[slot, :, cols] = jnp.dot(u_ref[rows, :], w_halves[half], preferred_element_type=F32)

    lbp = lbp_ref[...]
    e = jnp.exp(lbp - jnp.max(lbp, axis=0, keepdims=True))
    p = e / jnp.sum(e, axis=0, keepdims=True)
    lb = (p[0:1] + p[1:2]) - p[0:1]
    gn = gn_ref[...]
    ngroup = u_ref.shape[0] // group_rows
    srow = lax.broadcasted_iota(jnp.int32, (CHUNK, CHUNK), 0)
    scol = lax.broadcasted_iota(jnp.int32, (CHUNK, CHUNK), 1)
    levels = []
    h = CHUNK // 2
    while h >= 1:
        levels.append((h, (srow & ~(2 * h - 1)) == (scol & ~(2 * h - 1))))
        h //= 2
    sub_row = lax.broadcasted_iota(jnp.int32, (SUBLANES, 1), 0)
    chunk_row = lax.broadcasted_iota(jnp.int32, (CHUNK, 1), 0)
    tril3 = jnp.where(srow >= scol, 1.0, 0.0).astype(BF16)
    tril3 = jnp.concatenate([tril3, tril3, tril3], axis=1)

    def cumsum_rows(x):
        hi = x.astype(BF16)
        r1 = x - hi.astype(F32)
        mid = r1.astype(BF16)
        lo = (r1 - mid.astype(F32)).astype(BF16)
        return jnp.dot(tril3, jnp.concatenate([hi, mid, lo], axis=0), preferred_element_type=F32)

    def block_rows(x, size, r):
        out = []
        for lo in range(0, CHUNK, SUBLANES):
            acc = jnp.broadcast_to(x[lo + r:lo + r + 1], (SUBLANES, LANES))
            for off in range(size, SUBLANES, size):
                acc = jnp.where(sub_row >= off,
                                jnp.broadcast_to(x[lo + off + r:lo + off + r + 1], (SUBLANES, LANES)), acc)
            out.append(acc)
        return jnp.concatenate(out, axis=0)

    def level_factors(q, k, f, g2, h):
        if h < SUBLANES:
            upper = (chunk_row & (2 * h - 1)) >= h
            if h == 1:
                return jnp.where(upper, q * f, 0.0), jnp.where(upper, 0.0, k)
            d = g2 - block_rows(g2, 2 * h, h - 1)
            e = jnp.exp2(jnp.where(upper, d, -d))
            return jnp.where(upper, q * e, 0.0), jnp.where(upper, 0.0, k * e)
        zero = jnp.zeros((h, LANES), F32)
        qt, kt = [], []
        for lo in range(0, CHUNK, 2 * h):
            mid = lo + h
            rho = g2[mid - 1:mid, :]
            kt += [k[lo:mid] * jnp.exp2(rho - g2[lo:mid]), zero]
            qt += [zero, q[mid:mid + h] * jnp.exp2(g2[mid:mid + h] - rho)]
        return jnp.concatenate(qt, axis=0), jnp.concatenate(kt, axis=0)

    def group(i, carry):
        idx = range(HG_GROUP)
        slot = i % 2
        nxt = jnp.minimum(i + 1, ngroup - 1)
        local = [pl.ds(j * CHUNK, CHUNK) for j in idx]
        rows = [pl.ds(pl.multiple_of((i * HG_GROUP + j) * CHUNK, CHUNK), CHUNK) for j in idx]
        q = [_silu(p_ref[slot, r, col(0)]) for r in local]
        sig = [1.0 / (1.0 + jnp.exp(-p_ref[slot, r, col(1)])) for r in local]
        v = [p_ref[slot, r, col(2)].astype(BF16) for r in local]
        gate = [p_ref[slot, r, col(3)] for r in local]
        k = [(1.0 - lb) * (1.0 - x) for x in sig]
        f = [lb + (1.0 - lb) * x for x in sig]
        g2 = [cumsum_rows(jnp.log2(x)) for x in f]
        project(nxt, 1 - slot, 0)
        gend = [x[CHUNK - 1:CHUNK, :] for x in g2]
        qg = [(a * jnp.exp2(b)).astype(BF16) for a, b in zip(q, g2)]
        vtk = [_dot_tn(a, b * jnp.exp2(e - c)) for a, b, c, e in zip(v, k, g2, gend)]

        scores = [jnp.where(srow == scol, jnp.sum(a * b, axis=-1, keepdims=True), 0.0)
                  for a, b in zip(q, k)]
        for h, same_block in levels:
            for j in idx:
                qt, kt = level_factors(q[j], k[j], f[j], g2[j], h)
                scores[j] = scores[j] + jnp.where(same_block, _dot_nt(qt, kt), 0.0)
        out = [_dot(a, b) for a, b in zip(scores, v)]

        st = st_ref[...]
        for j in idx:
            out[j] = out[j] + _dot_nt(qg[j], st)
            st = st * jnp.exp2(gend[j]) + vtk[j]
        st_ref[...] = st
        project(nxt, 1 - slot, 1)
        for j in idx:
            o_ref[rows[j], :] = (_rms(out[j], gn) * _silu(gate[j])).astype(o_ref.dtype)
        return carry

    project(0, 0, 0)
    project(0, 0, 1)
    lax.fori_loop(0, ngroup, group, 0)


def _hgrn2(u3, w_bf16, first_split, lb_params, g_norm, ts=8192):
    bsz, seq, d = u3.shape

    def w_spec(split):
        base = split * (d // HG_DK)
        return pl.BlockSpec((d, HG_DK), lambda b, h, s: (0, base + h))

    return pl.pallas_call(
        _hgrn2_kernel,
        grid=(bsz, HG_HEADS, seq // ts),
        in_specs=[pl.BlockSpec((None, ts, d), lambda b, h, s: (b, s, 0))]
        + [w_spec(first_split + j) for j in range(4)]
        + [pl.BlockSpec((lb_params.shape[0], HG_DK), lambda b, h, s: (0, h)),
           pl.BlockSpec((1, HG_DK), lambda b, h, s: (0, 0))],
        out_specs=pl.BlockSpec((None, ts, HG_DK), lambda b, h, s: (b, s, h)),
        out_shape=jax.ShapeDtypeStruct((bsz, seq, d), BF16),
        scratch_shapes=[pltpu.VMEM((HG_DK, HG_DK), F32),
                        pltpu.VMEM((2, CHUNK * HG_GROUP, 4 * HG_DK), F32)],
        compiler_params=_params("parallel", "parallel", "arbitrary"),
        name="l0_hgrn2",
    )(u3, w_bf16, w_bf16, w_bf16, w_bf16, lb_params, g_norm)


def _out_proj0_kernel(hg_ref, og_ref, w_ref, pn_ref, x_ref, o_ref):
    half = hg_ref.shape[1]
    y = (jnp.dot(hg_ref[...], w_ref[:half, :], preferred_element_type=F32)
         + jnp.dot(og_ref[...], w_ref[half:, :], preferred_element_type=F32))
    o_ref[...] = x_ref[...] + _rms(y, pn_ref[...])


def _out_proj0(hg, og, w_bf16, post_gain, x2d, tm=1024):
    t, d = x2d.shape
    tok = lambda: pl.BlockSpec((tm, d), lambda i: (i, 0))
    return pl.pallas_call(
        _out_proj0_kernel,
        grid=(t // tm,),
        in_specs=[tok(), tok(),
                  pl.BlockSpec(w_bf16.shape, lambda i: (0, 0)),
                  pl.BlockSpec((1, d), lambda i: (0, 0)),
                  tok()],
        out_specs=tok(),
        out_shape=jax.ShapeDtypeStruct((t, d), F32),
        compiler_params=_params("parallel"),
        name="l0_out_proj",
    )(hg, og, w_bf16, post_gain, x2d)


def _rwkv_front_kernel(x_ref, pn_ref, mu_ref, win_ref, w0_ref, w1_ref, w2_ref, a0_ref, a1_ref,
                       a2_ref, kk_ref, ka_ref, ones_ref,
                       r_out, lw_out, k_out, v_out, kk_out, b_out, g_out, prev_ref):
    @pl.when(pl.program_id(1) == 0)
    def _():
        prev_ref[...] = jnp.zeros_like(prev_ref)

    u = _rms(x_ref[...], pn_ref[...])
    tm = u.shape[0]
    row = lax.broadcasted_iota(jnp.int32, (tm, 1), 0)
    delta = jnp.where(row == 0, prev_ref[...], pltpu.roll(u, 1, 0)) - u
    prev_ref[...] = u[tm - 1:tm, :]
    mix = lambda p: u + delta * mu_ref[p:p + 1, :]

    r_out[...] = _dot(mix(0), win_ref[0]).astype(r_out.dtype)
    k = _dot(mix(1), win_ref[1])
    v_out[...] = _dot(mix(2), win_ref[2]).astype(v_out.dtype)
    g_out[...] = _dot(mix(3), win_ref[3]).astype(g_out.dtype)
    lora_w = _dot(jnp.tanh(_dot(mix(4), w1_ref[...])), w2_ref[...])
    lora_a = _dot(_dot(mix(5), a1_ref[...]), a2_ref[...])
    lw_out[...] = (-EXP_NEG_HALF) * _sigmoid(w0_ref[...] + lora_w)
    iclr = _sigmoid(a0_ref[...] + lora_a)
    k_out[...] = (k * (1.0 + (iclr - 1.0) * ka_ref[...])).astype(k_out.dtype)
    kk = k * kk_ref[...]
    kk = kk * lax.rsqrt(jnp.maximum(_seg_sum(kk * kk, ones_ref[...]), 1e-24))
    kk_out[...] = kk.astype(kk_out.dtype)
    b_out[...] = (kk * iclr).astype(b_out.dtype)


def _rwkv_front(x1, pre_gain, mu, win, w0, w1, w2, a0, a1, a2, k_k, k_a, ones_bd, tm=512):
    bsz, seq, d = x1.shape
    tok = lambda: pl.BlockSpec((None, tm, d), lambda b, s: (b, s, 0))
    full = lambda a: pl.BlockSpec(a.shape, lambda b, s: (0,) * a.ndim)
    args = (pre_gain, mu, win, w0, w1, w2, a0, a1, a2, k_k, k_a, ones_bd)
    return pl.pallas_call(
        _rwkv_front_kernel,
        grid=(bsz, seq // tm),
        in_specs=[tok()] + [full(a) for a in args],
        out_specs=[tok() for _ in range(7)],
        out_shape=[jax.ShapeDtypeStruct((bsz, seq, d), F32 if i == 1 else BF16) for i in range(7)],
        scratch_shapes=[pltpu.VMEM((1, d), F32)],
        compiler_params=_params("parallel", "arbitrary"),
        name="l1_front",
    )(x1, *args)


def _rwkv_scan_kernel(r_ref, lw_ref, k_ref, v_ref, kk_ref, b_ref, y_ref,
                      ht_ref, rt_s, y0_s, ml_s, ha_s, gm_s):
    @pl.when(pl.program_id(2) == 0)
    def _():
        ht_ref[...] = jnp.zeros_like(ht_ref)

    ts, width = r_ref.shape
    nchunk = ts // CHUNK
    npair = width // LANES
    lanes_of = [slice(p * LANES, (p + 1) * LANES) for p in range(npair)]
    head0 = lax.broadcasted_iota(jnp.int32, (CHUNK, LANES), 1) < RW_HEAD
    row = lax.broadcasted_iota(jnp.int32, (CHUNK, LANES), 0)
    tok = lax.broadcasted_iota(jnp.int32, (CHUNK, LANES), 1) & (CHUNK - 1)
    strict = tok < row
    incl = tok <= row
    eye = jnp.where(tok == row, 1.0, 0.0)
    same_head = ((lax.broadcasted_iota(jnp.int32, (LANES, LANES), 0) < RW_HEAD)
                 == (lax.broadcasted_iota(jnp.int32, (LANES, LANES), 1) < RW_HEAD))

    def stack(x):
        return jnp.concatenate([jnp.where(head0, x, 0.0), jnp.where(head0, 0.0, x)], axis=0)

    def prep(c):
        rows = pl.ds(pl.multiple_of(c * CHUNK, CHUNK), CHUNK)
        r, lw, k, v, kk, b = (ref[rows, :].astype(F32)
                              for ref in (r_ref, lw_ref, k_ref, v_ref, kk_ref, b_ref))
        lcum = _cumsum_rows(lw)
        lend = lcum[CHUNK - 1:CHUNK, :]
        e_neg = jnp.exp(-lcum)
        e_end = jnp.exp(lend - lcum)
        gm_s[c] = jnp.exp(lend)
        at, rt = -kk * jnp.exp(lcum - lw), r * jnp.exp(lcum)
        bt, kt, bh, kh = b * e_neg, k * e_neg, b * e_end, k * e_end
        return [dict(rows=rows, c=c, p=p, sl=sl,
                     art=jnp.concatenate([at[:, sl], rt[:, sl]], axis=0).astype(BF16),
                     at2=stack(at[:, sl]).astype(BF16),
                     rt=rt[:, sl],
                     bk2=jnp.concatenate([stack(bt[:, sl]), stack(kt[:, sl])], axis=0).astype(BF16),
                     v=v[:, sl].astype(BF16),
                     v2=stack(v[:, sl]).astype(BF16),
                     bh=bh[:, sl].astype(BF16),
                     bkh=jnp.concatenate([bh[:, sl], kh[:, sl]], axis=0).astype(BF16))
                for p, sl in enumerate(lanes_of)]

    def local_group(i, between=()):
        pending = list(between)
        passed = [0]

        def boundary():
            fired = len(between) - len(pending)
            if pending and passed[0] * len(between) >= fired * SCAN_STAGES:
                pending.pop(0)()
            passed[0] += 1

        g = [q for j in range(SCAN_GROUP) for q in prep(i * SCAN_GROUP + j)]
        s = [_dot_nt(q["art"], q["bk2"]) for q in g]
        boundary()
        n = [jnp.where(strict, x[:CHUNK, :LANES], 0.0) for x in s]
        a_ak = [jnp.where(strict, x[:CHUNK, LANES:], 0.0).astype(BF16) for x in s]
        a_rbk = [jnp.concatenate([jnp.where(incl, x[CHUNK:, :LANES], 0.0),
                                  jnp.where(incl, x[CHUNK:, LANES:], 0.0)], axis=1).astype(BF16) for x in s]
        t = [eye + x for x in n]
        pw = [_dot(x, stack(x)) for x in n]
        boundary()
        d = 2
        while d < CHUNK // 2:
            m = [_dot(jnp.concatenate([a, b], axis=0), stack(b)) for a, b in zip(t, pw)]
            t = [a + x[:CHUNK] for a, x in zip(t, m)]
            pw = [x[CHUNK:] for x in m]
            boundary()
            d *= 2
        t = [a + _dot(a, stack(b)) for a, b in zip(t, pw)]
        boundary()
        akv = [_dot(x, q["v2"]) for x, q in zip(a_ak, g)]
        tw = [_dot(x, jnp.concatenate([q["at2"], stack(y).astype(BF16)], axis=1))
              for x, y, q in zip(t, akv, g)]
        boundary()
        wt = [x[:, :LANES] for x in tw]
        ut = [x[:, LANES:] for x in tw]
        rw = [_dot(x[:, :LANES], stack(y)) for x, y in zip(a_rbk, wt)]
        y0 = [_dot(x, jnp.concatenate([stack(y).astype(BF16), q["v2"]], axis=0))
              for x, y, q in zip(a_rbk, ut, g)]
        ml = [_dot_tn(q["bh"], y) for q, y in zip(g, wt)]
        ha = [_dot_tn(jnp.concatenate([y.astype(BF16), q["v"]], axis=0), q["bkh"]) for q, y in zip(g, ut)]
        while pending:
            pending.pop(0)()
        for j, q in enumerate(g):
            rt_s[q["rows"], q["sl"]] = (q["rt"] + rw[j]).astype(BF16)
            y0_s[q["rows"], q["sl"]] = y0[j]
            ml_s[q["c"], q["p"]] = jnp.where(same_head, ml[j], 0.0).astype(BF16)
            ha_s[q["c"], q["p"]] = jnp.where(same_head, ha[j], 0.0)

    def state_step(c, hts):
        rows = pl.ds(pl.multiple_of(c * CHUNK, CHUNK), CHUNK)
        rt, gm = rt_s[rows, :], gm_s[c]
        htb = [h.astype(BF16) for h in hts]
        ys = [_dot_nt(rt[:, sl], hb) for sl, hb in zip(lanes_of, htb)]
        new = [h * gm[:, sl] + _dot_nt(hb, ml_s[c, p]) + ha_s[c, p]
               for p, (sl, h, hb) in enumerate(zip(lanes_of, hts, htb))]
        y_ref[rows, :] = (jnp.concatenate(ys, axis=1) + y0_s[rows, :]).astype(y_ref.dtype)
        return tuple(new)

    def pipelined(i, hts):
        box = [hts]

        def advance(j):
            def run():
                box[0] = state_step((i - 1) * SCAN_GROUP + j, box[0])
            return run

        local_group(i, between=[advance(j) for j in range(SCAN_GROUP)])
        return box[0]

    ngroup = nchunk // SCAN_GROUP
    local_group(0)
    hts = lax.fori_loop(1, ngroup, pipelined, tuple(ht_ref[p] for p in range(npair)))
    for j in range(SCAN_GROUP):
        hts = state_step((ngroup - 1) * SCAN_GROUP + j, hts)
    for p in range(npair):
        ht_ref[p] = hts[p]


def _rwkv_scan(r, lw, k, v, kk, b, ts=1024, width=512):
    bsz, seq, d = r.shape
    nchunk = ts // CHUNK
    npair = width // LANES
    spec = lambda: pl.BlockSpec((None, ts, width), lambda bb, h, s: (bb, s, h))
    return pl.pallas_call(
        _rwkv_scan_kernel,
        grid=(bsz, d // width, seq // ts),
        in_specs=[spec() for _ in range(6)],
        out_specs=spec(),
        out_shape=jax.ShapeDtypeStruct((bsz, seq, d), BF16),
        scratch_shapes=[pltpu.VMEM((npair, LANES, LANES), F32),
                        pltpu.VMEM((ts, width), BF16),
                        pltpu.VMEM((ts, width), F32),
                        pltpu.VMEM((nchunk, npair, LANES, LANES), BF16),
                        pltpu.VMEM((nchunk, npair, LANES, LANES), F32),
                        pltpu.VMEM((nchunk, 1, width), F32)],
        compiler_params=_params("parallel", "parallel", "arbitrary"),
        name="l1_scan",
    )(r, lw, k, v, kk, b)


def _rwkv_back_kernel(y_ref, r_ref, k_ref, v_ref, g_ref, x_ref, rk_ref, lnw_ref, lnb_ref,
                      w_ref, pn_ref, ones_ref, o_ref):
    ones = ones_ref[...]
    inv_n = 1.0 / RW_HEAD
    y = y_ref[...].astype(F32)
    yc = y - _seg_sum(y, ones) * inv_n
    var = _seg_sum(yc * yc, ones) * inv_n
    yn = yc * lax.rsqrt(var + RW_GN_EPS) * lnw_ref[...] + lnb_ref[...]
    rk = r_ref[...].astype(F32) * k_ref[...].astype(F32) * rk_ref[...]
    yn = yn + _seg_sum(rk, ones) * v_ref[...].astype(F32)
    out = _dot(yn * _silu(g_ref[...].astype(F32)), w_ref[...])
    o_ref[...] = x_ref[...] + _rms(out, pn_ref[...])


def _rwkv_back(y, r, k, v, g, x1, r_k, ln_w, ln_b, w_bf16, post_gain, ones_bd, tm=512):
    t, d = y.shape
    tok = lambda: pl.BlockSpec((tm, d), lambda i: (i, 0))
    full = lambda a: pl.BlockSpec(a.shape, lambda i: (0,) * a.ndim)
    params = (r_k, ln_w, ln_b, w_bf16, post_gain, ones_bd)
    return pl.pallas_call(
        _rwkv_back_kernel,
        grid=(t // tm,),
        in_specs=[tok() for _ in range(6)] + [full(a) for a in params],
        out_specs=tok(),
        out_shape=jax.ShapeDtypeStruct((t, d), F32),
        compiler_params=_params("parallel"),
        name="l1_back",
    )(y, r, k, v, g, x1, *params)


def _block_diag(w, group):
    n, bi, bj = w.shape
    w = w.reshape(n // group, group, bi, bj)
    eye = jnp.eye(group, dtype=w.dtype)
    return jnp.einsum("gaij,ab->gaibj", w, eye).reshape(n // group, group * bi, group * bj)


def kernel(x, pre_norm, post_norm, ab_w_in, ab_w_out, rg_conv_w, rg_conv_b, rg_w_a, rg_b_a,
           rg_w_x, rg_b_x, rg_lambda, hg_lower_bounds, hg_out_norm, rw_mu, rw_w_in, rw_w_out,
           rw_w0, rw_w1, rw_w2, rw_a0, rw_a1, rw_a2, rw_k_k, rw_k_a, rw_r_k, rw_ln_w, rw_ln_b):
    bsz, seq, d = x.shape
    t = bsz * seq
    row = lambda a: a.reshape(1, -1).astype(F32)
    x2d = x.reshape(t, d)
    heads_per_tile = MXU_DIM // RW_HEAD
    ones_bd = _block_diag(jnp.ones((heads_per_tile, RW_HEAD, RW_HEAD), F32), heads_per_tile)[0].astype(BF16)

    w_in = ab_w_in[0].astype(BF16)
    rg_splits = 2
    proj, u0 = _norm_proj(x2d, row(pre_norm[0]), w_in, rg_splits * d)
    proj4 = proj.reshape(proj.shape[0], bsz, seq, proj.shape[-1])
    rg_group = MXU_DIM // rg_w_a.shape[-1]
    hg = _rglru(proj4, rg_conv_w[0].astype(F32), row(rg_conv_b[0]),
                _block_diag(rg_w_a[0], rg_group).astype(BF16), row(rg_b_a[0]),
                _block_diag(rg_w_x[0], rg_group).astype(BF16), row(rg_b_x[0]), row(rg_lambda[0]))
    og = _hgrn2(u0.reshape(bsz, seq, d), w_in, rg_splits, hg_lower_bounds.astype(F32),
                row(hg_out_norm[0]))
    x1 = _out_proj0(hg.reshape(t, d), og.reshape(t, d), ab_w_out[0].astype(BF16),
                    row(post_norm[0]), x2d)

    r, lw, k, v, kk, b, g = _rwkv_front(
        x1.reshape(bsz, seq, d), row(pre_norm[1]), rw_mu[0].astype(F32), rw_w_in[0].astype(BF16),
        row(rw_w0[0]), rw_w1[0].astype(BF16), rw_w2[0].astype(BF16), row(rw_a0[0]),
        rw_a1[0].astype(BF16), rw_a2[0].astype(BF16), row(rw_k_k[0]), row(rw_k_a[0]), ones_bd)
    y = _rwkv_scan(r, lw, k, v, kk, b)
    flat = lambda a: a.reshape(t, d)
    x2 = _rwkv_back(flat(y), flat(r), flat(k), flat(v), flat(g), x1, row(rw_r_k[0]),
                    row(rw_ln_w[0]), row(rw_ln_b[0]), rw_w_out[0].astype(BF16),
                    row(post_norm[1]), ones_bd)
    return x2.reshape(bsz, seq, d)
```

```python
import jax
import jax.numpy as jnp
from jax import lax
from jax.experimental import pallas as pl
from jax.experimental.pallas import tpu as pltpu

F32 = jnp.float32
BF16 = jnp.bfloat16

D_MODEL = 1024
RMS_EPS = 1e-6
RG_C = 8.0
RG_CONV = 4
HG_HEADS = 8
HG_DK = 128
RW_HEAD = 64
RW_GN_EPS = 64e-5

LANES = 128
SUBLANES = 8
MXU_DIM = 256
VMEM_LIMIT = 56 * 1024 * 1024

CHUNK = 64
SCAN_GROUP = 4
HG_GROUP = 16
SCAN_STAGES = 8
EXP_NEG_HALF = 0.6065306597126334


def _params(*sem):
    return pltpu.CompilerParams(dimension_semantics=sem, vmem_limit_bytes=VMEM_LIMIT)


def _dot(a, b):
    return jnp.dot(a.astype(BF16), b.astype(BF16), preferred_element_type=F32)


def _dot_nt(a, b):
    return lax.dot_general(a.astype(BF16), b.astype(BF16), (((1,), (1,)), ((), ())),
                           preferred_element_type=F32)


def _dot_tn(a, b):
    return lax.dot_general(a.astype(BF16), b.astype(BF16), (((0,), (0,)), ((), ())),
                           preferred_element_type=F32)


def _sigmoid(x):
    return 0.5 * jnp.tanh(0.5 * x) + 0.5


def _silu(x):
    return x * _sigmoid(x)


def _softplus(z):
    return jnp.maximum(z, 0.0) + jnp.log(1.0 + jnp.exp(-jnp.abs(z)))


def _rms(x, gain):
    return x * lax.rsqrt(jnp.mean(x * x, axis=-1, keepdims=True) + RMS_EPS) * gain


def _cumsum_rows(x):
    n = x.shape[0]
    row = lax.broadcasted_iota(jnp.int32, x.shape, 0)
    d = 1
    while d < n:
        x = x + jnp.where(row >= d, pltpu.roll(x, d, 0), 0.0)
        d *= 2
    return x


def _seg_sum(x, ones_bd):
    xb = x.astype(BF16)
    return jnp.concatenate(
        [jnp.dot(xb[:, g * MXU_DIM:(g + 1) * MXU_DIM], ones_bd, preferred_element_type=F32)
         for g in range(x.shape[1] // MXU_DIM)], axis=-1)


def _norm_proj_kernel(x_ref, g_ref, w_ref, o_ref, u_ref):
    @pl.when(pl.program_id(1) == 0)
    def _():
        u_ref[...] = _rms(x_ref[...], g_ref[...]).astype(BF16)

    o_ref[...] = jnp.dot(u_ref[...], w_ref[...], preferred_element_type=F32).astype(o_ref.dtype)


def _norm_proj(x2d, gain, w_bf16, n, tm=1024, tn=2048):
    t, d = x2d.shape
    return pl.pallas_call(
        _norm_proj_kernel,
        grid=(t // tm, n // tn),
        in_specs=[pl.BlockSpec((tm, d), lambda i, j: (i, 0)),
                  pl.BlockSpec((1, d), lambda i, j: (0, 0)),
                  pl.BlockSpec((d, tn), lambda i, j: (0, j))],
        out_specs=[pl.BlockSpec((None, tm, tn), lambda i, j: (j, i, 0)),
                   pl.BlockSpec((tm, d), lambda i, j: (i, 0))],
        out_shape=[jax.ShapeDtypeStruct((n // tn, t, tn), BF16),
                   jax.ShapeDtypeStruct((t, d), BF16)],
        compiler_params=_params("parallel", "arbitrary"),
        name="l0_norm_proj",
    )(x2d, gain, w_bf16)


def _rglru_kernel(x_ref, g_ref, cw_ref, cb_ref, wa_ref, ba_ref, wx_ref, bx_ref, lam_ref,
                  o_ref, h_ref, ext_ref, a_ref, b_ref):
    @pl.when(pl.program_id(1) == 0)
    def _():
        h_ref[...] = jnp.zeros_like(h_ref)
        ext_ref[:SUBLANES, :] = jnp.zeros((SUBLANES, ext_ref.shape[1]), F32)

    x = x_ref[...].astype(F32)
    ts, width = x.shape
    ext_ref[SUBLANES:, :] = x
    xc = cb_ref[...] + cw_ref[RG_CONV - 1:RG_CONV, :] * x
    for k in range(1, RG_CONV):
        xc = xc + cw_ref[RG_CONV - 1 - k:RG_CONV - k, :] * ext_ref[SUBLANES - k:SUBLANES - k + ts, :]
    ext_ref[:SUBLANES, :] = x[ts - SUBLANES:, :]

    xcb = xc.astype(BF16)
    pre_r, pre_i = [], []
    for g in range(width // MXU_DIM):
        sl = slice(g * MXU_DIM, (g + 1) * MXU_DIM)
        pre_r.append(jnp.dot(xcb[:, sl], wa_ref[g], preferred_element_type=F32))
        pre_i.append(jnp.dot(xcb[:, sl], wx_ref[g], preferred_element_type=F32))
    gate_r = _sigmoid(jnp.concatenate(pre_r, axis=-1) + ba_ref[...])
    gate_i = _sigmoid(jnp.concatenate(pre_i, axis=-1) + bx_ref[...])

    log_a = (-RG_C) * gate_r * _softplus(-lam_ref[...])
    a = jnp.exp(log_a)
    b = jnp.sqrt(1.0 - a * a) * (gate_i * xc)

    a = a.reshape(ts // SUBLANES, SUBLANES, width)
    b = b.reshape(ts // SUBLANES, SUBLANES, width)
    in_group = lax.broadcasted_iota(jnp.int32, (1, SUBLANES, 1), 1)
    d = 1
    while d < SUBLANES:
        m = in_group >= d
        b = jnp.where(m, a, 0.0) * pltpu.roll(b, d, 1) + b
        a = a * jnp.where(m, pltpu.roll(a, d, 1), 1.0)
        d *= 2
    a_ref[...] = a.reshape(ts, width)
    b_ref[...] = b.reshape(ts, width)

    def carry_group(i, h):
        rows = pl.ds(pl.multiple_of(i * SUBLANES, SUBLANES), SUBLANES)
        hg = a_ref[rows, :] * h + b_ref[rows, :]
        b_ref[rows, :] = hg
        return hg[SUBLANES - 1:, :]

    h_ref[...] = lax.fori_loop(0, ts // SUBLANES, carry_group, h_ref[...], unroll=8)
    o_ref[...] = (b_ref[...] * _silu(g_ref[...].astype(F32))).astype(o_ref.dtype)


def _proj_block(proj4, split, block, offset=0):
    per_slab = proj4.shape[-1] // block
    col = split * (D_MODEL // block) + offset
    return col // per_slab, col % per_slab


def _rglru(proj4, conv_w, conv_b, wa_bd, b_a, wx_bd, b_x, lam, ts=512):
    _, bsz, seq, _ = proj4.shape
    width = D_MODEL
    row = lambda: pl.BlockSpec((1, width), lambda b, s: (0, 0))

    def split_spec(split):
        slab, blk = _proj_block(proj4, split, width)
        return pl.BlockSpec((None, None, ts, width), lambda b, s: (slab, b, s, blk))

    return pl.pallas_call(
        _rglru_kernel,
        grid=(bsz, seq // ts),
        in_specs=[split_spec(0), split_spec(1),
                  pl.BlockSpec((RG_CONV, width), lambda b, s: (0, 0)),
                  row(),
                  pl.BlockSpec(wa_bd.shape, lambda b, s: (0, 0, 0)),
                  row(),
                  pl.BlockSpec(wx_bd.shape, lambda b, s: (0, 0, 0)),
                  row(), row()],
        out_specs=pl.BlockSpec((None, ts, width), lambda b, s: (b, s, 0)),
        out_shape=jax.ShapeDtypeStruct((bsz, seq, width), BF16),
        scratch_shapes=[pltpu.VMEM((1, width), F32), pltpu.VMEM((SUBLANES + ts, width), F32),
                        pltpu.VMEM((ts, width), F32), pltpu.VMEM((ts, width), F32)],
        compiler_params=_params("parallel", "arbitrary"),
        name="l0_rglru",
    )(proj4, proj4, conv_w, conv_b, wa_bd, b_a, wx_bd, b_x, lam)


def _hgrn2_kernel(u_ref, wq_ref, wf_ref, wv_ref, wg_ref, lbp_ref, gn_ref, o_ref, st_ref, p_ref):
    @pl.when(pl.program_id(2) == 0)
    def _():
        st_ref[...] = jnp.zeros_like(st_ref)

    group_rows = CHUNK * HG_GROUP
    col = lambda j: slice(j * HG_DK, (j + 1) * HG_DK)
    w_halves = (jnp.concatenate([wq_ref[...].astype(BF16), wf_ref[...].astype(BF16)], axis=1),
                jnp.concatenate([wv_ref[...].astype(BF16), wg_ref[...].astype(BF16)], axis=1))

    def project(g, slot, half):
        rows = pl.ds(pl.multiple_of(g * group_rows, group_rows), group_rows)
        cols = slice(half * 2 * HG_DK, (half + 1) * 2 * HG_DK)
        p_ref[slot, :, cols] = jnp.dot(u_ref[rows, :], w_halves[half], preferred_element_type=F32)

    lbp = lbp_ref[...]
    e = jnp.exp(lbp - jnp.max(lbp, axis=0, keepdims=True))
    p = e / jnp.sum(e, axis=0, keepdims=True)
    lb = (p[0:1] + p[1:2]) - p[0:1]
    gn = gn_ref[...]
    ngroup = u_ref.shape[0] // group_rows
    srow = lax.broadcasted_iota(jnp.int32, (CHUNK, CHUNK), 0)
    scol = lax.broadcasted_iota(jnp.int32, (CHUNK, CHUNK), 1)
    levels = []
    h = CHUNK // 2
    while h >= 1:
        levels.append((h, (srow & ~(2 * h - 1)) == (scol & ~(2 * h - 1))))
        h //= 2
    sub_row = lax.broadcasted_iota(jnp.int32, (SUBLANES, 1), 0)
    chunk_row = lax.broadcasted_iota(jnp.int32, (CHUNK, 1), 0)
    tril3 = jnp.where(srow >= scol, 1.0, 0.0).astype(BF16)
    tril3 = jnp.concatenate([tril3, tril3, tril3], axis=1)

    def cumsum_rows(x):
        hi = x.astype(BF16)
        r1 = x - hi.astype(F32)
        mid = r1.astype(BF16)
        lo = (r1 - mid.astype(F32)).astype(BF16)
        return jnp.dot(tril3, jnp.concatenate([hi, mid, lo], axis=0), preferred_element_type=F32)

    def block_rows(x, size, r):
        out = []
        for lo in range(0, CHUNK, SUBLANES):
            acc = jnp.broadcast_to(x[lo + r:lo + r + 1], (SUBLANES, LANES))
            for off in range(size, SUBLANES, size):
                acc = jnp.where(sub_row >= off,
                                jnp.broadcast_to(x[lo + off + r:lo + off + r + 1], (SUBLANES, LANES)), acc)
            out.append(acc)
        return jnp.concatenate(out, axis=0)

    def level_factors(q, k, f, g2, h):
        if h < SUBLANES:
            upper = (chunk_row & (2 * h - 1)) >= h
            if h == 1:
                return jnp.where(upper, q * f, 0.0), jnp.where(upper, 0.0, k)
            d = g2 - block_rows(g2, 2 * h, h - 1)
            e = jnp.exp2(jnp.where(upper, d, -d))
            return jnp.where(upper, q * e, 0.0), jnp.where(upper, 0.0, k * e)
        zero = jnp.zeros((h, LANES), F32)
        qt, kt = [], []
        for lo in range(0, CHUNK, 2 * h):
            mid = lo + h
            rho = g2[mid - 1:mid, :]
            kt += [k[lo:mid] * jnp.exp2(rho - g2[lo:mid]), zero]
            qt += [zero, q[mid:mid + h] * jnp.exp2(g2[mid:mid + h] - rho)]
        return jnp.concatenate(qt, axis=0), jnp.concatenate(kt, axis=0)

    def group(i, carry):
        idx = range(HG_GROUP)
        slot = i % 2
        nxt = jnp.minimum(i + 1, ngroup - 1)
        local = [pl.ds(j * CHUNK, CHUNK) for j in idx]
        rows = [pl.ds(pl.multiple_of((i * HG_GROUP + j) * CHUNK, CHUNK), CHUNK) for j in idx]
        q = [_silu(p_ref[slot, r, col(0)]) for r in local]
        sig = [1.0 / (1.0 + jnp.exp(-p_ref[slot, r, col(1)])) for r in local]
        v = [p_ref[slot, r, col(2)].astype(BF16) for r in local]
        gate = [p_ref[slot, r, col(3)] for r in local]
        k = [(1.0 - lb) * (1.0 - x) for x in sig]
        f = [lb + (1.0 - lb) * x for x in sig]
        g2 = [cumsum_rows(jnp.log2(x)) for x in f]
        project(nxt, 1 - slot, 0)
        gend = [x[CHUNK - 1:CHUNK, :] for x in g2]
        qg = [(a * jnp.exp2(b)).astype(BF16) for a, b in zip(q, g2)]
        vtk = [_dot_tn(a, b * jnp.exp2(e - c)) for a, b, c, e in zip(v, k, g2, gend)]

        scores = [jnp.where(srow == scol, jnp.sum(a * b, axis=-1, keepdims=True), 0.0)
                  for a, b in zip(q, k)]
        for h, same_block in levels:
            for j in idx:
                qt, kt = level_factors(q[j], k[j], f[j], g2[j], h)
                scores[j] = scores[j] + jnp.where(same_block, _dot_nt(qt, kt), 0.0)
        out = [_dot(a, b) for a, b in zip(scores, v)]

        st = st_ref[...]
        for j in idx:
            out[j] = out[j] + _dot_nt(qg[j], st)
            st = st * jnp.exp2(gend[j]) + vtk[j]
        st_ref[...] = st
        project(nxt, 1 - slot, 1)
        for j in idx:
            o_ref[rows[j], :] = (_rms(out[j], gn) * _silu(gate[j])).astype(o_ref.dtype)
        return carry

    project(0, 0, 0)
    project(0, 0, 1)
    lax.fori_loop(0, ngroup, group, 0)


def _hgrn2(u3, w_in, first_split, lb_params, g_norm, ts=8192):
    bsz, seq, d = u3.shape

    def w_spec(split):
        base = split * (d // HG_DK)
        return pl.BlockSpec((d, HG_DK), lambda b, h, s: (0, base + h))

    return pl.pallas_call(
        _hgrn2_kernel,
        grid=(bsz, HG_HEADS, seq // ts),
        in_specs=[pl.BlockSpec((None, ts, d), lambda b, h, s: (b, s, 0))]
        + [w_spec(first_split + j) for j in range(4)]
        + [pl.BlockSpec((lb_params.shape[0], HG_DK), lambda b, h, s: (0, h)),
           pl.BlockSpec((1, HG_DK), lambda b, h, s: (0, 0))],
        out_specs=pl.BlockSpec((None, ts, HG_DK), lambda b, h, s: (b, s, h)),
        out_shape=jax.ShapeDtypeStruct((bsz, seq, d), BF16),
        scratch_shapes=[pltpu.VMEM((HG_DK, HG_DK), F32),
                        pltpu.VMEM((2, CHUNK * HG_GROUP, 4 * HG_DK), F32)],
        compiler_params=_params("parallel", "parallel", "arbitrary"),
        name="l0_hgrn2",
    )(u3, w_in, w_in, w_in, w_in, lb_params, g_norm)


def _out_proj0_kernel(hg_ref, og_ref, w_ref, pn_ref, x_ref, o_ref):
    half = hg_ref.shape[1]
    y = (jnp.dot(hg_ref[...], w_ref[:half, :], preferred_element_type=F32)
         + jnp.dot(og_ref[...], w_ref[half:, :], preferred_element_type=F32))
    o_ref[...] = x_ref[...] + _rms(y, pn_ref[...])


def _out_proj0(hg, og, w_bf16, post_gain, x2d, tm=1024):
    t, d = x2d.shape
    tok = lambda: pl.BlockSpec((tm, d), lambda i: (i, 0))
    return pl.pallas_call(
        _out_proj0_kernel,
        grid=(t // tm,),
        in_specs=[tok(), tok(),
                  pl.BlockSpec(w_bf16.shape, lambda i: (0, 0)),
                  pl.BlockSpec((1, d), lambda i: (0, 0)),
                  tok()],
        out_specs=tok(),
        out_shape=jax.ShapeDtypeStruct((t, d), F32),
        compiler_params=_params("parallel"),
        name="l0_out_proj",
    )(hg, og, w_bf16, post_gain, x2d)


def _rwkv_front_kernel(x_ref, pn_ref, mu_ref, win_ref, w0_ref, w1_ref, w2_ref, a0_ref, a1_ref,
                       a2_ref, kk_ref, ka_ref, ones_ref,
                       r_out, lw_out, k_out, v_out, kk_out, b_out, g_out, prev_ref):
    @pl.when(pl.program_id(1) == 0)
    def _():
        prev_ref[...] = jnp.zeros_like(prev_ref)

    u = _rms(x_ref[...], pn_ref[...])
    tm = u.shape[0]
    row = lax.broadcasted_iota(jnp.int32, (tm, 1), 0)
    delta = jnp.where(row == 0, prev_ref[...], pltpu.roll(u, 1, 0)) - u
    prev_ref[...] = u[tm - 1:tm, :]
    mix = lambda p: u + delta * mu_ref[p:p + 1, :]

    r_out[...] = _dot(mix(0), win_ref[0]).astype(r_out.dtype)
    k = _dot(mix(1), win_ref[1])
    v_out[...] = _dot(mix(2), win_ref[2]).astype(v_out.dtype)
    g_out[...] = _dot(mix(3), win_ref[3]).astype(g_out.dtype)
    lora_w = _dot(jnp.tanh(_dot(mix(4), w1_ref[...])), w2_ref[...])
    lora_a = _dot(_dot(mix(5), a1_ref[...]), a2_ref[...])
    lw_out[...] = (-EXP_NEG_HALF) * _sigmoid(w0_ref[...] + lora_w)
    iclr = _sigmoid(a0_ref[...] + lora_a)
    k_out[...] = (k * (1.0 + (iclr - 1.0) * ka_ref[...])).astype(k_out.dtype)
    kk = k * kk_ref[...]
    kk = kk * lax.rsqrt(jnp.maximum(_seg_sum(kk * kk, ones_ref[...]), 1e-24))
    kk_out[...] = kk.astype(kk_out.dtype)
    b_out[...] = (kk * iclr).astype(b_out.dtype)


def _rwkv_front(x1, pre_gain, mu, win, w0, w1, w2, a0, a1, a2, k_k, k_a, ones_bd, tm=512):
    bsz, seq, d = x1.shape
    tok = lambda: pl.BlockSpec((None, tm, d), lambda b, s: (b, s, 0))
    full = lambda a: pl.BlockSpec(a.shape, lambda b, s: (0,) * a.ndim)
    args = (pre_gain, mu, win, w0, w1, w2, a0, a1, a2, k_k, k_a, ones_bd)
    return pl.pallas_call(
        _rwkv_front_kernel,
        grid=(bsz, seq // tm),
        in_specs=[tok()] + [full(a) for a in args],
        out_specs=[tok() for _ in range(7)],
        out_shape=[jax.ShapeDtypeStruct((bsz, seq, d), F32 if i == 1 else BF16) for i in range(7)],
        scratch_shapes=[pltpu.VMEM((1, d), F32)],
        compiler_params=_params("parallel", "arbitrary"),
        name="l1_front",
    )(x1, *args)


def _rwkv_scan_kernel(r_ref, lw_ref, k_ref, v_ref, kk_ref, b_ref, y_ref,
                      ht_ref, rt_s, y0_s, ml_s, ha_s, gm_s):
    @pl.when(pl.program_id(2) == 0)
    def _():
        ht_ref[...] = jnp.zeros_like(ht_ref)

    ts, width = r_ref.shape
    nchunk = ts // CHUNK
    npair = width // LANES
    lanes_of = [slice(p * LANES, (p + 1) * LANES) for p in range(npair)]
    head0 = lax.broadcasted_iota(jnp.int32, (CHUNK, LANES), 1) < RW_HEAD
    row = lax.broadcasted_iota(jnp.int32, (CHUNK, LANES), 0)
    tok = lax.broadcasted_iota(jnp.int32, (CHUNK, LANES), 1) & (CHUNK - 1)
    strict = tok < row
    incl = tok <= row
    eye = jnp.where(tok == row, 1.0, 0.0)
    same_head = ((lax.broadcasted_iota(jnp.int32, (LANES, LANES), 0) < RW_HEAD)
                 == (lax.broadcasted_iota(jnp.int32, (LANES, LANES), 1) < RW_HEAD))

    def stack(x):
        return jnp.concatenate([jnp.where(head0, x, 0.0), jnp.where(head0, 0.0, x)], axis=0)

    def prep(c):
        rows = pl.ds(pl.multiple_of(c * CHUNK, CHUNK), CHUNK)
        r, lw, k, v, kk, b = (ref[rows, :].astype(F32)
                              for ref in (r_ref, lw_ref, k_ref, v_ref, kk_ref, b_ref))
        lcum = _cumsum_rows(lw)
        lend = lcum[CHUNK - 1:CHUNK, :]
        e_neg = jnp.exp(-lcum)
        e_end = jnp.exp(lend - lcum)
        gm_s[c] = jnp.exp(lend)
        at, rt = -kk * jnp.exp(lcum - lw), r * jnp.exp(lcum)
        bt, kt, bh, kh = b * e_neg, k * e_neg, b * e_end, k * e_end
        return [dict(rows=rows, c=c, p=p, sl=sl,
                     art=jnp.concatenate([at[:, sl], rt[:, sl]], axis=0).astype(BF16),
                     at2=stack(at[:, sl]).astype(BF16),
                     rt=rt[:, sl],
                     bk2=jnp.concatenate([stack(bt[:, sl]), stack(kt[:, sl])], axis=0).astype(BF16),
                     v=v[:, sl].astype(BF16),
                     v2=stack(v[:, sl]).astype(BF16),
                     bh=bh[:, sl].astype(BF16),
                     bkh=jnp.concatenate([bh[:, sl], kh[:, sl]], axis=0).astype(BF16))
                for p, sl in enumerate(lanes_of)]

    def local_group(i, between=()):
        pending = list(between)
        passed = [0]

        def boundary():
            fired = len(between) - len(pending)
            if pending and passed[0] * len(between) >= fired * SCAN_STAGES:
                pending.pop(0)()
            passed[0] += 1

        g = [q for j in range(SCAN_GROUP) for q in prep(i * SCAN_GROUP + j)]
        s = [_dot_nt(q["art"], q["bk2"]) for q in g]
        boundary()
        n = [jnp.where(strict, x[:CHUNK, :LANES], 0.0) for x in s]
        a_ak = [jnp.where(strict, x[:CHUNK, LANES:], 0.0).astype(BF16) for x in s]
        a_rbk = [jnp.concatenate([jnp.where(incl, x[CHUNK:, :LANES], 0.0),
                                  jnp.where(incl, x[CHUNK:, LANES:], 0.0)], axis=1).astype(BF16) for x in s]
        t = [eye + x for x in n]
        pw = [_dot(x, stack(x)) for x in n]
        boundary()
        d = 2
        while d < CHUNK // 2:
            m = [_dot(jnp.concatenate([a, b], axis=0), stack(b)) for a, b in zip(t, pw)]
            t = [a + x[:CHUNK] for a, x in zip(t, m)]
            pw = [x[CHUNK:] for x in m]
            boundary()
            d *= 2
        t = [a + _dot(a, stack(b)) for a, b in zip(t, pw)]
        boundary()
        akv = [_dot(x, q["v2"]) for x, q in zip(a_ak, g)]
        tw = [_dot(x, jnp.concatenate([q["at2"], stack(y).astype(BF16)], axis=1))
              for x, y, q in zip(t, akv, g)]
        boundary()
        wt = [x[:, :LANES] for x in tw]
        ut = [x[:, LANES:] for x in tw]
        rw = [_dot(x[:, :LANES], stack(y)) for x, y in zip(a_rbk, wt)]
        y0 = [_dot(x, jnp.concatenate([stack(y).astype(BF16), q["v2"]], axis=0))
              for x, y, q in zip(a_rbk, ut, g)]
        ml = [_dot_tn(q["bh"], y) for q, y in zip(g, wt)]
        ha = [_dot_tn(jnp.concatenate([y.astype(BF16), q["v"]], axis=0), q["bkh"]) for q, y in zip(g, ut)]
        while pending:
            pending.pop(0)()
        for j, q in enumerate(g):
            rt_s[q["rows"], q["sl"]] = (q["rt"] + rw[j]).astype(BF16)
            y0_s[q["rows"], q["sl"]] = y0[j]
            ml_s[q["c"], q["p"]] = jnp.where(same_head, ml[j], 0.0).astype(BF16)
            ha_s[q["c"], q["p"]] = jnp.where(same_head, ha[j], 0.0)

    def state_step(c, hts):
        rows = pl.ds(pl.multiple_of(c * CHUNK, CHUNK), CHUNK)
        rt, gm = rt_s[rows, :], gm_s[c]
        htb = [h.astype(BF16) for h in hts]
        ys = [_dot_nt(rt[:, sl], hb) for sl, hb in zip(lanes_of, htb)]
        new = [h * gm[:, sl] + _dot_nt(hb, ml_s[c, p]) + ha_s[c, p]
               for p, (sl, h, hb) in enumerate(zip(lanes_of, hts, htb))]
        y_ref[rows, :] = (jnp.concatenate(ys, axis=1) + y0_s[rows, :]).astype(y_ref.dtype)
        return tuple(new)

    def pipelined(i, hts):
        box = [hts]

        def advance(j):
            def run():
                box[0] = state_step((i - 1) * SCAN_GROUP + j, box[0])
            return run

        local_group(i, between=[advance(j) for j in range(SCAN_GROUP)])
        return box[0]

    ngroup = nchunk // SCAN_GROUP
    local_group(0)
    hts = lax.fori_loop(1, ngroup, pipelined, tuple(ht_ref[p] for p in range(npair)))
    for j in range(SCAN_GROUP):
        hts = state_step((ngroup - 1) * SCAN_GROUP + j, hts)
    for p in range(npair):
        ht_ref[p] = hts[p]


def _rwkv_scan(r, lw, k, v, kk, b, ts=1024, width=512):
    bsz, seq, d = r.shape
    nchunk = ts // CHUNK
    npair = width // LANES
    spec = lambda: pl.BlockSpec((None, ts, width), lambda bb, h, s: (bb, s, h))
    return pl.pallas_call(
        _rwkv_scan_kernel,
        grid=(bsz, d // width, seq // ts),
        in_specs=[spec() for _ in range(6)],
        out_specs=spec(),
        out_shape=jax.ShapeDtypeStruct((bsz, seq, d), BF16),
        scratch_shapes=[pltpu.VMEM((npair, LANES, LANES), F32),
                        pltpu.VMEM((ts, width), BF16),
                        pltpu.VMEM((ts, width), F32),
                        pltpu.VMEM((nchunk, npair, LANES, LANES), BF16),
                        pltpu.VMEM((nchunk, npair, LANES, LANES), F32),
                        pltpu.VMEM((nchunk, 1, width), F32)],
        compiler_params=_params("parallel", "parallel", "arbitrary"),
        name="l1_scan",
    )(r, lw, k, v, kk, b)


def _rwkv_back_kernel(y_ref, r_ref, k_ref, v_ref, g_ref, x_ref, rk_ref, lnw_ref, lnb_ref,
                      w_ref, pn_ref, ones_ref, o_ref):
    ones = ones_ref[...]
    inv_n = 1.0 / RW_HEAD
    y = y_ref[...].astype(F32)
    yc = y - _seg_sum(y, ones) * inv_n
    var = _seg_sum(yc * yc, ones) * inv_n
    yn = yc * lax.rsqrt(var + RW_GN_EPS) * lnw_ref[...] + lnb_ref[...]
    rk = r_ref[...].astype(F32) * k_ref[...].astype(F32) * rk_ref[...]
    yn = yn + _seg_sum(rk, ones) * v_ref[...].astype(F32)
    out = _dot(yn * _silu(g_ref[...].astype(F32)), w_ref[...])
    o_ref[...] = x_ref[...] + _rms(out, pn_ref[...])


def _rwkv_back(y, r, k, v, g, x1, r_k, ln_w, ln_b, w_bf16, post_gain, ones_bd, tm=512):
    t, d = y.shape
    tok = lambda: pl.BlockSpec((tm, d), lambda i: (i, 0))
    full = lambda a: pl.BlockSpec(a.shape, lambda i: (0,) * a.ndim)
    params = (r_k, ln_w, ln_b, w_bf16, post_gain, ones_bd)
    return pl.pallas_call(
        _rwkv_back_kernel,
        grid=(t // tm,),
        in_specs=[tok() for _ in range(6)] + [full(a) for a in params],
        out_specs=tok(),
        out_shape=jax.ShapeDtypeStruct((t, d), F32),
        compiler_params=_params("parallel"),
        name="l1_back",
    )(y, r, k, v, g, x1, *params)


def _block_diag(w, group):
    n, bi, bj = w.shape
    w = w.reshape(n // group, group, bi, bj)
    eye = jnp.eye(group, dtype=w.dtype)
    return jnp.einsum("gaij,ab->gaibj", w, eye).reshape(n // group, group * bi, group * bj)


def kernel(x, pre_norm, post_norm, ab_w_in, ab_w_out, rg_conv_w, rg_conv_b, rg_w_a, rg_b_a,
           rg_w_x, rg_b_x, rg_lambda, hg_lower_bounds, hg_out_norm, rw_mu, rw_w_in, rw_w_out,
           rw_w0, rw_w1, rw_w2, rw_a0, rw_a1, rw_a2, rw_k_k, rw_k_a, rw_r_k, rw_ln_w, rw_ln_b):
    bsz, seq, d = x.shape
    t = bsz * seq
    row = lambda a: a.reshape(1, -1).astype(F32)
    x2d = x.reshape(t, d)
    heads_per_tile = MXU_DIM // RW_HEAD
    ones_bd = _block_diag(jnp.ones((heads_per_tile, RW_HEAD, RW_HEAD), F32), heads_per_tile)[0].astype(BF16)

    w_in = ab_w_in[0]
    rg_splits = 2
    proj, u0 = _norm_proj(x2d, row(pre_norm[0]), w_in[:, :rg_splits * d].astype(BF16), rg_splits * d)
    proj4 = proj.reshape(proj.shape[0], bsz, seq, proj.shape[-1])
    rg_group = MXU_DIM // rg_w_a.shape[-1]
    hg = _rglru(proj4, rg_conv_w[0].astype(F32), row(rg_conv_b[0]),
                _block_diag(rg_w_a[0], rg_group).astype(BF16), row(rg_b_a[0]),
                _block_diag(rg_w_x[0], rg_group).astype(BF16), row(rg_b_x[0]), row(rg_lambda[0]))
    og = _hgrn2(u0.reshape(bsz, seq, d), w_in, rg_splits, hg_lower_bounds.astype(F32),
                row(hg_out_norm[0]))
    x1 = _out_proj0(hg.reshape(t, d), og.reshape(t, d), ab_w_out[0].astype(BF16),
                    row(post_norm[0]), x2d)

    r, lw, k, v, kk, b, g = _rwkv_front(
        x1.reshape(bsz, seq, d), row(pre_norm[1]), rw_mu[0].astype(F32), rw_w_in[0].astype(BF16),
        row(rw_w0[0]), rw_w1[0].astype(BF16), rw_w2[0].astype(BF16), row(rw_a0[0]),
        rw_a1[0].astype(BF16), rw_a2[0].astype(BF16), row(rw_k_k[0]), row(rw_k_a[0]), ones_bd)
    y = _rwkv_scan(r, lw, k, v, kk, b)
    flat = lambda a: a.reshape(t, d)
    x2 = _rwkv_back(flat(y), flat(r), flat(k), flat(v), flat(g), x1, row(rw_r_k[0]),
                    row(rw_ln_w[0]), row(rw_ln_b[0]), rw_w_out[0].astype(BF16),
                    row(post_norm[1]), ones_bd)
    return x2.reshape(bsz, seq, d)
```

```python
import jax
import jax.numpy as jnp
from jax import lax
from jax.experimental import pallas as pl
from jax.experimental.pallas import tpu as pltpu

F32 = jnp.float32
BF16 = jnp.bfloat16

D_MODEL = 1024
RMS_EPS = 1e-6
RG_C = 8.0
RG_CONV = 4
HG_HEADS = 8
HG_DK = 128
RW_HEAD = 64
RW_GN_EPS = 64e-5

LANES = 128
SUBLANES = 8
MXU_DIM = 256
VMEM_LIMIT = 56 * 1024 * 1024

CHUNK = 64
SCAN_GROUP = 4
HG_GROUP = 16
SCAN_STAGES = 8
EXP_NEG_HALF = 0.6065306597126334


def _params(*sem):
    return pltpu.CompilerParams(dimension_semantics=sem, vmem_limit_bytes=VMEM_LIMIT)


def _dot(a, b):
    return jnp.dot(a.astype(BF16), b.astype(BF16), preferred_element_type=F32)


def _dot_nt(a, b):
    return lax.dot_general(a.astype(BF16), b.astype(BF16), (((1,), (1,)), ((), ())),
                           preferred_element_type=F32)


def _dot_tn(a, b):
    return lax.dot_general(a.astype(BF16), b.astype(BF16), (((0,), (0,)), ((), ())),
                           preferred_element_type=F32)


def _sigmoid(x):
    return 0.5 * jnp.tanh(0.5 * x) + 0.5


def _silu(x):
    return x * _sigmoid(x)


def _softplus(z):
    return jnp.maximum(z, 0.0) + jnp.log(1.0 + jnp.exp(-jnp.abs(z)))


def _rms(x, gain):
    return x * lax.rsqrt(jnp.mean(x * x, axis=-1, keepdims=True) + RMS_EPS) * gain


def _cumsum_rows(x):
    n = x.shape[0]
    row = lax.broadcasted_iota(jnp.int32, x.shape, 0)
    d = 1
    while d < n:
        x = x + jnp.where(row >= d, pltpu.roll(x, d, 0), 0.0)
        d *= 2
    return x


def _seg_sum(x, ones_bd):
    xb = x.astype(BF16)
    return jnp.concatenate(
        [jnp.dot(xb[:, g * MXU_DIM:(g + 1) * MXU_DIM], ones_bd, preferred_element_type=F32)
         for g in range(x.shape[1] // MXU_DIM)], axis=-1)


def _norm_proj_kernel(x_ref, g_ref, w_ref, o_ref, u_ref):
    @pl.when(pl.program_id(1) == 0)
    def _():
        u_ref[...] = _rms(x_ref[...], g_ref[...]).astype(BF16)

    o_ref[...] = jnp.dot(u_ref[...], w_ref[...], preferred_element_type=F32).astype(o_ref.dtype)


def _norm_proj(x2d, gain, w_bf16, n, tm=1024, tn=2048):
    t, d = x2d.shape
    return pl.pallas_call(
        _norm_proj_kernel,
        grid=(t // tm, n // tn),
        in_specs=[pl.BlockSpec((tm, d), lambda i, j: (i, 0)),
                  pl.BlockSpec((1, d), lambda i, j: (0, 0)),
                  pl.BlockSpec((d, tn), lambda i, j: (0, j))],
        out_specs=[pl.BlockSpec((None, tm, tn), lambda i, j: (j, i, 0)),
                   pl.BlockSpec((tm, d), lambda i, j: (i, 0))],
        out_shape=[jax.ShapeDtypeStruct((n // tn, t, tn), BF16),
                   jax.ShapeDtypeStruct((t, d), BF16)],
        compiler_params=_params("parallel", "arbitrary"),
        name="l0_norm_proj",
    )(x2d, gain, w_bf16)


def _rglru_kernel(x_ref, g_ref, cw_ref, cb_ref, wa_ref, ba_ref, wx_ref, bx_ref, lam_ref,
                  o_ref, h_ref, ext_ref, a_ref, b_ref):
    @pl.when(pl.program_id(1) == 0)
    def _():
        h_ref[...] = jnp.zeros_like(h_ref)
        ext_ref[:SUBLANES, :] = jnp.zeros((SUBLANES, ext_ref.shape[1]), F32)

    x = x_ref[...].astype(F32)
    ts, width = x.shape
    ext_ref[SUBLANES:, :] = x
    xc = cb_ref[...] + cw_ref[RG_CONV - 1:RG_CONV, :] * x
    for k in range(1, RG_CONV):
        xc = xc + cw_ref[RG_CONV - 1 - k:RG_CONV - k, :] * ext_ref[SUBLANES - k:SUBLANES - k + ts, :]
    ext_ref[:SUBLANES, :] = x[ts - SUBLANES:, :]

    xcb = xc.astype(BF16)
    pre_r, pre_i = [], []
    for g in range(width // MXU_DIM):
        sl = slice(g * MXU_DIM, (g + 1) * MXU_DIM)
        pre_r.append(jnp.dot(xcb[:, sl], wa_ref[g], preferred_element_type=F32))
        pre_i.append(jnp.dot(xcb[:, sl], wx_ref[g], preferred_element_type=F32))
    gate_r = _sigmoid(jnp.concatenate(pre_r, axis=-1) + ba_ref[...])
    gate_i = _sigmoid(jnp.concatenate(pre_i, axis=-1) + bx_ref[...])

    log_a = (-RG_C) * gate_r * _softplus(-lam_ref[...])
    a = jnp.exp(log_a)
    b = jnp.sqrt(1.0 - a * a) * (gate_i * xc)

    a = a.reshape(ts // SUBLANES, SUBLANES, width)
    b = b.reshape(ts // SUBLANES, SUBLANES, width)
    in_group = lax.broadcasted_iota(jnp.int32, (1, SUBLANES, 1), 1)
    d = 1
    while d < SUBLANES:
        m = in_group >= d
        b = jnp.where(m, a, 0.0) * pltpu.roll(b, d, 1) + b
        a = a * jnp.where(m, pltpu.roll(a, d, 1), 1.0)
        d *= 2
    a_ref[...] = a.reshape(ts, width)
    b_ref[...] = b.reshape(ts, width)

    def carry_group(i, h):
        rows = pl.ds(pl.multiple_of(i * SUBLANES, SUBLANES), SUBLANES)
        hg = a_ref[rows, :] * h + b_ref[rows, :]
        b_ref[rows, :] = hg
        return hg[SUBLANES - 1:, :]

    h_ref[...] = lax.fori_loop(0, ts // SUBLANES, carry_group, h_ref[...], unroll=8)
    o_ref[...] = (b_ref[...] * _silu(g_ref[...].astype(F32))).astype(o_ref.dtype)


def _proj_block(proj4, split, block, offset=0):
    per_slab = proj4.shape[-1] // block
    col = split * (D_MODEL // block) + offset
    return col // per_slab, col % per_slab


def _rglru(proj4, conv_w, conv_b, wa_bd, b_a, wx_bd, b_x, lam, ts=512):
    _, bsz, seq, _ = proj4.shape
    width = D_MODEL
    row = lambda: pl.BlockSpec((1, width), lambda b, s: (0, 0))

    def split_spec(split):
        slab, blk = _proj_block(proj4, split, width)
        return pl.BlockSpec((None, None, ts, width), lambda b, s: (slab, b, s, blk))

    return pl.pallas_call(
        _rglru_kernel,
        grid=(bsz, seq // ts),
        in_specs=[split_spec(0), split_spec(1),
                  pl.BlockSpec((RG_CONV, width), lambda b, s: (0, 0)),
                  row(),
                  pl.BlockSpec(wa_bd.shape, lambda b, s: (0, 0, 0)),
                  row(),
                  pl.BlockSpec(wx_bd.shape, lambda b, s: (0, 0, 0)),
                  row(), row()],
        out_specs=pl.BlockSpec((None, ts, width), lambda b, s: (b, s, 0)),
        out_shape=jax.ShapeDtypeStruct((bsz, seq, width), BF16),
        scratch_shapes=[pltpu.VMEM((1, width), F32), pltpu.VMEM((SUBLANES + ts, width), F32),
                        pltpu.VMEM((ts, width), F32), pltpu.VMEM((ts, width), F32)],
        compiler_params=_params("parallel", "arbitrary"),
        name="l0_rglru",
    )(proj4, proj4, conv_w, conv_b, wa_bd, b_a, wx_bd, b_x, lam)


def _hgrn2_kernel(u_ref, wq_ref, wf_ref, wv_ref, wg_ref, lbp_ref, gn_ref, o_ref, st_ref, p_ref):
    @pl.when(pl.program_id(2) == 0)
    def _():
        st_ref[...] = jnp.zeros_like(st_ref)

    group_rows = CHUNK * HG_GROUP
    col = lambda j: slice(j * HG_DK, (j + 1) * HG_DK)
    w_halves = (jnp.concatenate([wq_ref[...].astype(BF16), wf_ref[...].astype(BF16)], axis=1),
                jnp.concatenate([wv_ref[...].astype(BF16), wg_ref[...].astype(BF16)], axis=1))

    def project(g, slot, half):
        rows = pl.ds(pl.multiple_of(g * group_rows, group_rows), group_rows)
        cols = slice(half * 2 * HG_DK, (half + 1) * 2 * HG_DK)
        p_ref[slot, :, cols] = jnp.dot(u_ref[rows, :], w_halves[half], preferred_element_type=F32)

    lbp = lbp_ref[...]
    e = jnp.exp(lbp - jnp.max(lbp, axis=0, keepdims=True))
    p = e / jnp.sum(e, axis=0, keepdims=True)
    lb = (p[0:1] + p[1:2]) - p[0:1]
    gn = gn_ref[...]
    ngroup = u_ref.shape[0] // group_rows
    srow = lax.broadcasted_iota(jnp.int32, (CHUNK, CHUNK), 0)
    scol = lax.broadcasted_iota(jnp.int32, (CHUNK, CHUNK), 1)
    levels = []
    h = CHUNK // 2
    while h >= 1:
        levels.append((h, (srow & ~(2 * h - 1)) == (scol & ~(2 * h - 1))))
        h //= 2
    sub_row = lax.broadcasted_iota(jnp.int32, (SUBLANES, 1), 0)
    chunk_row = lax.broadcasted_iota(jnp.int32, (CHUNK, 1), 0)
    tril3 = jnp.where(srow >= scol, 1.0, 0.0).astype(BF16)
    tril3 = jnp.concatenate([tril3, tril3, tril3], axis=1)

    def cumsum_rows(x):
        hi = x.astype(BF16)
        r1 = x - hi.astype(F32)
        mid = r1.astype(BF16)
        lo = (r1 - mid.astype(F32)).astype(BF16)
        return jnp.dot(tril3, jnp.concatenate([hi, mid, lo], axis=0), preferred_element_type=F32)

    def block_rows(x, size, r):
        out = []
        for lo in range(0, CHUNK, SUBLANES):
            acc = jnp.broadcast_to(x[lo + r:lo + r + 1], (SUBLANES, LANES))
            for off in range(size, SUBLANES, size):
                acc = jnp.where(sub_row >= off,
                                jnp.broadcast_to(x[lo + off + r:lo + off + r + 1], (SUBLANES, LANES)), acc)
            out.append(acc)
        return jnp.concatenate(out, axis=0)

    def level_factors(q, k, f, g2, h):
        if h < SUBLANES:
            upper = (chunk_row & (2 * h - 1)) >= h
            if h == 1:
                return jnp.where(upper, q * f, 0.0), jnp.where(upper, 0.0, k)
            d = g2 - block_rows(g2, 2 * h, h - 1)
            e = jnp.exp2(jnp.where(upper, d, -d))
            return jnp.where(upper, q * e, 0.0), jnp.where(upper, 0.0, k * e)
        zero = jnp.zeros((h, LANES), F32)
        qt, kt = [], []
        for lo in range(0, CHUNK, 2 * h):
            mid = lo + h
            rho = g2[mid - 1:mid, :]
            kt += [k[lo:mid] * jnp.exp2(rho - g2[lo:mid]), zero]
            qt += [zero, q[mid:mid + h] * jnp.exp2(g2[mid:mid + h] - rho)]
        return jnp.concatenate(qt, axis=0), jnp.concatenate(kt, axis=0)

    def group(i, carry):
        idx = range(HG_GROUP)
        slot = i % 2
        nxt = jnp.minimum(i + 1, ngroup - 1)
        local = [pl.ds(j * CHUNK, CHUNK) for j in idx]
        rows = [pl.ds(pl.multiple_of((i * HG_GROUP + j) * CHUNK, CHUNK), CHUNK) for j in idx]
        q = [_silu(p_ref[slot, r, col(0)]) for r in local]
        sig = [1.0 / (1.0 + jnp.exp(-p_ref[slot, r, col(1)])) for r in local]
        v = [p_ref[slot, r, col(2)].astype(BF16) for r in local]
        gate = [p_ref[slot, r, col(3)] for r in local]
        k = [(1.0 - lb) * (1.0 - x) for x in sig]
        f = [lb + (1.0 - lb) * x for x in sig]
        g2 = [cumsum_rows(jnp.log2(x)) for x in f]
        project(nxt, 1 - slot, 0)
        gend = [x[CHUNK - 1:CHUNK, :] for x in g2]
        qg = [(a * jnp.exp2(b)).astype(BF16) for a, b in zip(q, g2)]
        vtk = [_dot_tn(a, b * jnp.exp2(e - c)) for a, b, c, e in zip(v, k, g2, gend)]

        scores = [jnp.where(srow == scol, jnp.sum(a * b, axis=-1, keepdims=True), 0.0)
                  for a, b in zip(q, k)]
        for h, same_block in levels:
            for j in idx:
                qt, kt = level_factors(q[j], k[j], f[j], g2[j], h)
                scores[j] = scores[j] + jnp.where(same_block, _dot_nt(qt, kt), 0.0)
        out = [_dot(a, b) for a, b in zip(scores, v)]

        st = st_ref[...]
        for j in idx:
            out[j] = out[j] + _dot_nt(qg[j], st)
            st = st * jnp.exp2(gend[j]) + vtk[j]
        st_ref[...] = st
        project(nxt, 1 - slot, 1)
        for j in idx:
            o_ref[rows[j], :] = (_rms(out[j], gn) * _silu(gate[j])).astype(o_ref.dtype)
        return carry

    project(0, 0, 0)
    project(0, 0, 1)
    lax.fori_loop(0, ngroup, group, 0)


def _hgrn2(u3, w_in, first_split, lb_params, g_norm, ts=8192):
    bsz, seq, d = u3.shape

    def w_spec(split):
        base = split * (d // HG_DK)
        return pl.BlockSpec((d, HG_DK), lambda b, h, s: (0, base + h))

    return pl.pallas_call(
        _hgrn2_kernel,
        grid=(bsz, HG_HEADS, seq // ts),
        in_specs=[pl.BlockSpec((None, ts, d), lambda b, h, s: (b, s, 0))]
        + [w_spec(first_split + j) for j in range(4)]
        + [pl.BlockSpec((lb_params.shape[0], HG_DK), lambda b, h, s: (0, h)),
           pl.BlockSpec((1, HG_DK), lambda b, h, s: (0, 0))],
        out_specs=pl.BlockSpec((None, ts, HG_DK), lambda b, h, s: (b, s, h)),
        out_shape=jax.ShapeDtypeStruct((bsz, seq, d), BF16),
        scratch_shapes=[pltpu.VMEM((HG_DK, HG_DK), F32),
                        pltpu.VMEM((2, CHUNK * HG_GROUP, 4 * HG_DK), F32)],
        compiler_params=_params("parallel", "parallel", "arbitrary"),
        name="l0_hgrn2",
    )(u3, w_in, w_in, w_in, w_in, lb_params, g_norm)


def _out_proj0_kernel(hg_ref, og_ref, w_ref, pn_ref, x_ref, o_ref):
    half = hg_ref.shape[1]
    y = (jnp.dot(hg_ref[...], w_ref[:half, :], preferred_element_type=F32)
         + jnp.dot(og_ref[...], w_ref[half:, :], preferred_element_type=F32))
    o_ref[...] = x_ref[...] + _rms(y, pn_ref[...])


def _out_proj0(hg, og, w_bf16, post_gain, x2d, tm=1024):
    t, d = x2d.shape
    tok = lambda: pl.BlockSpec((tm, d), lambda i: (i, 0))
    return pl.pallas_call(
        _out_proj0_kernel,
        grid=(t // tm,),
        in_specs=[tok(), tok(),
                  pl.BlockSpec(w_bf16.shape, lambda i: (0, 0)),
                  pl.BlockSpec((1, d), lambda i: (0, 0)),
                  tok()],
        out_specs=tok(),
        out_shape=jax.ShapeDtypeStruct((t, d), F32),
        compiler_params=_params("parallel"),
        name="l0_out_proj",
    )(hg, og, w_bf16, post_gain, x2d)


def _rwkv_front_kernel(x_ref, pn_ref, mu_ref, win_ref, w0_ref, w1_ref, w2_ref, a0_ref, a1_ref,
                       a2_ref, kk_ref, ka_ref, ones_ref,
                       r_out, lw_out, k_out, v_out, kk_out, b_out, g_out, prev_ref):
    @pl.when(pl.program_id(1) == 0)
    def _():
        prev_ref[...] = jnp.zeros_like(prev_ref)

    u = _rms(x_ref[...], pn_ref[...])
    tm = u.shape[0]
    row = lax.broadcasted_iota(jnp.int32, (tm, 1), 0)
    delta = jnp.where(row == 0, prev_ref[...], pltpu.roll(u, 1, 0)) - u
    prev_ref[...] = u[tm - 1:tm, :]
    mix = lambda p: u + delta * mu_ref[p:p + 1, :]

    r_out[...] = _dot(mix(0), win_ref[0]).astype(r_out.dtype)
    k = _dot(mix(1), win_ref[1])
    v_out[...] = _dot(mix(2), win_ref[2]).astype(v_out.dtype)
    g_out[...] = _dot(mix(3), win_ref[3]).astype(g_out.dtype)
    lora_w = _dot(jnp.tanh(_dot(mix(4), w1_ref[...])), w2_ref[...])
    lora_a = _dot(_dot(mix(5), a1_ref[...]), a2_ref[...])
    lw_out[...] = (-EXP_NEG_HALF) * _sigmoid(w0_ref[...] + lora_w)
    iclr = _sigmoid(a0_ref[...] + lora_a)
    k_out[...] = (k * (1.0 + (iclr - 1.0) * ka_ref[...])).astype(k_out.dtype)
    kk = k * kk_ref[...]
    kk = kk * lax.rsqrt(jnp.maximum(_seg_sum(kk * kk, ones_ref[...]), 1e-24))
    kk_out[...] = kk.astype(kk_out.dtype)
    b_out[...] = (kk * iclr).astype(b_out.dtype)


def _rwkv_front(x1, pre_gain, mu, win, w0, w1, w2, a0, a1, a2, k_k, k_a, ones_bd, tm=512):
    bsz, seq, d = x1.shape
    tok = lambda: pl.BlockSpec((None, tm, d), lambda b, s: (b, s, 0))
    full = lambda a: pl.BlockSpec(a.shape, lambda b, s: (0,) * a.ndim)
    args = (pre_gain, mu, win, w0, w1, w2, a0, a1, a2, k_k, k_a, ones_bd)
    return pl.pallas_call(
        _rwkv_front_kernel,
        grid=(bsz, seq // tm),
        in_specs=[tok()] + [full(a) for a in args],
        out_specs=[tok() for _ in range(7)],
        out_shape=[jax.ShapeDtypeStruct((bsz, seq, d), F32 if i == 1 else BF16) for i in range(7)],
        scratch_shapes=[pltpu.VMEM((1, d), F32)],
        compiler_params=_params("parallel", "arbitrary"),
        name="l1_front",
    )(x1, *args)


def _rwkv_scan_kernel(r_ref, lw_ref, k_ref, v_ref, kk_ref, b_ref, y_ref,
                      ht_ref, rt_s, y0_s, ml_s, ha_s, gm_s):
    @pl.when(pl.program_id(2) == 0)
    def _():
        ht_ref[...] = jnp.zeros_like(ht_ref)

    ts, width = r_ref.shape
    nchunk = ts // CHUNK
    npair = width // LANES
    lanes_of = [slice(p * LANES, (p + 1) * LANES) for p in range(npair)]
    head0 = lax.broadcasted_iota(jnp.int32, (CHUNK, LANES), 1) < RW_HEAD
    row = lax.broadcasted_iota(jnp.int32, (CHUNK, LANES), 0)
    tok = lax.broadcasted_iota(jnp.int32, (CHUNK, LANES), 1) & (CHUNK - 1)
    strict = tok < row
    incl = tok <= row
    eye = jnp.where(tok == row, 1.0, 0.0)
    same_head = ((lax.broadcasted_iota(jnp.int32, (LANES, LANES), 0) < RW_HEAD)
                 == (lax.broadcasted_iota(jnp.int32, (LANES, LANES), 1) < RW_HEAD))

    def stack(x):
        return jnp.concatenate([jnp.where(head0, x, 0.0), jnp.where(head0, 0.0, x)], axis=0)

    def prep(c):
        rows = pl.ds(pl.multiple_of(c * CHUNK, CHUNK), CHUNK)
        r, lw, k, v, kk, b = (ref[rows, :].astype(F32)
                              for ref in (r_ref, lw_ref, k_ref, v_ref, kk_ref, b_ref))
        lcum = _cumsum_rows(lw)
        lend = lcum[CHUNK - 1:CHUNK, :]
        e_neg = jnp.exp(-lcum)
        e_end = jnp.exp(lend - lcum)
        gm_s[c] = jnp.exp(lend)
        at, rt = -kk * jnp.exp(lcum - lw), r * jnp.exp(lcum)
        bt, kt, bh, kh = b * e_neg, k * e_neg, b * e_end, k * e_end
        return [dict(rows=rows, c=c, p=p, sl=sl,
                     art=jnp.concatenate([at[:, sl], rt[:, sl]], axis=0).astype(BF16),
                     at2=stack(at[:, sl]).astype(BF16),
                     rt=rt[:, sl],
                     bk2=jnp.concatenate([stack(bt[:, sl]), stack(kt[:, sl])], axis=0).astype(BF16),
                     v=v[:, sl].astype(BF16),
                     v2=stack(v[:, sl]).astype(BF16),
                     bh=bh[:, sl].astype(BF16),
                     bkh=jnp.concatenate([bh[:, sl], kh[:, sl]], axis=0).astype(BF16))
                for p, sl in enumerate(lanes_of)]

    def local_group(i, between=()):
        pending = list(between)
        passed = [0]

        def boundary():
            fired = len(between) - len(pending)
            if pending and passed[0] * len(between) >= fired * SCAN_STAGES:
                pending.pop(0)()
            passed[0] += 1

        g = [q for j in range(SCAN_GROUP) for q in prep(i * SCAN_GROUP + j)]
        s = [_dot_nt(q["art"], q["bk2"]) for q in g]
        boundary()
        n = [jnp.where(strict, x[:CHUNK, :LANES], 0.0) for x in s]
        a_ak = [jnp.where(strict, x[:CHUNK, LANES:], 0.0).astype(BF16) for x in s]
        a_rbk = [jnp.concatenate([jnp.where(incl, x[CHUNK:, :LANES], 0.0),
                                  jnp.where(incl, x[CHUNK:, LANES:], 0.0)], axis=1).astype(BF16) for x in s]
        t = [eye + x for x in n]
        pw = [_dot(x, stack(x)) for x in n]
        boundary()
        d = 2
        while d < CHUNK // 2:
            m = [_dot(jnp.concatenate([a, b], axis=0), stack(b)) for a, b in zip(t, pw)]
            t = [a + x[:CHUNK] for a, x in zip(t, m)]
            pw = [x[CHUNK:] for x in m]
            boundary()
            d *= 2
        t = [a + _dot(a, stack(b)) for a, b in zip(t, pw)]
        boundary()
        akv = [_dot(x, q["v2"]) for x, q in zip(a_ak, g)]
        tw = [_dot(x, jnp.concatenate([q["at2"], stack(y).astype(BF16)], axis=1))
              for x, y, q in zip(t, akv, g)]
        boundary()
        wt = [x[:, :LANES] for x in tw]
        ut = [x[:, LANES:] for x in tw]
        rw = [_dot(x[:, :LANES], stack(y)) for x, y in zip(a_rbk, wt)]
        y0 = [_dot(x, jnp.concatenate([stack(y).astype(BF16), q["v2"]], axis=0))
              for x, y, q in zip(a_rbk, ut, g)]
        ml = [_dot_tn(q["bh"], y) for q, y in zip(g, wt)]
        ha = [_dot_tn(jnp.concatenate([y.astype(BF16), q["v"]], axis=0), q["bkh"]) for q, y in zip(g, ut)]
        while pending:
            pending.pop(0)()
        for j, q in enumerate(g):
            rt_s[q["rows"], q["sl"]] = (q["rt"] + rw[j]).astype(BF16)
            y0_s[q["rows"], q["sl"]] = y0[j]
            ml_s[q["c"], q["p"]] = jnp.where(same_head, ml[j], 0.0).astype(BF16)
            ha_s[q["c"], q["p"]] = jnp.where(same_head, ha[j], 0.0)

    def state_step(c, hts):
        rows = pl.ds(pl.multiple_of(c * CHUNK, CHUNK), CHUNK)
        rt, gm = rt_s[rows, :], gm_s[c]
        htb = [h.astype(BF16) for h in hts]
        ys = [_dot_nt(rt[:, sl], hb) for sl, hb in zip(lanes_of, htb)]
        new = [h * gm[:, sl] + _dot_nt(hb, ml_s[c, p]) + ha_s[c, p]
               for p, (sl, h, hb) in enumerate(zip(lanes_of, hts, htb))]
        y_ref[rows, :] = (jnp.concatenate(ys, axis=1) + y0_s[rows, :]).astype(y_ref.dtype)
        return tuple(new)

    def pipelined(i, hts):
        box = [hts]

        def advance(j):
            def run():
                box[0] = state_step((i - 1) * SCAN_GROUP + j, box[0])
            return run

        local_group(i, between=[advance(j) for j in range(SCAN_GROUP)])
        return box[0]

    ngroup = nchunk // SCAN_GROUP
    local_group(0)
    hts = lax.fori_loop(1, ngroup, pipelined, tuple(ht_ref[p] for p in range(npair)))
    for j in range(SCAN_GROUP):
        hts = state_step((ngroup - 1) * SCAN_GROUP + j, hts)
    for p in range(npair):
        ht_ref[p] = hts[p]


def _rwkv_scan(r, lw, k, v, kk, b, ts=1024, width=512):
    bsz, seq, d = r.shape
    nchunk = ts // CHUNK
    npair = width // LANES
    spec = lambda: pl.BlockSpec((None, ts, width), lambda bb, h, s: (bb, s, h))
    return pl.pallas_call(
        _rwkv_scan_kernel,
        grid=(bsz, d // width, seq // ts),
        in_specs=[spec() for _ in range(6)],
        out_specs=spec(),
        out_shape=jax.ShapeDtypeStruct((bsz, seq, d), BF16),
        scratch_shapes=[pltpu.VMEM((npair, LANES, LANES), F32),
                        pltpu.VMEM((ts, width), BF16),
                        pltpu.VMEM((ts, width), F32),
                        pltpu.VMEM((nchunk, npair, LANES, LANES), BF16),
                        pltpu.VMEM((nchunk, npair, LANES, LANES), F32),
                        pltpu.VMEM((nchunk, 1, width), F32)],
        compiler_params=_params("parallel", "parallel", "arbitrary"),
        name="l1_scan",
    )(r, lw, k, v, kk, b)


def _rwkv_back_kernel(y_ref, r_ref, k_ref, v_ref, g_ref, x_ref, rk_ref, lnw_ref, lnb_ref,
                      w_ref, pn_ref, ones_ref, o_ref):
    ones = ones_ref[...]
    inv_n = 1.0 / RW_HEAD
    y = y_ref[...].astype(F32)
    yc = y - _seg_sum(y, ones) * inv_n
    var = _seg_sum(yc * yc, ones) * inv_n
    yn = yc * lax.rsqrt(var + RW_GN_EPS) * lnw_ref[...] + lnb_ref[...]
    rk = (r_ref[...] * k_ref[...]).astype(F32) * rk_ref[...]
    yn = yn + _seg_sum(rk, ones) * v_ref[...].astype(F32)
    out = _dot(yn * _silu(g_ref[...].astype(F32)), w_ref[...])
    o_ref[...] = x_ref[...] + _rms(out, pn_ref[...])


def _rwkv_back(y, r, k, v, g, x1, r_k, ln_w, ln_b, w_bf16, post_gain, ones_bd, tm=512):
    t, d = y.shape
    tok = lambda: pl.BlockSpec((tm, d), lambda i: (i, 0))
    full = lambda a: pl.BlockSpec(a.shape, lambda i: (0,) * a.ndim)
    params = (r_k, ln_w, ln_b, w_bf16, post_gain, ones_bd)
    return pl.pallas_call(
        _rwkv_back_kernel,
        grid=(t // tm,),
        in_specs=[tok() for _ in range(6)] + [full(a) for a in params],
        out_specs=tok(),
        out_shape=jax.ShapeDtypeStruct((t, d), F32),
        compiler_params=_params("parallel"),
        name="l1_back",
    )(y, r, k, v, g, x1, *params)


def _block_diag(w, group):
    n, bi, bj = w.shape
    w = w.reshape(n // group, group, bi, bj)
    eye = jnp.eye(group, dtype=w.dtype)
    return jnp.einsum("gaij,ab->gaibj", w, eye).reshape(n // group, group * bi, group * bj)


def kernel(x, pre_norm, post_norm, ab_w_in, ab_w_out, rg_conv_w, rg_conv_b, rg_w_a, rg_b_a,
           rg_w_x, rg_b_x, rg_lambda, hg_lower_bounds, hg_out_norm, rw_mu, rw_w_in, rw_w_out,
           rw_w0, rw_w1, rw_w2, rw_a0, rw_a1, rw_a2, rw_k_k, rw_k_a, rw_r_k, rw_ln_w, rw_ln_b):
    bsz, seq, d = x.shape
    t = bsz * seq
    row = lambda a: a.reshape(1, -1).astype(F32)
    x2d = x.reshape(t, d)
    heads_per_tile = MXU_DIM // RW_HEAD
    ones_bd = _block_diag(jnp.ones((heads_per_tile, RW_HEAD, RW_HEAD), F32), heads_per_tile)[0].astype(BF16)

    w_in = ab_w_in[0]
    rg_splits = 2
    proj, u0 = _norm_proj(x2d, row(pre_norm[0]), w_in[:, :rg_splits * d].astype(BF16), rg_splits * d)
    proj4 = proj.reshape(proj.shape[0], bsz, seq, proj.shape[-1])
    rg_group = MXU_DIM // rg_w_a.shape[-1]
    hg = _rglru(proj4, rg_conv_w[0].astype(F32), row(rg_conv_b[0]),
                _block_diag(rg_w_a[0], rg_group).astype(BF16), row(rg_b_a[0]),
                _block_diag(rg_w_x[0], rg_group).astype(BF16), row(rg_b_x[0]), row(rg_lambda[0]))
    og = _hgrn2(u0.reshape(bsz, seq, d), w_in, rg_splits, hg_lower_bounds.astype(F32),
                row(hg_out_norm[0]))
    x1 = _out_proj0(hg.reshape(t, d), og.reshape(t, d), ab_w_out[0].astype(BF16),
                    row(post_norm[0]), x2d)

    r, lw, k, v, kk, b, g = _rwkv_front(
        x1.reshape(bsz, seq, d), row(pre_norm[1]), rw_mu[0].astype(F32), rw_w_in[0].astype(BF16),
        row(rw_w0[0]), rw_w1[0].astype(BF16), rw_w2[0].astype(BF16), row(rw_a0[0]),
        rw_a1[0].astype(BF16), rw_a2[0].astype(BF16), row(rw_k_k[0]), row(rw_k_a[0]), ones_bd)
    y = _rwkv_scan(r, lw, k, v, kk, b)
    flat = lambda a: a.reshape(t, d)
    x2 = _rwkv_back(flat(y), flat(r), flat(k), flat(v), flat(g), x1, row(rw_r_k[0]),
                    row(rw_ln_w[0]), row(rw_ln_b[0]), rw_w_out[0].astype(BF16),
                    row(post_norm[1]), ones_bd)
    return x2.reshape(bsz, seq, d)
```
